```python
import jax, jax.numpy as jnp
from jax import lax
import numpy as np

D_MODEL = 2048
BATCH = 4
SEQ = 2048
DEPTH = 1

PLE_DIM = 256
HG_HEADS = 8
HG_DK = 128
HG_DV = 128
HG_WIDTH = HG_HEADS * HG_DK
HG_CHUNK = 64
AT_HEADS = 8
AT_HEAD_DIM = 128
AT_WIDTH = AT_HEADS * AT_HEAD_DIM
IDX_HEADS = 16
IDX_DIM = 64
TOPK_MAX = 256
Q_BLOCK = 128
N_GROUPS = 4
EXPERTS_PER_GROUP = 8
N_EXPERTS = N_GROUPS * EXPERTS_PER_GROUP
TOP_K_IN_GROUP = 2
D_EXPERT = 512
RMS_EPS = 1e-6
SPLITS = (HG_WIDTH, HG_WIDTH, HG_HEADS * HG_DV, HG_HEADS * HG_DV,
          AT_WIDTH, AT_WIDTH, AT_WIDTH,
          IDX_HEADS * IDX_DIM, IDX_DIM, IDX_HEADS,
          D_MODEL, D_MODEL)
D_IN = sum(SPLITS)

kernel_name = 'hybrid_hgrn2_dsa_hmoe_block'


def rms_norm(x, g):
    x32 = x.astype(jnp.float32)
    y = x32 * lax.rsqrt(jnp.mean(x32 * x32, axis=-1, keepdims=True) + RMS_EPS)
    return (y * g.astype(jnp.float32)).astype(x.dtype)


def hgrn2_mix(q, f_raw, i, lb):
    B, T, H, DK = q.shape
    DV = i.shape[-1]
    C = HG_CHUNK
    N = T // C
    f32 = jnp.float32
    z = f_raw.astype(f32)
    lb = lb.astype(f32)
    q32 = jax.nn.silu(q.astype(f32))
    logf = jnp.logaddexp(jnp.log(lb), jnp.log1p(-lb) + jax.nn.log_sigmoid(z))
    k32 = (1.0 - lb) * jax.nn.sigmoid(-z)

    def to_chunks(a):
        return a.reshape(B, N, C, H, a.shape[-1]).transpose(1, 0, 3, 2, 4)

    qc, kc, vc, lc = (to_chunks(a) for a in (q32, k32, i.astype(f32), logf))
    bc = jnp.cumsum(lc, axis=3)
    causal = jnp.tril(jnp.ones((C, C), dtype=bool))

    def step(S, inp):
        qn, kn, vn, bn = inp
        rel = bn[:, :, :, None, :] - bn[:, :, None, :, :]
        decay = jnp.exp(jnp.where(causal[:, :, None], rel, -jnp.inf))
        A = jnp.einsum('bhtsd,bhsd->bhts', qn[:, :, :, None, :] * decay, kn)
        o = (jnp.einsum('bhts,bhsv->bhtv', A, vn)
             + jnp.einsum('bhtd,bhdv->bhtv', qn * jnp.exp(bn), S))
        blast = bn[:, :, -1, :]
        S = (jnp.exp(blast)[..., None] * S
             + jnp.einsum('bhsd,bhsv->bhdv', kn * jnp.exp(blast[:, :, None, :] - bn), vn))
        return S, o

    S0 = jnp.zeros((B, H, DK, DV), f32)
    _, o = lax.scan(step, S0, (qc, kc, vc, bc))
    return o.transpose(1, 0, 3, 2, 4).reshape(B, T, H, DV)


def dsa_attention(q, k, v, iq, ik, iw):
    B, T, H, Dh = q.shape
    ksel = min(TOPK_MAX, T // 4)
    nb = T // Q_BLOCK
    f32 = jnp.float32
    scale = Dh ** -0.5
    slopes = jnp.exp2(-8.0 * jnp.arange(1, H + 1, dtype=f32) / H)
    iw = iw.astype(f32) * (IDX_HEADS ** -0.5) * (IDX_DIM ** -0.5)
    ik32 = ik.astype(f32)
    spos = jnp.arange(T, dtype=jnp.int32)

    def blocks(a):
        return a.reshape(B, nb, Q_BLOCK, *a.shape[2:]).swapaxes(0, 1)

    def one_block(inp):
        qb, iqb, iwb, start = inp
        tpos = start + jnp.arange(Q_BLOCK, dtype=jnp.int32)
        rel = jax.nn.relu(jnp.einsum('bthd,bsd->bths', iqb.astype(f32), ik32))
        score = jnp.einsum('bths,bth->bts', rel, iwb)
        visible = spos[None, :] <= tpos[:, None]
        score = jnp.where(visible[None], score, -jnp.inf)
        _, idx = lax.top_k(score, ksel)
        valid = idx <= tpos[None, :, None]
        kg = jax.vmap(lambda kb, ib: kb[ib])(k, idx)
        vg = jax.vmap(lambda vb, ib: vb[ib])(v, idx)
        logits = jnp.einsum('bthd,btkhd->bthk', qb, kg).astype(f32) * scale
        dist = (tpos[None, :, None] - idx).astype(f32)
        logits = logits - slopes[None, None, :, None] * dist[:, :, None, :]
        logits = jnp.where(valid[:, :, None, :], logits, -jnp.inf)
        probs = jax.nn.softmax(logits, axis=-1).astype(vg.dtype)
        return jnp.einsum('bthk,btkhd->bthd', probs, vg)

    starts = jnp.arange(nb, dtype=jnp.int32) * Q_BLOCK
    out = lax.map(one_block, (blocks(q), blocks(iq), blocks(iw), starts))
    return out.swapaxes(0, 1).reshape(B, T, H, Dh)


def hier_moe(h, w_rg, b_rg, w_re, b_re, w_g, w_u, w_d):
    B, T, D = h.shape
    f32 = jnp.float32
    hf = h.reshape(-1, D)
    n = hf.shape[0]
    pg = jax.nn.softmax((hf @ w_rg + b_rg).astype(f32), axis=-1)
    pg_top, gid = lax.top_k(pg, 1)
    elog = (hf @ w_re + b_re).astype(f32).reshape(n, N_GROUPS, EXPERTS_PER_GROUP)
    gidx = jnp.broadcast_to(gid[:, :, None], (n, 1, EXPERTS_PER_GROUP))
    elog_sel = jnp.take_along_axis(elog, gidx, axis=1)[:, 0]
    pe = jax.nn.softmax(elog_sel, axis=-1)
    pe_top, eidx = lax.top_k(pe, TOP_K_IN_GROUP)
    wts = pg_top * pe_top / jnp.sum(pe_top, axis=-1, keepdims=True)
    eid = gid * EXPERTS_PER_GROUP + eidx
    comb = jnp.sum(jax.nn.one_hot(eid, N_EXPERTS, dtype=f32) * wts[..., None], axis=1)
    out = jnp.zeros((n, D), h.dtype)
    for g in range(N_GROUPS):
        sl = slice(g * EXPERTS_PER_GROUP, (g + 1) * EXPERTS_PER_GROUP)
        hid = (jax.nn.silu(jnp.einsum('nd,edf->nef', hf, w_g[sl]))
               * jnp.einsum('nd,edf->nef', hf, w_u[sl]))
        hid = hid * comb[:, sl, None].astype(hid.dtype)
        out = out + jnp.einsum('nef,efd->nd', hid, w_d[sl])
    return out.reshape(B, T, D)


def setup_inputs(seed: int = 0) -> dict:
    key = jax.random.key(seed)
    ks = jax.random.split(key, 24)
    nrm = jax.random.normal
    f32 = jnp.float32
    L = DEPTH
    return {
        'x': nrm(ks[0], (BATCH, SEQ, D_MODEL), f32),
        'p': nrm(ks[1], (DEPTH, BATCH, SEQ, PLE_DIM), f32),
        'attn_norm': 1.0 + 0.02 * nrm(ks[2], (L, D_MODEL), f32),
        'w_in': nrm(ks[3], (L, D_MODEL, D_IN), f32) * D_MODEL ** -0.5,
        'b_in': 0.01 * nrm(ks[4], (L, D_IN), f32),
        'hg_lb_logits': nrm(ks[5], (L + 1, HG_WIDTH), f32),
        'hg_out_norm': 1.0 + 0.02 * nrm(ks[6], (L, HG_DV), f32),
        'w_hg_up': nrm(ks[7], (L, HG_HEADS * HG_DV, D_MODEL), f32) * (HG_HEADS * HG_DV) ** -0.5,
        'w_attn_up': nrm(ks[8], (L, AT_WIDTH, D_MODEL), f32) * AT_WIDTH ** -0.5,
        'w_out': nrm(ks[9], (L, D_MODEL, D_MODEL), f32) * D_MODEL ** -0.5,
        'ffn_norm': 1.0 + 0.02 * nrm(ks[10], (L, D_MODEL), f32),
        'w_router_group': nrm(ks[11], (L, D_MODEL, N_GROUPS), f32) * D_MODEL ** -0.5,
        'b_router_group': 0.01 * nrm(ks[12], (L, N_GROUPS), f32),
        'w_router_expert': nrm(ks[13], (L, D_MODEL, N_EXPERTS), f32) * D_MODEL ** -0.5,
        'b_router_expert': 0.01 * nrm(ks[14], (L, N_EXPERTS), f32),
        'w_exp_gate': nrm(ks[15], (L, N_EXPERTS, D_MODEL, D_EXPERT), f32) * D_MODEL ** -0.5,
        'w_exp_up': nrm(ks[16], (L, N_EXPERTS, D_MODEL, D_EXPERT), f32) * D_MODEL ** -0.5,
        'w_exp_down': nrm(ks[17], (L, N_EXPERTS, D_EXPERT, D_MODEL), f32) * D_EXPERT ** -0.5,
        'ple_norm': 1.0 + 0.02 * nrm(ks[18], (L, D_MODEL), f32),
        'w_ple_gate': nrm(ks[19], (L, D_MODEL, D_MODEL), f32) * D_MODEL ** -0.5,
        'w_ple_proj': nrm(ks[20], (L, PLE_DIM, D_MODEL), f32) * PLE_DIM ** -0.5,
        'final_norm': 1.0 + 0.02 * nrm(ks[21], (D_MODEL,), f32),
    }


def reference(x, p, attn_norm, w_in, b_in, hg_lb_logits, hg_out_norm, w_hg_up, w_attn_up, w_out,
              ffn_norm, w_router_group, b_router_group, w_router_expert, b_router_expert,
              w_exp_gate, w_exp_up, w_exp_down, ple_norm, w_ple_gate, w_ple_proj, final_norm):
    B, T, D = x.shape
    offsets = [int(o) for o in np.cumsum(SPLITS)[:-1]]
    lbs = jnp.cumsum(jax.nn.softmax(hg_lb_logits.astype(jnp.float32), axis=0), axis=0)
    for i in range(DEPTH):
        h = rms_norm(x, attn_norm[i])
        zin = h @ w_in[i] + b_in[i]
        (hq, hf, hi, hgo, aq, ak, av, iq, ik, iw, ga, gb) = jnp.split(zin, offsets, axis=-1)
        o_hg = hgrn2_mix(hq.reshape(B, T, HG_HEADS, HG_DK), hf.reshape(B, T, HG_HEADS, HG_DK),
                         hi.reshape(B, T, HG_HEADS, HG_DV), lbs[i].reshape(HG_HEADS, HG_DK))
        o_hg = rms_norm(o_hg.astype(x.dtype), hg_out_norm[i]) * jax.nn.silu(hgo.reshape(B, T, HG_HEADS, HG_DV))
        y_hg = o_hg.reshape(B, T, HG_HEADS * HG_DV) @ w_hg_up[i]
        o_at = dsa_attention(aq.reshape(B, T, AT_HEADS, AT_HEAD_DIM), ak.reshape(B, T, AT_HEADS, AT_HEAD_DIM),
                             av.reshape(B, T, AT_HEADS, AT_HEAD_DIM), iq.reshape(B, T, IDX_HEADS, IDX_DIM), ik, iw)
        y_at = o_at.reshape(B, T, AT_WIDTH) @ w_attn_up[i]
        mixed = jax.nn.sigmoid(ga) * y_hg + jax.nn.sigmoid(gb) * y_at
        x = x + mixed @ w_out[i]
        x = x + hier_moe(rms_norm(x, ffn_norm[i]), w_router_group[i], b_router_group[i],
                         w_router_expert[i], b_router_expert[i], w_exp_gate[i], w_exp_up[i], w_exp_down[i])
        gate = jax.nn.sigmoid(rms_norm(x, ple_norm[i]) @ w_ple_gate[i])
        x = x + gate * (p[i].astype(x.dtype) @ w_ple_proj[i])
    return rms_norm(x, final_norm)
```

```python
import functools

import numpy as np
import jax
import jax.numpy as jnp
from jax import lax
from jax.experimental import pallas as pl
from jax.experimental.pallas import tpu as pltpu

F32 = jnp.float32
BF16 = jnp.bfloat16
HIGHEST = lax.Precision.HIGHEST

HG_HEADS = 8
HG_DK = 128
HG_DV = 128
HG_WIDTH = HG_HEADS * HG_DK
HG_CHUNK = 64
AT_HEADS = 8
AT_HEAD_DIM = 128
AT_WIDTH = AT_HEADS * AT_HEAD_DIM
IDX_HEADS = 16
IDX_DIM = 64
TOPK_MAX = 256
N_GROUPS = 4
EXPERTS_PER_GROUP = 8
N_EXPERTS = N_GROUPS * EXPERTS_PER_GROUP
D_EXPERT = 512
RMS_EPS = 1e-6
LANES = 128
INT_MIN = -(2 ** 31)
NEG_BIG = -1e30

_NT = (((1,), (1,)), ((), ()))
_TN = (((0,), (0,)), ((), ()))


def _params(sem, vmem_mb=None):
    kw = dict(dimension_semantics=sem)
    if vmem_mb is not None:
        kw["vmem_limit_bytes"] = vmem_mb * 1024 * 1024
    return pltpu.CompilerParams(**kw)


def _sigmoid(x):
    return 1.0 / (1.0 + jnp.exp(-x))


def _rms(x, g):
    ms = jnp.mean(x * x, axis=-1, keepdims=True)
    return x * lax.rsqrt(ms + RMS_EPS) * g


def _rmsnorm_kernel(x_ref, g_ref, o_ref):
    o_ref[...] = _rms(x_ref[...], g_ref[...]).astype(o_ref.dtype)


def _rmsnorm(x, g, out_dtype, tm=512):
    m, d = x.shape
    return pl.pallas_call(
        _rmsnorm_kernel,
        grid=(m // tm,),
        in_specs=[pl.BlockSpec((tm, d), lambda i: (i, 0)), pl.BlockSpec((1, d), lambda i: (0, 0))],
        out_specs=pl.BlockSpec((tm, d), lambda i: (i, 0)),
        out_shape=jax.ShapeDtypeStruct((m, d), out_dtype),
        compiler_params=_params(("parallel",)),
        name="rmsnorm",
    )(x, g.reshape(1, d))


def _mm_kernel(a_ref, w_ref, b_ref, o_ref, *, act):
    acc = jnp.dot(a_ref[...], w_ref[...], preferred_element_type=F32) + b_ref[...]
    if act == "sigmoid":
        acc = _sigmoid(acc)
    o_ref[...] = acc.astype(o_ref.dtype)


def _matmul(a, w, b, out_dtype, act=None, tm=1024, tn=512, name="proj"):
    m, k = a.shape
    n = w.shape[1]
    tm, tn = min(tm, m), min(tn, n)
    return pl.pallas_call(
        functools.partial(_mm_kernel, act=act),
        grid=(m // tm, n // tn),
        in_specs=[pl.BlockSpec((tm, k), lambda i, j: (i, 0)),
                  pl.BlockSpec((k, tn), lambda i, j: (0, j)),
                  pl.BlockSpec((1, tn), lambda i, j: (0, j))],
        out_specs=pl.BlockSpec((tm, tn), lambda i, j: (i, j)),
        out_shape=jax.ShapeDtypeStruct((m, n), out_dtype),
        compiler_params=_params(("parallel", "arbitrary"), 48),
        name=name,
    )(a, w, b.reshape(1, n))


def _hgrn_tables():
    c = HG_CHUNK
    levels = [c >> i for i in range(int(np.log2(c)))]
    t = np.arange(c)
    u = np.arange(c)
    rall = [(u[None, :] <= t[:, None]), (u[None, :] > t[:, None])]
    low, mask = [], []
    for p in levels:
        half = p // 2
        m = (t // p) * p + half
        lower = (t % p) >= half
        r = np.where(lower[:, None], (u[None, :] > m[:, None]) & (u[None, :] <= t[:, None]),
                     (u[None, :] > t[:, None]) & (u[None, :] <= m[:, None]))
        rall.append(r)
        low.append(np.broadcast_to(lower[:, None], (c, LANES)))
        mask.append((t[:, None] // p) == (t[None, :] // p))
    mask.append(np.eye(c, dtype=bool))
    return (np.concatenate(rall, 0).astype(np.float32), np.stack(low).astype(np.float32),
            np.stack(mask).astype(np.float32), len(levels))


def _hgrn_kernel(q_ref, f_ref, i_ref, g_ref, lb_ref, gn_ref, rall_ref, low_ref, mask_ref, o_ref, st_ref,
                 *, n_chunks, n_levels):
    c = HG_CHUNK
    lb = lb_ref[0]
    one_m_lb = 1.0 - lb
    gn = gn_ref[...]
    st_ref[...] = jnp.zeros_like(st_ref)

    def body(ci, carry):
        rows = pl.ds(pl.multiple_of(ci * c, c), c)
        z = f_ref[rows, :]
        q = q_ref[rows, :]
        vb = i_ref[rows, :].astype(BF16)
        e = jnp.exp(-jnp.abs(z))
        r = 1.0 / (1.0 + e)
        pos = z >= 0
        sig = jnp.where(pos, r, e * r)
        nsig = jnp.where(pos, e * r, r)
        logf = jnp.log(lb + one_m_lb * sig)
        k = one_m_lb * nsig
        qs = q * _sigmoid(q)
        ex = jnp.exp(jnp.dot(rall_ref[...], logf, precision=HIGHEST, preferred_element_type=F32))
        eb = ex[0:c]
        qh = (qs * eb).astype(BF16)
        kd = (k * ex[c:2 * c]).astype(BF16)
        a = mask_ref[n_levels] * lax.dot_general(qs.astype(BF16), k.astype(BF16), _NT,
                                                 preferred_element_type=F32)
        for li in range(n_levels):
            ep = ex[(2 + li) * c:(3 + li) * c]
            lowm = low_ref[li]
            qt = (qs * ep * lowm).astype(BF16)
            kt = (k * ep * (1.0 - lowm)).astype(BF16)
            a = a + mask_ref[li] * lax.dot_general(qt, kt, _NT, preferred_element_type=F32)
        st = st_ref[...]
        o = (jnp.dot(a.astype(BF16), vb, preferred_element_type=F32)
             + lax.dot_general(qh, st.astype(BF16), _NT, preferred_element_type=F32))
        st_ref[...] = st * eb[c - 1:c, :] + lax.dot_general(vb, kd, _TN, preferred_element_type=F32)
        gate = g_ref[rows, :]
        o_ref[rows, :] = (_rms(o, gn) * (gate * _sigmoid(gate))).astype(o_ref.dtype)
        return carry

    lax.fori_loop(0, n_chunks, body, 0)


def _hgrn(zhg, lb, gn, batch, seq):
    rall, low, mask, n_levels = _hgrn_tables()
    h = HG_HEADS
    col = lambda off: pl.BlockSpec((seq, HG_DK), lambda b, hh: (b, off + hh))
    full = lambda a: pl.BlockSpec(a.shape, lambda b, hh: (0,) * a.ndim)
    return pl.pallas_call(
        functools.partial(_hgrn_kernel, n_chunks=seq // HG_CHUNK, n_levels=n_levels),
        grid=(batch, h),
        in_specs=[col(0), col(h), col(2 * h), col(3 * h),
                  pl.BlockSpec((1, 1, HG_DK), lambda b, hh: (hh, 0, 0)),
                  pl.BlockSpec((1, HG_DV), lambda b, hh: (0, 0)),
                  full(rall), full(low), full(mask)],
        out_specs=pl.BlockSpec((seq, HG_DV), lambda b, hh: (b, hh)),
        out_shape=jax.ShapeDtypeStruct((batch * seq, h * HG_DV), BF16),
        scratch_shapes=[pltpu.VMEM((HG_DV, HG_DK), F32)],
        compiler_params=_params(("parallel", "parallel"), 48),
        name="hgrn2",
    )(zhg, zhg, zhg, zhg, lb.reshape(h, 1, HG_DK), gn.reshape(1, HG_DV),
      jnp.asarray(rall), jnp.asarray(low), jnp.asarray(mask))


def _dsa_kernel(iq_ref, ikt_ref, iw_ref, q_ref, k_ref, v_ref, o_ref, acc_ref, key_ref, bias_ref, dist_ref,
                *, tq, seq, ksel):
    t0 = pl.program_id(1) * tq
    ikt = ikt_ref[0]
    ik_hi = ikt.astype(BF16)
    ik_lo = (ikt - ik_hi.astype(F32)).astype(BF16)
    rhs = jnp.concatenate([ik_hi, ik_hi, ik_lo, ik_lo], axis=0)
    lane = lax.broadcasted_iota(jnp.int32, (tq, LANES), 1)
    w_all = iw_ref[...] * (IDX_HEADS ** -0.5 * IDX_DIM ** -0.5)
    for h in range(IDX_HEADS):
        x = iq_ref[:, h * LANES:(h + 1) * LANES]
        x_hi = x.astype(BF16).astype(F32)
        u = jnp.where(lane < IDX_DIM, x_hi, x - x_hi).astype(BF16)
        s = jnp.dot(jnp.concatenate([u, u], axis=1), rhs, preferred_element_type=F32)
        term = jnp.maximum(s, 0.0) * w_all[:, IDX_DIM + h:IDX_DIM + h + 1]
        if h == 0:
            acc_ref[...] = term
        else:
            acc_ref[...] += term

    col = lax.broadcasted_iota(jnp.int32, (tq, seq), 1)
    row = t0 + lax.broadcasted_iota(jnp.int32, (tq, seq), 0)
    vis = col <= row
    bits = pltpu.bitcast(acc_ref[...], jnp.int32)
    key = bits ^ ((bits >> 31) & 0x7FFFFFFF)
    key_ref[...] = jnp.where(vis, key, INT_MIN)
    dist_ref[...] = (row - col).astype(F32)

    def bisect(it, c):
        cand = c | jnp.left_shift(jnp.int32(1), 31 - it)
        cnt = jnp.sum((key_ref[...] >= (cand ^ INT_MIN)).astype(jnp.int32), axis=1, keepdims=True)
        return jnp.where(cnt >= ksel, cand, c)

    c = lax.fori_loop(0, 32, bisect, jnp.zeros((tq, 1), jnp.int32))
    thr = c ^ INT_MIN
    bias_ref[...] = jnp.where(vis, jnp.where(key_ref[...] >= thr, 0.0, NEG_BIG), NEG_BIG)

    scale = AT_HEAD_DIM ** -0.5
    for h in range(AT_HEADS):
        hs = slice(h * AT_HEAD_DIM, (h + 1) * AT_HEAD_DIM)
        s = lax.dot_general(q_ref[:, hs], k_ref[:, hs], _NT, preferred_element_type=F32)
        slope = 2.0 ** (-8.0 * (h + 1) / AT_HEADS)
        logit = s * scale - slope * dist_ref[...] + bias_ref[...]
        m = jnp.max(logit, axis=1, keepdims=True)
        p = jnp.exp(logit - m)
        den = jnp.sum(p, axis=1, keepdims=True)
        oh = jnp.dot(p.astype(BF16), v_ref[:, hs], preferred_element_type=F32) / den
        o_ref[:, hs] = oh.astype(o_ref.dtype)


def _dsa(iq_dup, ikt, ikw, zat, batch, seq, tq=128):
    nq = seq // tq
    ksel = min(TOPK_MAX, seq // 4)
    return pl.pallas_call(
        functools.partial(_dsa_kernel, tq=tq, seq=seq, ksel=ksel),
        grid=(batch, nq),
        in_specs=[pl.BlockSpec((tq, IDX_HEADS * LANES), lambda b, i: (b * nq + i, 0)),
                  pl.BlockSpec((1, IDX_DIM, seq), lambda b, i: (b, 0, 0)),
                  pl.BlockSpec((tq, LANES), lambda b, i: (b * nq + i, 0)),
                  pl.BlockSpec((tq, AT_WIDTH), lambda b, i: (b * nq + i, 0)),
                  pl.BlockSpec((seq, AT_WIDTH), lambda b, i: (b, 1)),
                  pl.BlockSpec((seq, AT_WIDTH), lambda b, i: (b, 2))],
        out_specs=pl.BlockSpec((tq, AT_WIDTH), lambda b, i: (b * nq + i, 0)),
        out_shape=jax.ShapeDtypeStruct((batch * seq, AT_WIDTH), BF16),
        scratch_shapes=[pltpu.VMEM((tq, seq), F32), pltpu.VMEM((tq, seq), jnp.int32),
                        pltpu.VMEM((tq, seq), F32), pltpu.VMEM((tq, seq), F32)],
        compiler_params=_params(("parallel", "arbitrary"), 48),
        name="dsa",
    )(iq_dup, ikt, ikw, zat, zat, zat)


def _merge_kernel(a1_ref, w1_ref, a2_ref, w2_ref, g1_ref, g2_ref, o_ref):
    y1 = jnp.dot(a1_ref[...], w1_ref[...], preferred_element_type=F32)
    y2 = jnp.dot(a2_ref[...], w2_ref[...], preferred_element_type=F32)
    o_ref[...] = (g1_ref[...].astype(F32) * y1 + g2_ref[...].astype(F32) * y2).astype(o_ref.dtype)


def _merge(o_hg, w_hg, o_at, w_at, gates, tm=1024, tn=512):
    m, k1 = o_hg.shape
    k2 = o_at.shape[1]
    n = w_hg.shape[1]
    tm = min(tm, m)
    nb = n // tn
    return pl.pallas_call(
        _merge_kernel,
        grid=(m // tm, nb),
        in_specs=[pl.BlockSpec((tm, k1), lambda i, j: (i, 0)), pl.BlockSpec((k1, tn), lambda i, j: (0, j)),
                  pl.BlockSpec((tm, k2), lambda i, j: (i, 0)), pl.BlockSpec((k2, tn), lambda i, j: (0, j)),
                  pl.BlockSpec((tm, tn), lambda i, j: (i, j)), pl.BlockSpec((tm, tn), lambda i, j: (i, nb + j))],
        out_specs=pl.BlockSpec((tm, tn), lambda i, j: (i, j)),
        out_shape=jax.ShapeDtypeStruct((m, n), BF16),
        compiler_params=_params(("parallel", "arbitrary"), 48),
        name="merge",
    )(o_hg, w_hg, o_at, w_at, gates, gates)


def _outproj_kernel(m_ref, w_ref, x_ref, gn_ref, wr_ref, br_ref, x1_ref, hn_ref, ri_ref, rw_ref):
    x1 = x_ref[...] + jnp.dot(m_ref[...], w_ref[...], preferred_element_type=F32)
    x1_ref[...] = x1
    hn = _rms(x1, gn_ref[...])
    hn_ref[...] = hn
    logit = jnp.dot(hn, wr_ref[...], precision=HIGHEST, preferred_element_type=F32) + br_ref[...]
    lane = lax.broadcasted_iota(jnp.int32, logit.shape, 1)
    lane_f = lane.astype(F32)
    is_g = lane < N_GROUPS
    lg = jnp.where(is_g, logit, -jnp.inf)
    mg = jnp.max(lg, axis=1, keepdims=True)
    pg_top = 1.0 / jnp.sum(jnp.exp(lg - mg), axis=1, keepdims=True)
    gid = jnp.min(jnp.where(lg == mg, lane_f, float(LANES)), axis=1, keepdims=True).astype(jnp.int32)
    in_grp = ((lane - N_GROUPS) >> 3) == gid
    le = jnp.where(in_grp, logit, -jnp.inf)
    m1 = jnp.max(le, axis=1, keepdims=True)
    i1 = jnp.min(jnp.where(le == m1, lane_f, float(LANES)), axis=1, keepdims=True)
    le2 = jnp.where(lane_f == i1, -jnp.inf, le)
    m2 = jnp.max(le2, axis=1, keepdims=True)
    i2 = jnp.min(jnp.where(le2 == m2, lane_f, float(LANES)), axis=1, keepdims=True)
    r = jnp.exp(m2 - m1)
    w1 = pg_top / (1.0 + r)
    w2 = pg_top * r / (1.0 + r)
    e1 = i1.astype(jnp.int32) - N_GROUPS
    e2 = i2.astype(jnp.int32) - N_GROUPS
    ri_ref[...] = jnp.where(lane == 0, e1, jnp.where(lane == 1, e2, 0))
    rw_ref[...] = jnp.where(lane == 0, w1, jnp.where(lane == 1, w2, 0.0))


def _outproj(mixed, w_out, x, gn, w_r, b_r, tm=256):
    m, d = x.shape
    row = lambda w: pl.BlockSpec((tm, w), lambda i: (i, 0))
    const = lambda a: pl.BlockSpec(a.shape, lambda i: (0, 0))
    gn = gn.reshape(1, d)
    b_r = b_r.reshape(1, LANES)
    return pl.pallas_call(
        _outproj_kernel,
        grid=(m // tm,),
        in_specs=[row(d), const(w_out), row(d), const(gn), const(w_r), const(b_r)],
        out_specs=[row(d), row(d), row(LANES), row(LANES)],
        out_shape=[jax.ShapeDtypeStruct((m, d), F32), jax.ShapeDtypeStruct((m, d), F32),
                   jax.ShapeDtypeStruct((m, LANES), jnp.int32), jax.ShapeDtypeStruct((m, LANES), F32)],
        compiler_params=_params(("parallel",), 56),
        name="outproj_router",
    )(mixed, w_out, x, gn, w_r, b_r)


def _expert_kernel(te_ref, tv_ref, pair_ref, hn_hbm, wg_ref, wu_ref, wd_ref, y_hbm,
                   xbuf, ybuf, wgb, wub, wdb, gsem, ssem, *, tm, n_tok):
    t = pl.program_id(0)
    nv = tv_ref[t]
    base = t * tm

    @pl.when(nv > 0)
    def _():
        def gather(r, carry):
            tok = lax.rem(pair_ref[base + r], n_tok)
            pltpu.make_async_copy(hn_hbm.at[pl.ds(tok, 1), :], xbuf.at[pl.ds(r, 1), :], gsem).start()
            return carry

        lax.fori_loop(0, tm, gather, 0)

        @pl.when(jnp.logical_or(t == 0, te_ref[t] != te_ref[jnp.maximum(t - 1, 0)]))
        def _():
            wgb[...] = wg_ref[0].astype(BF16)
            wub[...] = wu_ref[0].astype(BF16)
            wdb[...] = wd_ref[0].astype(BF16)

        pltpu.make_async_copy(hn_hbm.at[pl.ds(0, tm), :], xbuf, gsem).wait()
        xb = xbuf[...].astype(BF16)
        g = jnp.dot(xb, wgb[...], preferred_element_type=F32)
        u = jnp.dot(xb, wub[...], preferred_element_type=F32)
        hid = (g * _sigmoid(g) * u).astype(BF16)
        ybuf[...] = jnp.dot(hid, wdb[...], preferred_element_type=F32)

        def scatter(r, carry):
            dst = pair_ref[base + r]
            pltpu.make_async_copy(ybuf.at[pl.ds(r, 1), :], y_hbm.at[pl.ds(dst, 1), :], ssem).start()
            return carry

        lax.fori_loop(0, nv, scatter, 0)

        def drain(r, carry):
            pltpu.make_async_copy(ybuf.at[pl.ds(0, 1), :], y_hbm.at[pl.ds(0, 1), :], ssem).wait()
            return carry

        lax.fori_loop(0, nv, drain, 0)


def _experts(hn, tile_expert, tile_valid, pair, w_g, w_u, w_d, tm):
    n_tok, d = hn.shape
    n_tiles = tile_expert.shape[0]
    wspec = lambda a: pl.BlockSpec((1,) + a.shape[1:], lambda t, te, tv, pr: (te[t], 0, 0))
    return pl.pallas_call(
        functools.partial(_expert_kernel, tm=tm, n_tok=n_tok),
        grid_spec=pltpu.PrefetchScalarGridSpec(
            num_scalar_prefetch=3,
            grid=(n_tiles,),
            in_specs=[pl.BlockSpec(memory_space=pl.ANY), wspec(w_g), wspec(w_u), wspec(w_d)],
            out_specs=pl.BlockSpec(memory_space=pl.ANY),
            scratch_shapes=[pltpu.VMEM((tm, d), F32), pltpu.VMEM((tm, d), F32),
                            pltpu.VMEM(w_g.shape[1:], BF16), pltpu.VMEM(w_u.shape[1:], BF16),
                            pltpu.VMEM(w_d.shape[1:], BF16),
                            pltpu.SemaphoreType.DMA, pltpu.SemaphoreType.DMA]),
        out_shape=jax.ShapeDtypeStruct((2 * n_tok, d), F32),
        compiler_params=_params(("arbitrary",), 56),
        name="experts",
    )(tile_expert, tile_valid, pair, hn, w_g, w_u, w_d)


def _dispatch_plan(route_e, tm):
    n_tok = route_e.shape[0]
    n_pairs = 2 * n_tok
    n_tiles = n_pairs // tm + N_EXPERTS
    eflat = route_e.T.reshape(-1)
    order = jnp.argsort(eflat, stable=True).astype(jnp.int32)
    counts = jnp.zeros((N_EXPERTS,), jnp.int32).at[eflat].add(1)
    tiles_e = (counts + tm - 1) // tm
    tile_end = jnp.cumsum(tiles_e)
    tile_off = tile_end - tiles_e
    start = jnp.cumsum(counts) - counts
    sorted_e = eflat[order]
    dest = tile_off[sorted_e] * tm + (jnp.arange(n_pairs, dtype=jnp.int32) - start[sorted_e])
    pair = jnp.zeros((n_tiles * tm,), jnp.int32).at[dest].set(order)
    tid = jnp.arange(n_tiles, dtype=jnp.int32)
    te = jnp.minimum(jnp.searchsorted(tile_end, tid, side="right"), N_EXPERTS - 1).astype(jnp.int32)
    tv = jnp.clip(counts[te] - (tid - tile_off[te]) * tm, 0, tm)
    used = tid < tile_end[-1]
    last_e = te[jnp.maximum(tile_end[-1] - 1, 0)]
    return jnp.where(used, te, last_e).astype(jnp.int32), jnp.where(used, tv, 0).astype(jnp.int32), pair


def _final_kernel(x1_ref, y0_ref, y1_ref, rw_ref, p_ref, wgate_ref, wproj_ref, gple_ref, gfin_ref, o_ref,
                  *, last_layer):
    rw = rw_ref[...]
    x2 = x1_ref[...] + rw[:, 0:1] * y0_ref[...] + rw[:, 1:2] * y1_ref[...]
    hn = _rms(x2, gple_ref[...]).astype(BF16)
    gate = _sigmoid(jnp.dot(hn, wgate_ref[...], preferred_element_type=F32))
    emb = jnp.dot(p_ref[...].astype(BF16), wproj_ref[...], preferred_element_type=F32)
    x3 = x2 + gate * emb
    o_ref[...] = _rms(x3, gfin_ref[...]) if last_layer else x3


def _final(x1, y, rw, p, w_gate, w_proj, g_ple, g_fin, last_layer, tm=256):
    m, d = x1.shape
    nb = m // tm
    row = lambda w: pl.BlockSpec((tm, w), lambda i: (i, 0))
    const = lambda a: pl.BlockSpec(a.shape, lambda i: (0, 0))
    g_ple, g_fin = g_ple.reshape(1, d), g_fin.reshape(1, d)
    return pl.pallas_call(
        functools.partial(_final_kernel, last_layer=last_layer),
        grid=(nb,),
        in_specs=[row(d), row(d), pl.BlockSpec((tm, d), lambda i: (nb + i, 0)), row(LANES), row(p.shape[1]),
                  const(w_gate), const(w_proj), const(g_ple), const(g_fin)],
        out_specs=row(d),
        out_shape=jax.ShapeDtypeStruct((m, d), F32),
        compiler_params=_params(("parallel",), 56),
        name="final",
    )(x1, y, y, rw, p, w_gate, w_proj, g_ple, g_fin)


def kernel(x, p, attn_norm, w_in, b_in, hg_lb_logits, hg_out_norm, w_hg_up, w_attn_up, w_out, ffn_norm,
           w_router_group, b_router_group, w_router_expert, b_router_expert, w_exp_gate, w_exp_up,
           w_exp_down, ple_norm, w_ple_gate, w_ple_proj, final_norm):
    batch, seq, d = x.shape
    n = batch * seq
    depth = w_in.shape[0]
    o_hf, o_at = 0, 4 * HG_WIDTH
    o_iq = o_at + 3 * AT_WIDTH
    o_ik = o_iq + IDX_HEADS * IDX_DIM
    o_iw = o_ik + IDX_DIM
    o_g = o_iw + IDX_HEADS
    lbs = jnp.cumsum(jax.nn.softmax(hg_lb_logits.astype(F32), axis=0), axis=0)
    xf = x.reshape(n, d)
    moe_tm = min(256, n // 2)
    for li in range(depth):
        w, b = w_in[li], b_in[li]
        h = _rmsnorm(xf, attn_norm[li], BF16)
        zhg = _matmul(h, w[:, o_hf:o_at].astype(BF16), b[o_hf:o_at], F32, name="proj_hgrn")
        zat = _matmul(h, w[:, o_at:o_iq].astype(BF16), b[o_at:o_iq], BF16, name="proj_attn")
        dup = lambda a: jnp.repeat(a.reshape(a.shape[:-1] + (IDX_HEADS, 1, IDX_DIM)), 2, axis=-2).reshape(
            a.shape[:-1] + (2 * IDX_HEADS * IDX_DIM,))
        ziq = _matmul(h, dup(w[:, o_iq:o_ik]).astype(BF16), dup(b[o_iq:o_ik]), F32, name="proj_iq")
        pad = LANES - IDX_DIM - IDX_HEADS
        zikw = _matmul(h, jnp.pad(w[:, o_ik:o_g], ((0, 0), (0, pad))).astype(BF16),
                       jnp.pad(b[o_ik:o_g], (0, pad)), F32, name="proj_ikw")
        gates = _matmul(h, w[:, o_g:].astype(BF16), b[o_g:], BF16, act="sigmoid", name="proj_gates")

        o_hg = _hgrn(zhg, lbs[li], hg_out_norm[li], batch, seq)
        ikt = zikw[:, :IDX_DIM].reshape(batch, seq, IDX_DIM).transpose(0, 2, 1)
        o_att = _dsa(ziq, ikt, zikw, zat, batch, seq)
        mixed = _merge(o_hg, w_hg_up[li].astype(BF16), o_att, w_attn_up[li].astype(BF16), gates)

        w_r = jnp.pad(jnp.concatenate([w_router_group[li], w_router_expert[li]], axis=1),
                      ((0, 0), (0, LANES - N_GROUPS - N_EXPERTS)))
        b_r = jnp.pad(jnp.concatenate([b_router_group[li], b_router_expert[li]]), (0, LANES - N_GROUPS - N_EXPERTS))
        x1, hn, route_e, route_w = _outproj(mixed, w_out[li].astype(BF16), xf, ffn_norm[li], w_r, b_r)
        tile_e, tile_v, pair = _dispatch_plan(route_e[:, :2], moe_tm)
        y = _experts(hn, tile_e, tile_v, pair, w_exp_gate[li], w_exp_up[li], w_exp_down[li], moe_tm)
        xf = _final(x1, y, route_w, p[li].reshape(n, -1), w_ple_gate[li].astype(BF16),
                    w_ple_proj[li].astype(BF16), ple_norm[li], final_norm, li == depth - 1)
    return xf.reshape(batch, seq, d)
```

```python
import functools

import numpy as np
import jax
import jax.numpy as jnp
from jax import lax
from jax.experimental import pallas as pl
from jax.experimental.pallas import tpu as pltpu

F32 = jnp.float32
BF16 = jnp.bfloat16
HIGHEST = lax.Precision.HIGHEST

HG_HEADS = 8
HG_DK = 128
HG_DV = 128
HG_WIDTH = HG_HEADS * HG_DK
HG_CHUNK = 64
AT_HEADS = 8
AT_HEAD_DIM = 128
AT_WIDTH = AT_HEADS * AT_HEAD_DIM
IDX_HEADS = 16
IDX_DIM = 64
TOPK_MAX = 256
N_GROUPS = 4
EXPERTS_PER_GROUP = 8
N_EXPERTS = N_GROUPS * EXPERTS_PER_GROUP
D_EXPERT = 512
RMS_EPS = 1e-6
LANES = 128
INT_MIN = -(2 ** 31)
NEG_BIG = -1e30

_NT = (((1,), (1,)), ((), ()))
_TN = (((0,), (0,)), ((), ()))


def _params(sem, vmem_mb=None):
    kw = dict(dimension_semantics=sem)
    if vmem_mb is not None:
        kw["vmem_limit_bytes"] = vmem_mb * 1024 * 1024
    return pltpu.CompilerParams(**kw)


def _sigmoid(x):
    return 1.0 / (1.0 + jnp.exp(-x))


def _rms(x, g):
    ms = jnp.mean(x * x, axis=-1, keepdims=True)
    return x * lax.rsqrt(ms + RMS_EPS) * g


def _rmsnorm_kernel(x_ref, g_ref, o_ref):
    o_ref[...] = _rms(x_ref[...], g_ref[...]).astype(o_ref.dtype)


def _rmsnorm(x, g, out_dtype, tm=512):
    m, d = x.shape
    return pl.pallas_call(
        _rmsnorm_kernel,
        grid=(m // tm,),
        in_specs=[pl.BlockSpec((tm, d), lambda i: (i, 0)), pl.BlockSpec((1, d), lambda i: (0, 0))],
        out_specs=pl.BlockSpec((tm, d), lambda i: (i, 0)),
        out_shape=jax.ShapeDtypeStruct((m, d), out_dtype),
        compiler_params=_params(("parallel",)),
        name="rmsnorm",
    )(x, g.reshape(1, d))


def _mm_kernel(a_ref, w_ref, b_ref, o_ref, *, act):
    acc = jnp.dot(a_ref[...], w_ref[...], preferred_element_type=F32) + b_ref[...]
    if act == "sigmoid":
        acc = _sigmoid(acc)
    o_ref[...] = acc.astype(o_ref.dtype)


def _matmul(a, w, b, out_dtype, act=None, tm=1024, tn=512, name="proj"):
    m, k = a.shape
    n = w.shape[1]
    tm, tn = min(tm, m), min(tn, n)
    return pl.pallas_call(
        functools.partial(_mm_kernel, act=act),
        grid=(m // tm, n // tn),
        in_specs=[pl.BlockSpec((tm, k), lambda i, j: (i, 0)),
                  pl.BlockSpec((k, tn), lambda i, j: (0, j)),
                  pl.BlockSpec((1, tn), lambda i, j: (0, j))],
        out_specs=pl.BlockSpec((tm, tn), lambda i, j: (i, j)),
        out_shape=jax.ShapeDtypeStruct((m, n), out_dtype),
        compiler_params=_params(("parallel", "arbitrary"), 48),
        name=name,
    )(a, w, b.reshape(1, n))


HG_GROUP = 8
HG_ROWS = 512


def _hgrn_tables():
    c = HG_CHUNK
    levels = [c >> i for i in range(int(np.log2(c)))]
    t = np.arange(c)
    u = np.arange(c)
    rall = [(u[None, :] <= t[:, None]), (u[None, :] > t[:, None])]
    low, mask = [], []
    for p in levels:
        half = p // 2
        m = (t // p) * p + half
        lower = (t % p) >= half
        r = np.where(lower[:, None], (u[None, :] > m[:, None]) & (u[None, :] <= t[:, None]),
                     (u[None, :] > t[:, None]) & (u[None, :] <= m[:, None]))
        rall.append(r)
        low.append(np.broadcast_to(lower[:, None], (c, LANES)))
        mask.append(((t[:, None] // p) == (t[None, :] // p)) & lower[:, None] & ~lower[None, :])
    mask.append(np.eye(c, dtype=bool))
    rall = np.concatenate(rall, 0).astype(np.float32)
    return (np.tile(rall, (1, 3)), np.stack(low).astype(np.float32), np.stack(mask).astype(np.float32),
            len(levels))


def _hgrn_kernel(q_ref, f_ref, i_ref, g_ref, lb_ref, gn_ref, rall_ref, low_ref, mask_ref, o_ref, st_ref,
                 *, n_chunks, n_levels, heads):
    c = HG_CHUNK

    @pl.when(pl.program_id(2) == 0)
    def _():
        st_ref[...] = jnp.zeros_like(st_ref)

    lb = lb_ref[...]
    one_m_lb = 1.0 - lb
    gn = gn_ref[...]

    def body(ci, carry):
        rows = pl.ds(pl.multiple_of(ci * c, c), c)
        z = f_ref[rows, :]
        q = q_ref[rows, :]
        e = jnp.exp(-jnp.abs(z))
        r = 1.0 / (1.0 + e)
        pos = z >= 0
        sig = jnp.where(pos, r, e * r)
        nsig = jnp.where(pos, e * r, r)
        logf = jnp.log(lb + one_m_lb * sig)
        k = one_m_lb * nsig
        qs = q * _sigmoid(q)
        l1 = logf.astype(BF16)
        r1 = logf - l1.astype(F32)
        l2 = r1.astype(BF16)
        l3 = (r1 - l2.astype(F32)).astype(BF16)
        ex = jnp.exp(jnp.dot(rall_ref[...], jnp.concatenate([l1, l2, l3], axis=0), preferred_element_type=F32))
        vb = i_ref[rows, :].astype(BF16)
        gate = g_ref[rows, :]
        gate = gate * _sigmoid(gate)
        for g in range(heads):
            sl = slice(g * HG_DK, (g + 1) * HG_DK)
            qg, kg, vg = qs[:, sl], k[:, sl], vb[:, sl]
            eb = ex[0:c, sl]
            a = mask_ref[n_levels] * lax.dot_general(qg.astype(BF16), kg.astype(BF16), _NT,
                                                     preferred_element_type=F32)
            for li in range(n_levels):
                w = (jnp.where(low_ref[li] != 0.0, qg, kg) * ex[(2 + li) * c:(3 + li) * c, sl]).astype(BF16)
                a = a + mask_ref[li] * lax.dot_general(w, w, _NT, preferred_element_type=F32)
            st = st_ref[g]
            o = (jnp.dot(a.astype(BF16), vg, preferred_element_type=F32)
                 + lax.dot_general((qg * eb).astype(BF16), st.astype(BF16), _NT, preferred_element_type=F32))
            st_ref[g] = st * eb[c - 1:c, :] + lax.dot_general(vg, (kg * ex[c:2 * c, sl]).astype(BF16), _TN,
                                                              preferred_element_type=F32)
            o_ref[rows, sl] = (_rms(o, gn) * gate[:, sl]).astype(o_ref.dtype)
        return carry

    lax.fori_loop(0, n_chunks, body, 0, unroll=2)


def _hgrn(zhg, lb, gn, batch, seq):
    rall, low, mask, n_levels = _hgrn_tables()
    hb = HG_HEADS // HG_GROUP
    gw = HG_GROUP * HG_DK
    tc = min(HG_ROWS, seq)
    nt = seq // tc
    col = lambda off: pl.BlockSpec((tc, gw), lambda b, hh, tt: (b * nt + tt, off + hh))
    full = lambda a: pl.BlockSpec(a.shape, lambda b, hh, tt: (0,) * a.ndim)
    return pl.pallas_call(
        functools.partial(_hgrn_kernel, n_chunks=tc // HG_CHUNK, n_levels=n_levels, heads=HG_GROUP),
        grid=(batch, hb, nt),
        in_specs=[col(0), col(hb), col(2 * hb), col(3 * hb),
                  pl.BlockSpec((1, gw), lambda b, hh, tt: (0, hh)),
                  pl.BlockSpec((1, HG_DV), lambda b, hh, tt: (0, 0)),
                  full(rall), full(low), full(mask)],
        out_specs=pl.BlockSpec((tc, gw), lambda b, hh, tt: (b * nt + tt, hh)),
        out_shape=jax.ShapeDtypeStruct((batch * seq, HG_HEADS * HG_DV), BF16),
        scratch_shapes=[pltpu.VMEM((HG_GROUP, HG_DV, HG_DK), F32)],
        compiler_params=_params(("parallel", "parallel", "arbitrary"), 48),
        name="hgrn2",
    )(zhg, zhg, zhg, zhg, lb.reshape(1, HG_WIDTH), gn.reshape(1, HG_DV),
      jnp.asarray(rall, BF16), jnp.asarray(low), jnp.asarray(mask))


def _dsa_kernel(iq_ref, ikt_ref, iw_ref, q_ref, k_ref, v_ref, o_ref, acc_ref, key_ref, bias_ref, dist_ref,
                *, tq, seq, ksel):
    t0 = pl.program_id(1) * tq
    ikt = ikt_ref[0]
    ik_hi = ikt.astype(BF16)
    ik_lo = (ikt - ik_hi.astype(F32)).astype(BF16)
    rhs = jnp.concatenate([ik_hi, ik_hi, ik_lo, ik_lo], axis=0)
    lane = lax.broadcasted_iota(jnp.int32, (tq, LANES), 1)
    w_all = iw_ref[...] * (IDX_HEADS ** -0.5 * IDX_DIM ** -0.5)
    for h in range(IDX_HEADS):
        x = iq_ref[:, h * LANES:(h + 1) * LANES]
        x_hi = x.astype(BF16).astype(F32)
        u = jnp.where(lane < IDX_DIM, x_hi, x - x_hi).astype(BF16)
        s = jnp.dot(jnp.concatenate([u, u], axis=1), rhs, preferred_element_type=F32)
        term = jnp.maximum(s, 0.0) * w_all[:, IDX_DIM + h:IDX_DIM + h + 1]
        if h == 0:
            acc_ref[...] = term
        else:
            acc_ref[...] += term

    col = lax.broadcasted_iota(jnp.int32, (tq, seq), 1)
    row = t0 + lax.broadcasted_iota(jnp.int32, (tq, seq), 0)
    vis = col <= row
    bits = pltpu.bitcast(acc_ref[...], jnp.int32)
    key = bits ^ ((bits >> 31) & 0x7FFFFFFF)
    key_ref[...] = jnp.where(vis, key, INT_MIN)
    dist_ref[...] = (row - col).astype(F32)

    def bisect(it, c):
        cand = c | jnp.left_shift(jnp.int32(1), 31 - it)
        cnt = jnp.sum((key_ref[...] >= (cand ^ INT_MIN)).astype(jnp.int32), axis=1, keepdims=True)
        return jnp.where(cnt >= ksel, cand, c)

    c = lax.fori_loop(0, 32, bisect, jnp.zeros((tq, 1), jnp.int32))
    thr = c ^ INT_MIN
    bias_ref[...] = jnp.where(vis, jnp.where(key_ref[...] >= thr, 0.0, NEG_BIG), NEG_BIG)

    scale = AT_HEAD_DIM ** -0.5
    for h in range(AT_HEADS):
        hs = slice(h * AT_HEAD_DIM, (h + 1) * AT_HEAD_DIM)
        s = lax.dot_general(q_ref[:, hs], k_ref[:, hs], _NT, preferred_element_type=F32)
        slope = 2.0 ** (-8.0 * (h + 1) / AT_HEADS)
        logit = s * scale - slope * dist_ref[...] + bias_ref[...]
        m = jnp.max(logit, axis=1, keepdims=True)
        p = jnp.exp(logit - m)
        den = jnp.sum(p, axis=1, keepdims=True)
        oh = jnp.dot(p.astype(BF16), v_ref[:, hs], preferred_element_type=F32) / den
        o_ref[:, hs] = oh.astype(o_ref.dtype)


def _dsa(iq_dup, ikt, ikw, zat, batch, seq, tq=128):
    nq = seq // tq
    ksel = min(TOPK_MAX, seq // 4)
    return pl.pallas_call(
        functools.partial(_dsa_kernel, tq=tq, seq=seq, ksel=ksel),
        grid=(batch, nq),
        in_specs=[pl.BlockSpec((tq, IDX_HEADS * LANES), lambda b, i: (b * nq + i, 0)),
                  pl.BlockSpec((1, IDX_DIM, seq), lambda b, i: (b, 0, 0)),
                  pl.BlockSpec((tq, LANES), lambda b, i: (b * nq + i, 0)),
                  pl.BlockSpec((tq, AT_WIDTH), lambda b, i: (b * nq + i, 0)),
                  pl.BlockSpec((seq, AT_WIDTH), lambda b, i: (b, 1)),
                  pl.BlockSpec((seq, AT_WIDTH), lambda b, i: (b, 2))],
        out_specs=pl.BlockSpec((tq, AT_WIDTH), lambda b, i: (b * nq + i, 0)),
        out_shape=jax.ShapeDtypeStruct((batch * seq, AT_WIDTH), BF16),
        scratch_shapes=[pltpu.VMEM((tq, seq), F32), pltpu.VMEM((tq, seq), jnp.int32),
                        pltpu.VMEM((tq, seq), F32), pltpu.VMEM((tq, seq), F32)],
        compiler_params=_params(("parallel", "arbitrary"), 48),
        name="dsa",
    )(iq_dup, ikt, ikw, zat, zat, zat)


def _merge_kernel(a1_ref, w1_ref, a2_ref, w2_ref, g1_ref, g2_ref, o_ref):
    y1 = jnp.dot(a1_ref[...], w1_ref[...], preferred_element_type=F32)
    y2 = jnp.dot(a2_ref[...], w2_ref[...], preferred_element_type=F32)
    o_ref[...] = (g1_ref[...].astype(F32) * y1 + g2_ref[...].astype(F32) * y2).astype(o_ref.dtype)


def _merge(o_hg, w_hg, o_at, w_at, gates, tm=1024, tn=512):
    m, k1 = o_hg.shape
    k2 = o_at.shape[1]
    n = w_hg.shape[1]
    tm = min(tm, m)
    nb = n // tn
    return pl.pallas_call(
        _merge_kernel,
        grid=(m // tm, nb),
        in_specs=[pl.BlockSpec((tm, k1), lambda i, j: (i, 0)), pl.BlockSpec((k1, tn), lambda i, j: (0, j)),
                  pl.BlockSpec((tm, k2), lambda i, j: (i, 0)), pl.BlockSpec((k2, tn), lambda i, j: (0, j)),
                  pl.BlockSpec((tm, tn), lambda i, j: (i, j)), pl.BlockSpec((tm, tn), lambda i, j: (i, nb + j))],
        out_specs=pl.BlockSpec((tm, tn), lambda i, j: (i, j)),
        out_shape=jax.ShapeDtypeStruct((m, n), BF16),
        compiler_params=_params(("parallel", "arbitrary"), 48),
        name="merge",
    )(o_hg, w_hg, o_at, w_at, gates, gates)


def _outproj_kernel(m_ref, w_ref, x_ref, gn_ref, wr_ref, br_ref, x1_ref, hn_ref, ri_ref, rw_ref):
    x1 = x_ref[...] + jnp.dot(m_ref[...], w_ref[...], preferred_element_type=F32)
    x1_ref[...] = x1
    hn = _rms(x1, gn_ref[...])
    hn_ref[...] = hn
    logit = jnp.dot(hn, wr_ref[...], precision=HIGHEST, preferred_element_type=F32) + br_ref[...]
    lane = lax.broadcasted_iota(jnp.int32, logit.shape, 1)
    lane_f = lane.astype(F32)
    is_g = lane < N_GROUPS
    lg = jnp.where(is_g, logit, -jnp.inf)
    mg = jnp.max(lg, axis=1, keepdims=True)
    pg_top = 1.0 / jnp.sum(jnp.exp(lg - mg), axis=1, keepdims=True)
    gid = jnp.min(jnp.where(lg == mg, lane_f, float(LANES)), axis=1, keepdims=True).astype(jnp.int32)
    in_grp = ((lane - N_GROUPS) >> 3) == gid
    le = jnp.where(in_grp, logit, -jnp.inf)
    m1 = jnp.max(le, axis=1, keepdims=True)
    i1 = jnp.min(jnp.where(le == m1, lane_f, float(LANES)), axis=1, keepdims=True)
    le2 = jnp.where(lane_f == i1, -jnp.inf, le)
    m2 = jnp.max(le2, axis=1, keepdims=True)
    i2 = jnp.min(jnp.where(le2 == m2, lane_f, float(LANES)), axis=1, keepdims=True)
    r = jnp.exp(m2 - m1)
    w1 = pg_top / (1.0 + r)
    w2 = pg_top * r / (1.0 + r)
    e1 = i1.astype(jnp.int32) - N_GROUPS
    e2 = i2.astype(jnp.int32) - N_GROUPS
    ri_ref[...] = jnp.where(lane == 0, e1, jnp.where(lane == 1, e2, 0))
    rw_ref[...] = jnp.where(lane == 0, w1, jnp.where(lane == 1, w2, 0.0))


def _outproj(mixed, w_out, x, gn, w_r, b_r, tm=256):
    m, d = x.shape
    row = lambda w: pl.BlockSpec((tm, w), lambda i: (i, 0))
    const = lambda a: pl.BlockSpec(a.shape, lambda i: (0, 0))
    gn = gn.reshape(1, d)
    b_r = b_r.reshape(1, LANES)
    return pl.pallas_call(
        _outproj_kernel,
        grid=(m // tm,),
        in_specs=[row(d), const(w_out), row(d), const(gn), const(w_r), const(b_r)],
        out_specs=[row(d), row(d), row(LANES), row(LANES)],
        out_shape=[jax.ShapeDtypeStruct((m, d), F32), jax.ShapeDtypeStruct((m, d), F32),
                   jax.ShapeDtypeStruct((m, LANES), jnp.int32), jax.ShapeDtypeStruct((m, LANES), F32)],
        compiler_params=_params(("parallel",), 56),
        name="outproj_router",
    )(mixed, w_out, x, gn, w_r, b_r)


def _expert_kernel(te_ref, tv_ref, pair_ref, hn_hbm, wg_ref, wu_ref, wd_ref, y_hbm,
                   xbuf, ybuf, wgb, wub, wdb, gsem, ssem, *, tm, n_tok):
    t = pl.program_id(0)
    nv = tv_ref[t]
    base = t * tm

    @pl.when(nv > 0)
    def _():
        def gather(r, carry):
            tok = lax.rem(pair_ref[base + r], n_tok)
            pltpu.make_async_copy(hn_hbm.at[pl.ds(tok, 1), :], xbuf.at[pl.ds(r, 1), :], gsem).start()
            return carry

        lax.fori_loop(0, tm, gather, 0)

        @pl.when(jnp.logical_or(t == 0, te_ref[t] != te_ref[jnp.maximum(t - 1, 0)]))
        def _():
            wgb[...] = wg_ref[0].astype(BF16)
            wub[...] = wu_ref[0].astype(BF16)
            wdb[...] = wd_ref[0].astype(BF16)

        pltpu.make_async_copy(hn_hbm.at[pl.ds(0, tm), :], xbuf, gsem).wait()
        xb = xbuf[...].astype(BF16)
        g = jnp.dot(xb, wgb[...], preferred_element_type=F32)
        u = jnp.dot(xb, wub[...], preferred_element_type=F32)
        hid = (g * _sigmoid(g) * u).astype(BF16)
        ybuf[...] = jnp.dot(hid, wdb[...], preferred_element_type=F32)

        def scatter(r, carry):
            dst = pair_ref[base + r]
            pltpu.make_async_copy(ybuf.at[pl.ds(r, 1), :], y_hbm.at[pl.ds(dst, 1), :], ssem).start()
            return carry

        lax.fori_loop(0, nv, scatter, 0)

        def drain(r, carry):
            pltpu.make_async_copy(ybuf.at[pl.ds(0, 1), :], y_hbm.at[pl.ds(0, 1), :], ssem).wait()
            return carry

        lax.fori_loop(0, nv, drain, 0)


def _experts(hn, tile_expert, tile_valid, pair, w_g, w_u, w_d, tm):
    n_tok, d = hn.shape
    n_tiles = tile_expert.shape[0]
    wspec = lambda a: pl.BlockSpec((1,) + a.shape[1:], lambda t, te, tv, pr: (te[t], 0, 0))
    return pl.pallas_call(
        functools.partial(_expert_kernel, tm=tm, n_tok=n_tok),
        grid_spec=pltpu.PrefetchScalarGridSpec(
            num_scalar_prefetch=3,
            grid=(n_tiles,),
            in_specs=[pl.BlockSpec(memory_space=pl.ANY), wspec(w_g), wspec(w_u), wspec(w_d)],
            out_specs=pl.BlockSpec(memory_space=pl.ANY),
            scratch_shapes=[pltpu.VMEM((tm, d), F32), pltpu.VMEM((tm, d), F32),
                            pltpu.VMEM(w_g.shape[1:], BF16), pltpu.VMEM(w_u.shape[1:], BF16),
                            pltpu.VMEM(w_d.shape[1:], BF16),
                            pltpu.SemaphoreType.DMA, pltpu.SemaphoreType.DMA]),
        out_shape=jax.ShapeDtypeStruct((2 * n_tok, d), F32),
        compiler_params=_params(("arbitrary",), 56),
        name="experts",
    )(tile_expert, tile_valid, pair, hn, w_g, w_u, w_d)


def _dispatch_plan(route_e, tm):
    n_tok = route_e.shape[0]
    n_pairs = 2 * n_tok
    n_tiles = n_pairs // tm + N_EXPERTS
    eflat = route_e.T.reshape(-1)
    order = jnp.argsort(eflat, stable=True).astype(jnp.int32)
    counts = jnp.sum(eflat[:, None] == jnp.arange(N_EXPERTS, dtype=jnp.int32)[None, :], axis=0, dtype=jnp.int32)
    tiles_e = (counts + tm - 1) // tm
    tile_end = jnp.cumsum(tiles_e)
    tile_off = tile_end - tiles_e
    start = jnp.cumsum(counts) - counts
    tid = jnp.arange(n_tiles, dtype=jnp.int32)
    te = jnp.minimum(jnp.sum(tile_end[None, :] <= tid[:, None], axis=1), N_EXPERTS - 1).astype(jnp.int32)
    tv = jnp.clip(counts[te] - (tid - tile_off[te]) * tm, 0, tm)
    used = tid < tile_end[-1]
    last_e = te[jnp.maximum(tile_end[-1] - 1, 0)]
    te = jnp.where(used, te, last_e).astype(jnp.int32)
    tv = jnp.where(used, tv, 0).astype(jnp.int32)
    r = jnp.arange(tm, dtype=jnp.int32)[None, :]
    src = (start[te] + (tid - tile_off[te]) * tm)[:, None] + r
    pair = jnp.where(r < tv[:, None], order[jnp.clip(src, 0, n_pairs - 1)], 0).reshape(-1)
    return te, tv, pair.astype(jnp.int32)


def _final_kernel(x1_ref, y0_ref, y1_ref, rw_ref, p_ref, wgate_ref, wproj_ref, gple_ref, gfin_ref, o_ref,
                  *, last_layer):
    rw = rw_ref[...]
    x2 = x1_ref[...] + rw[:, 0:1] * y0_ref[...] + rw[:, 1:2] * y1_ref[...]
    hn = _rms(x2, gple_ref[...]).astype(BF16)
    gate = _sigmoid(jnp.dot(hn, wgate_ref[...], preferred_element_type=F32))
    emb = jnp.dot(p_ref[...].astype(BF16), wproj_ref[...], preferred_element_type=F32)
    x3 = x2 + gate * emb
    o_ref[...] = _rms(x3, gfin_ref[...]) if last_layer else x3


def _final(x1, y, rw, p, w_gate, w_proj, g_ple, g_fin, last_layer, tm=256):
    m, d = x1.shape
    nb = m // tm
    row = lambda w: pl.BlockSpec((tm, w), lambda i: (i, 0))
    const = lambda a: pl.BlockSpec(a.shape, lambda i: (0, 0))
    g_ple, g_fin = g_ple.reshape(1, d), g_fin.reshape(1, d)
    return pl.pallas_call(
        functools.partial(_final_kernel, last_layer=last_layer),
        grid=(nb,),
        in_specs=[row(d), row(d), pl.BlockSpec((tm, d), lambda i: (nb + i, 0)), row(LANES), row(p.shape[1]),
                  const(w_gate), const(w_proj), const(g_ple), const(g_fin)],
        out_specs=row(d),
        out_shape=jax.ShapeDtypeStruct((m, d), F32),
        compiler_params=_params(("parallel",), 56),
        name="final",
    )(x1, y, y, rw, p, w_gate, w_proj, g_ple, g_fin)


def kernel(x, p, attn_norm, w_in, b_in, hg_lb_logits, hg_out_norm, w_hg_up, w_attn_up, w_out, ffn_norm,
           w_router_group, b_router_group, w_router_expert, b_router_expert, w_exp_gate, w_exp_up,
           w_exp_down, ple_norm, w_ple_gate, w_ple_proj, final_norm):
    batch, seq, d = x.shape
    n = batch * seq
    depth = w_in.shape[0]
    o_hf, o_at = 0, 4 * HG_WIDTH
    o_iq = o_at + 3 * AT_WIDTH
    o_ik = o_iq + IDX_HEADS * IDX_DIM
    o_iw = o_ik + IDX_DIM
    o_g = o_iw + IDX_HEADS
    lbs = jnp.cumsum(jax.nn.softmax(hg_lb_logits.astype(F32), axis=0), axis=0)
    xf = x.reshape(n, d)
    moe_tm = min(256, n // 2)
    for li in range(depth):
        w, b = w_in[li], b_in[li]
        h = _rmsnorm(xf, attn_norm[li], BF16)
        zhg = _matmul(h, w[:, o_hf:o_at].astype(BF16), b[o_hf:o_at], F32, name="proj_hgrn")
        zat = _matmul(h, w[:, o_at:o_iq].astype(BF16), b[o_at:o_iq], BF16, name="proj_attn")
        dup = lambda a: jnp.repeat(a.reshape(a.shape[:-1] + (IDX_HEADS, 1, IDX_DIM)), 2, axis=-2).reshape(
            a.shape[:-1] + (2 * IDX_HEADS * IDX_DIM,))
        ziq = _matmul(h, dup(w[:, o_iq:o_ik]).astype(BF16), dup(b[o_iq:o_ik]), F32, name="proj_iq")
        pad = LANES - IDX_DIM - IDX_HEADS
        zikw = _matmul(h, jnp.pad(w[:, o_ik:o_g], ((0, 0), (0, pad))).astype(BF16),
                       jnp.pad(b[o_ik:o_g], (0, pad)), F32, name="proj_ikw")
        gates = _matmul(h, w[:, o_g:].astype(BF16), b[o_g:], BF16, act="sigmoid", name="proj_gates")

        o_hg = _hgrn(zhg, lbs[li], hg_out_norm[li], batch, seq)
        ikt = zikw[:, :IDX_DIM].reshape(batch, seq, IDX_DIM).transpose(0, 2, 1)
        o_att = _dsa(ziq, ikt, zikw, zat, batch, seq)
        mixed = _merge(o_hg, w_hg_up[li].astype(BF16), o_att, w_attn_up[li].astype(BF16), gates)

        w_r = jnp.pad(jnp.concatenate([w_router_group[li], w_router_expert[li]], axis=1),
                      ((0, 0), (0, LANES - N_GROUPS - N_EXPERTS)))
        b_r = jnp.pad(jnp.concatenate([b_router_group[li], b_router_expert[li]]), (0, LANES - N_GROUPS - N_EXPERTS))
        x1, hn, route_e, route_w = _outproj(mixed, w_out[li].astype(BF16), xf, ffn_norm[li], w_r, b_r)
        tile_e, tile_v, pair = _dispatch_plan(route_e[:, :2], moe_tm)
        y = _experts(hn, tile_e, tile_v, pair, w_exp_gate[li], w_exp_up[li], w_exp_down[li], moe_tm)
        xf = _final(x1, y, route_w, p[li].reshape(n, -1), w_ple_gate[li].astype(BF16),
                    w_ple_proj[li].astype(BF16), ple_norm[li], final_norm, li == depth - 1)
    return xf.reshape(batch, seq, d)
```

```python
import functools

import numpy as np
import jax
import jax.numpy as jnp
from jax import lax
from jax.experimental import pallas as pl
from jax.experimental.pallas import tpu as pltpu

F32 = jnp.float32
BF16 = jnp.bfloat16
HIGHEST = lax.Precision.HIGHEST

HG_HEADS = 8
HG_DK = 128
HG_DV = 128
HG_WIDTH = HG_HEADS * HG_DK
HG_CHUNK = 64
AT_HEADS = 8
AT_HEAD_DIM = 128
AT_WIDTH = AT_HEADS * AT_HEAD_DIM
IDX_HEADS = 16
IDX_DIM = 64
TOPK_MAX = 256
N_GROUPS = 4
EXPERTS_PER_GROUP = 8
N_EXPERTS = N_GROUPS * EXPERTS_PER_GROUP
D_EXPERT = 512
RMS_EPS = 1e-6
LANES = 128
INT_MIN = -(2 ** 31)
NEG_BIG = -1e30

_NT = (((1,), (1,)), ((), ()))
_TN = (((0,), (0,)), ((), ()))


def _params(sem, vmem_mb=None):
    kw = dict(dimension_semantics=sem)
    if vmem_mb is not None:
        kw["vmem_limit_bytes"] = vmem_mb * 1024 * 1024
    return pltpu.CompilerParams(**kw)


def _sigmoid(x):
    return 1.0 / (1.0 + jnp.exp(-x))


def _rms(x, g):
    ms = jnp.mean(x * x, axis=-1, keepdims=True)
    return x * lax.rsqrt(ms + RMS_EPS) * g


def _rmsnorm_kernel(x_ref, g_ref, o_ref):
    o_ref[...] = _rms(x_ref[...], g_ref[...]).astype(o_ref.dtype)


def _rmsnorm(x, g, out_dtype, tm=512):
    m, d = x.shape
    return pl.pallas_call(
        _rmsnorm_kernel,
        grid=(m // tm,),
        in_specs=[pl.BlockSpec((tm, d), lambda i: (i, 0)), pl.BlockSpec((1, d), lambda i: (0, 0))],
        out_specs=pl.BlockSpec((tm, d), lambda i: (i, 0)),
        out_shape=jax.ShapeDtypeStruct((m, d), out_dtype),
        compiler_params=_params(("parallel",)),
        name="rmsnorm",
    )(x, g.reshape(1, d))


def _mm_kernel(a_ref, w_ref, b_ref, o_ref, *, act):
    acc = jnp.dot(a_ref[...], w_ref[...], preferred_element_type=F32) + b_ref[...]
    if act == "sigmoid":
        acc = _sigmoid(acc)
    o_ref[...] = acc.astype(o_ref.dtype)


def _matmul(a, w, b, out_dtype, act=None, tm=1024, tn=512, name="proj"):
    m, k = a.shape
    n = w.shape[1]
    tm, tn = min(tm, m), min(tn, n)
    return pl.pallas_call(
        functools.partial(_mm_kernel, act=act),
        grid=(m // tm, n // tn),
        in_specs=[pl.BlockSpec((tm, k), lambda i, j: (i, 0)),
                  pl.BlockSpec((k, tn), lambda i, j: (0, j)),
                  pl.BlockSpec((1, tn), lambda i, j: (0, j))],
        out_specs=pl.BlockSpec((tm, tn), lambda i, j: (i, j)),
        out_shape=jax.ShapeDtypeStruct((m, n), out_dtype),
        compiler_params=_params(("parallel", "arbitrary"), 48),
        name=name,
    )(a, w, b.reshape(1, n))


HG_GROUP = 8
HG_ROWS = 512


def _hgrn_tables():
    c = HG_CHUNK
    levels = [c >> i for i in range(int(np.log2(c)))]
    t = np.arange(c)
    u = np.arange(c)
    rall = [(u[None, :] <= t[:, None]), (u[None, :] > t[:, None])]
    low, mask = [], []
    for p in levels:
        half = p // 2
        m = (t // p) * p + half
        lower = (t % p) >= half
        r = np.where(lower[:, None], (u[None, :] > m[:, None]) & (u[None, :] <= t[:, None]),
                     (u[None, :] > t[:, None]) & (u[None, :] <= m[:, None]))
        rall.append(r)
        low.append(np.broadcast_to(lower[:, None], (c, LANES)))
        mask.append(((t[:, None] // p) == (t[None, :] // p)) & lower[:, None] & ~lower[None, :])
    mask.append(np.eye(c, dtype=bool))
    rall = np.concatenate(rall, 0).astype(np.float32)
    return (np.tile(rall, (1, 3)), np.stack(low).astype(np.float32), np.stack(mask).astype(np.float32),
            len(levels))


def _hgrn_kernel(q_ref, f_ref, i_ref, g_ref, lb_ref, gn_ref, rall_ref, low_ref, mask_ref, o_ref, st_ref,
                 *, n_chunks, n_levels, heads):
    c = HG_CHUNK

    @pl.when(pl.program_id(2) == 0)
    def _():
        st_ref[...] = jnp.zeros_like(st_ref)

    lb = lb_ref[...]
    one_m_lb = 1.0 - lb
    gn = gn_ref[...]

    def body(ci, carry):
        rows = pl.ds(pl.multiple_of(ci * c, c), c)
        z = f_ref[rows, :]
        q = q_ref[rows, :]
        e = jnp.exp(-jnp.abs(z))
        r = 1.0 / (1.0 + e)
        pos = z >= 0
        sig = jnp.where(pos, r, e * r)
        nsig = jnp.where(pos, e * r, r)
        logf = jnp.log(lb + one_m_lb * sig)
        k = one_m_lb * nsig
        qs = q * _sigmoid(q)
        l1 = logf.astype(BF16)
        r1 = logf - l1.astype(F32)
        l2 = r1.astype(BF16)
        l3 = (r1 - l2.astype(F32)).astype(BF16)
        ex = jnp.exp(jnp.dot(rall_ref[...], jnp.concatenate([l1, l2, l3], axis=0), preferred_element_type=F32))
        vb = i_ref[rows, :].astype(BF16)
        gate = g_ref[rows, :]
        gate = gate * _sigmoid(gate)
        for g in range(heads):
            sl = slice(g * HG_DK, (g + 1) * HG_DK)
            qg, kg, vg = qs[:, sl], k[:, sl], vb[:, sl]
            eb = ex[0:c, sl]
            a = mask_ref[n_levels] * lax.dot_general(qg.astype(BF16), kg.astype(BF16), _NT,
                                                     preferred_element_type=F32)
            for li in range(n_levels):
                w = (jnp.where(low_ref[li] != 0.0, qg, kg) * ex[(2 + li) * c:(3 + li) * c, sl]).astype(BF16)
                a = a + mask_ref[li] * lax.dot_general(w, w, _NT, preferred_element_type=F32)
            st = st_ref[g]
            o = (jnp.dot(a.astype(BF16), vg, preferred_element_type=F32)
                 + lax.dot_general((qg * eb).astype(BF16), st.astype(BF16), _NT, preferred_element_type=F32))
            st_ref[g] = st * eb[c - 1:c, :] + lax.dot_general(vg, (kg * ex[c:2 * c, sl]).astype(BF16), _TN,
                                                              preferred_element_type=F32)
            o_ref[rows, sl] = (_rms(o, gn) * gate[:, sl]).astype(o_ref.dtype)
        return carry

    lax.fori_loop(0, n_chunks, body, 0, unroll=2)


def _hgrn(zhg, lb, gn, batch, seq):
    rall, low, mask, n_levels = _hgrn_tables()
    hb = HG_HEADS // HG_GROUP
    gw = HG_GROUP * HG_DK
    tc = min(HG_ROWS, seq)
    nt = seq // tc
    col = lambda off: pl.BlockSpec((tc, gw), lambda b, hh, tt: (b * nt + tt, off + hh))
    full = lambda a: pl.BlockSpec(a.shape, lambda b, hh, tt: (0,) * a.ndim)
    return pl.pallas_call(
        functools.partial(_hgrn_kernel, n_chunks=tc // HG_CHUNK, n_levels=n_levels, heads=HG_GROUP),
        grid=(batch, hb, nt),
        in_specs=[col(0), col(hb), col(2 * hb), col(3 * hb),
                  pl.BlockSpec((1, gw), lambda b, hh, tt: (0, hh)),
                  pl.BlockSpec((1, HG_DV), lambda b, hh, tt: (0, 0)),
                  full(rall), full(low), full(mask)],
        out_specs=pl.BlockSpec((tc, gw), lambda b, hh, tt: (b * nt + tt, hh)),
        out_shape=jax.ShapeDtypeStruct((batch * seq, HG_HEADS * HG_DV), BF16),
        scratch_shapes=[pltpu.VMEM((HG_GROUP, HG_DV, HG_DK), F32)],
        compiler_params=_params(("parallel", "parallel", "arbitrary"), 48),
        name="hgrn2",
    )(zhg, zhg, zhg, zhg, lb.reshape(1, HG_WIDTH), gn.reshape(1, HG_DV),
      jnp.asarray(rall, BF16), jnp.asarray(low), jnp.asarray(mask))


def _alibi_tables(seq):
    assert 8 % AT_HEADS == 0 and seq <= 64 * 256
    pos = jnp.arange(seq, dtype=jnp.int32)
    hi, lo = (pos // 64).astype(F32), (pos % 64).astype(F32)
    one = jnp.ones((seq,), F32)
    kext = jnp.stack([one, one, hi, lo], axis=1)
    kext = jnp.pad(kext, ((0, 0), (0, AT_HEAD_DIM - 4))).astype(BF16)
    qext = []
    for h in range(AT_HEADS):
        slope = 2.0 ** (-8.0 * (h + 1) / AT_HEADS)
        cols = jnp.stack([-slope * 64.0 * hi, -slope * lo, slope * 64.0 * one, slope * one], axis=1)
        qext.append(jnp.pad(cols, ((0, 0), (0, AT_HEAD_DIM - 4))))
    return jnp.concatenate(qext, axis=1).astype(BF16), kext


def _dsa_kernel(iq_ref, ikt_ref, iw_ref, q_ref, qext_ref, k_ref, kext_ref, v_ref, o_ref, acc_ref, key_ref, bias_ref,
                *, tq, seq, ksel, col_step):
    first_row = pl.program_id(1) * tq
    for g in range(seq // col_step):
        @pl.when(first_row // col_step == g)
        def _(width=(g + 1) * col_step):
            _dsa_body(iq_ref, ikt_ref, iw_ref, q_ref, qext_ref, k_ref, kext_ref, v_ref, o_ref, acc_ref, key_ref,
                      bias_ref, tq=tq, ksel=ksel, width=width, first_row=first_row)


def _dsa_body(iq_ref, ikt_ref, iw_ref, q_ref, qext_ref, k_ref, kext_ref, v_ref, o_ref, acc_ref, key_ref, bias_ref,
              *, tq, ksel, width, first_row):
    seq = width
    t0 = first_row
    acc_ref, key_ref, bias_ref = acc_ref.at[:, :width], key_ref.at[:, :width], bias_ref.at[:, :width]
    ikt = ikt_ref[0, :, :width]
    ik_hi = ikt.astype(BF16)
    ik_lo = (ikt - ik_hi.astype(F32)).astype(BF16)
    rhs = jnp.concatenate([ik_hi, ik_hi, ik_lo, ik_lo], axis=0)
    lane = lax.broadcasted_iota(jnp.int32, (tq, LANES), 1)
    w_all = iw_ref[...] * (IDX_HEADS ** -0.5 * IDX_DIM ** -0.5)
    for h in range(IDX_HEADS):
        x = iq_ref[:, h * LANES:(h + 1) * LANES]
        x_hi = x.astype(BF16).astype(F32)
        u = jnp.where(lane < IDX_DIM, x_hi, x - x_hi).astype(BF16)
        s = jnp.dot(jnp.concatenate([u, u], axis=1), rhs, preferred_element_type=F32)
        term = jnp.maximum(s, 0.0) * w_all[:, IDX_DIM + h:IDX_DIM + h + 1]
        if h == 0:
            acc_ref[...] = term
        else:
            acc_ref[...] += term

    col = lax.broadcasted_iota(jnp.int32, (tq, seq), 1)
    row = t0 + lax.broadcasted_iota(jnp.int32, (tq, seq), 0)
    vis = col <= row
    bits = pltpu.bitcast(acc_ref[...], jnp.int32)
    key = bits ^ ((bits >> 31) & 0x7FFFFFFF)
    key_ref[...] = jnp.where(vis, key, INT_MIN)

    def bisect(it, c):
        cand = c | jnp.left_shift(jnp.int32(1), 31 - it)
        cnt = jnp.sum((key_ref[...] >= (cand ^ INT_MIN)).astype(jnp.int32), axis=1, keepdims=True)
        return jnp.where(cnt >= ksel, cand, c)

    c = lax.fori_loop(0, 32, bisect, jnp.zeros((tq, 1), jnp.int32))
    thr = c ^ INT_MIN
    bias_ref[...] = jnp.where(vis, jnp.where(key_ref[...] >= thr, 0.0, NEG_BIG), NEG_BIG)

    kext = kext_ref[:width, :]
    for h in range(AT_HEADS):
        hs = slice(h * AT_HEAD_DIM, (h + 1) * AT_HEAD_DIM)
        qa = jnp.concatenate([q_ref[:, hs], qext_ref[:, hs]], axis=1)
        ka = jnp.concatenate([k_ref[:width, hs], kext], axis=1)
        logit = lax.dot_general(qa, ka, _NT, preferred_element_type=F32) + bias_ref[...]
        m = jnp.max(logit, axis=1, keepdims=True)
        p = jnp.exp(logit - m)
        den = jnp.sum(p, axis=1, keepdims=True)
        oh = jnp.dot(p.astype(BF16), v_ref[:width, hs], preferred_element_type=F32) / den
        o_ref[:, hs] = oh.astype(o_ref.dtype)


def _dsa(iq_dup, ikt, ikw, zat, batch, seq, tq=128):
    nq = seq // tq
    ksel = min(TOPK_MAX, seq // 4)
    col_step = max(seq // 4, tq)
    qext, kext = _alibi_tables(seq)
    return pl.pallas_call(
        functools.partial(_dsa_kernel, tq=tq, seq=seq, ksel=ksel, col_step=col_step),
        grid=(batch, nq),
        in_specs=[pl.BlockSpec((tq, IDX_HEADS * LANES), lambda b, i: (b * nq + i, 0)),
                  pl.BlockSpec((1, IDX_DIM, seq), lambda b, i: (b, 0, 0)),
                  pl.BlockSpec((tq, LANES), lambda b, i: (b * nq + i, 0)),
                  pl.BlockSpec((tq, AT_WIDTH), lambda b, i: (b * nq + i, 0)),
                  pl.BlockSpec((tq, AT_WIDTH), lambda b, i: (i, 0)),
                  pl.BlockSpec((seq, AT_WIDTH), lambda b, i: (b, 1)),
                  pl.BlockSpec((seq, AT_HEAD_DIM), lambda b, i: (0, 0)),
                  pl.BlockSpec((seq, AT_WIDTH), lambda b, i: (b, 2))],
        out_specs=pl.BlockSpec((tq, AT_WIDTH), lambda b, i: (b * nq + i, 0)),
        out_shape=jax.ShapeDtypeStruct((batch * seq, AT_WIDTH), BF16),
        scratch_shapes=[pltpu.VMEM((tq, seq), F32), pltpu.VMEM((tq, seq), jnp.int32),
                        pltpu.VMEM((tq, seq), F32)],
        compiler_params=_params(("parallel", "arbitrary"), 48),
        name="dsa",
    )(iq_dup, ikt, ikw, zat, qext, zat, kext, zat)


def _merge_kernel(a1_ref, w1_ref, a2_ref, w2_ref, g1_ref, g2_ref, o_ref):
    y1 = jnp.dot(a1_ref[...], w1_ref[...], preferred_element_type=F32)
    y2 = jnp.dot(a2_ref[...], w2_ref[...], preferred_element_type=F32)
    o_ref[...] = (g1_ref[...].astype(F32) * y1 + g2_ref[...].astype(F32) * y2).astype(o_ref.dtype)


def _merge(o_hg, w_hg, o_at, w_at, gates, tm=1024, tn=512):
    m, k1 = o_hg.shape
    k2 = o_at.shape[1]
    n = w_hg.shape[1]
    tm = min(tm, m)
    nb = n // tn
    return pl.pallas_call(
        _merge_kernel,
        grid=(m // tm, nb),
        in_specs=[pl.BlockSpec((tm, k1), lambda i, j: (i, 0)), pl.BlockSpec((k1, tn), lambda i, j: (0, j)),
                  pl.BlockSpec((tm, k2), lambda i, j: (i, 0)), pl.BlockSpec((k2, tn), lambda i, j: (0, j)),
                  pl.BlockSpec((tm, tn), lambda i, j: (i, j)), pl.BlockSpec((tm, tn), lambda i, j: (i, nb + j))],
        out_specs=pl.BlockSpec((tm, tn), lambda i, j: (i, j)),
        out_shape=jax.ShapeDtypeStruct((m, n), BF16),
        compiler_params=_params(("parallel", "arbitrary"), 48),
        name="merge",
    )(o_hg, w_hg, o_at, w_at, gates, gates)


def _outproj_kernel(m_ref, w_ref, x_ref, gn_ref, wr_ref, br_ref, x1_ref, hn_ref, ri_ref, rw_ref):
    x1 = x_ref[...] + jnp.dot(m_ref[...], w_ref[...], preferred_element_type=F32)
    x1_ref[...] = x1
    hn = _rms(x1, gn_ref[...])
    hn_ref[...] = hn
    logit = jnp.dot(hn, wr_ref[...], precision=HIGHEST, preferred_element_type=F32) + br_ref[...]
    lane = lax.broadcasted_iota(jnp.int32, logit.shape, 1)
    lane_f = lane.astype(F32)
    is_g = lane < N_GROUPS
    lg = jnp.where(is_g, logit, -jnp.inf)
    mg = jnp.max(lg, axis=1, keepdims=True)
    pg_top = 1.0 / jnp.sum(jnp.exp(lg - mg), axis=1, keepdims=True)
    gid = jnp.min(jnp.where(lg == mg, lane_f, float(LANES)), axis=1, keepdims=True).astype(jnp.int32)
    in_grp = ((lane - N_GROUPS) >> 3) == gid
    le = jnp.where(in_grp, logit, -jnp.inf)
    m1 = jnp.max(le, axis=1, keepdims=True)
    i1 = jnp.min(jnp.where(le == m1, lane_f, float(LANES)), axis=1, keepdims=True)
    le2 = jnp.where(lane_f == i1, -jnp.inf, le)
    m2 = jnp.max(le2, axis=1, keepdims=True)
    i2 = jnp.min(jnp.where(le2 == m2, lane_f, float(LANES)), axis=1, keepdims=True)
    r = jnp.exp(m2 - m1)
    w1 = pg_top / (1.0 + r)
    w2 = pg_top * r / (1.0 + r)
    e1 = i1.astype(jnp.int32) - N_GROUPS
    e2 = i2.astype(jnp.int32) - N_GROUPS
    ri_ref[...] = jnp.where(lane == 0, e1, jnp.where(lane == 1, e2, 0))
    rw_ref[...] = jnp.where(lane == 0, w1, jnp.where(lane == 1, w2, 0.0))


def _outproj(mixed, w_out, x, gn, w_r, b_r, tm=256):
    m, d = x.shape
    row = lambda w: pl.BlockSpec((tm, w), lambda i: (i, 0))
    const = lambda a: pl.BlockSpec(a.shape, lambda i: (0, 0))
    gn = gn.reshape(1, d)
    b_r = b_r.reshape(1, LANES)
    return pl.pallas_call(
        _outproj_kernel,
        grid=(m // tm,),
        in_specs=[row(d), const(w_out), row(d), const(gn), const(w_r), const(b_r)],
        out_specs=[row(d), row(d), row(LANES), row(LANES)],
        out_shape=[jax.ShapeDtypeStruct((m, d), F32), jax.ShapeDtypeStruct((m, d), F32),
                   jax.ShapeDtypeStruct((m, LANES), jnp.int32), jax.ShapeDtypeStruct((m, LANES), F32)],
        compiler_params=_params(("parallel",), 56),
        name="outproj_router",
    )(mixed, w_out, x, gn, w_r, b_r)


def _expert_kernel(te_ref, tv_ref, pair_ref, hn_hbm, wg_ref, wu_ref, wd_ref, y_hbm,
                   xbuf, ybuf, wgb, wub, wdb, gsem, ssem, *, tm, n_tok):
    t = pl.program_id(0)
    nv = tv_ref[t]
    base = t * tm

    @pl.when(nv > 0)
    def _():
        def gather(r, carry):
            tok = lax.rem(pair_ref[base + r], n_tok)
            pltpu.make_async_copy(hn_hbm.at[pl.ds(tok, 1), :], xbuf.at[pl.ds(r, 1), :], gsem).start()
            return carry

        lax.fori_loop(0, tm, gather, 0)

        @pl.when(jnp.logical_or(t == 0, te_ref[t] != te_ref[jnp.maximum(t - 1, 0)]))
        def _():
            wgb[...] = wg_ref[0].astype(BF16)
            wub[...] = wu_ref[0].astype(BF16)
            wdb[...] = wd_ref[0].astype(BF16)

        pltpu.make_async_copy(hn_hbm.at[pl.ds(0, tm), :], xbuf, gsem).wait()
        xb = xbuf[...].astype(BF16)
        g = jnp.dot(xb, wgb[...], preferred_element_type=F32)
        u = jnp.dot(xb, wub[...], preferred_element_type=F32)
        hid = (g * _sigmoid(g) * u).astype(BF16)
        ybuf[...] = jnp.dot(hid, wdb[...], preferred_element_type=F32)

        def scatter(r, carry):
            dst = pair_ref[base + r]
            pltpu.make_async_copy(ybuf.at[pl.ds(r, 1), :], y_hbm.at[pl.ds(dst, 1), :], ssem).start()
            return carry

        lax.fori_loop(0, nv, scatter, 0)

        def drain(r, carry):
            pltpu.make_async_copy(ybuf.at[pl.ds(0, 1), :], y_hbm.at[pl.ds(0, 1), :], ssem).wait()
            return carry

        lax.fori_loop(0, nv, drain, 0)


def _experts(hn, tile_expert, tile_valid, pair, w_g, w_u, w_d, tm):
    n_tok, d = hn.shape
    n_tiles = tile_expert.shape[0]
    wspec = lambda a: pl.BlockSpec((1,) + a.shape[1:], lambda t, te, tv, pr: (te[t], 0, 0))
    return pl.pallas_call(
        functools.partial(_expert_kernel, tm=tm, n_tok=n_tok),
        grid_spec=pltpu.PrefetchScalarGridSpec(
            num_scalar_prefetch=3,
            grid=(n_tiles,),
            in_specs=[pl.BlockSpec(memory_space=pl.ANY), wspec(w_g), wspec(w_u), wspec(w_d)],
            out_specs=pl.BlockSpec(memory_space=pl.ANY),
            scratch_shapes=[pltpu.VMEM((tm, d), F32), pltpu.VMEM((tm, d), F32),
                            pltpu.VMEM(w_g.shape[1:], BF16), pltpu.VMEM(w_u.shape[1:], BF16),
                            pltpu.VMEM(w_d.shape[1:], BF16),
                            pltpu.SemaphoreType.DMA, pltpu.SemaphoreType.DMA]),
        out_shape=jax.ShapeDtypeStruct((2 * n_tok, d), F32),
        compiler_params=_params(("arbitrary",), 56),
        name="experts",
    )(tile_expert, tile_valid, pair, hn, w_g, w_u, w_d)


def _dispatch_plan(route_e, tm):
    n_tok = route_e.shape[0]
    n_pairs = 2 * n_tok
    n_tiles = n_pairs // tm + N_EXPERTS
    eflat = route_e.T.reshape(-1)
    order = jnp.argsort(eflat, stable=True).astype(jnp.int32)
    counts = jnp.sum(eflat[:, None] == jnp.arange(N_EXPERTS, dtype=jnp.int32)[None, :], axis=0, dtype=jnp.int32)
    tiles_e = (counts + tm - 1) // tm
    tile_end = jnp.cumsum(tiles_e)
    tile_off = tile_end - tiles_e
    start = jnp.cumsum(counts) - counts
    tid = jnp.arange(n_tiles, dtype=jnp.int32)
    te = jnp.minimum(jnp.sum(tile_end[None, :] <= tid[:, None], axis=1), N_EXPERTS - 1).astype(jnp.int32)
    tv = jnp.clip(counts[te] - (tid - tile_off[te]) * tm, 0, tm)
    used = tid < tile_end[-1]
    last_e = te[jnp.maximum(tile_end[-1] - 1, 0)]
    te = jnp.where(used, te, last_e).astype(jnp.int32)
    tv = jnp.where(used, tv, 0).astype(jnp.int32)
    r = jnp.arange(tm, dtype=jnp.int32)[None, :]
    src = (start[te] + (tid - tile_off[te]) * tm)[:, None] + r
    pair = jnp.where(r < tv[:, None], order[jnp.clip(src, 0, n_pairs - 1)], 0).reshape(-1)
    return te, tv, pair.astype(jnp.int32)


def _final_kernel(x1_ref, y0_ref, y1_ref, rw_ref, p_ref, wgate_ref, wproj_ref, gple_ref, gfin_ref, o_ref,
                  *, last_layer):
    rw = rw_ref[...]
    x2 = x1_ref[...] + rw[:, 0:1] * y0_ref[...] + rw[:, 1:2] * y1_ref[...]
    hn = _rms(x2, gple_ref[...]).astype(BF16)
    gate = _sigmoid(jnp.dot(hn, wgate_ref[...], preferred_element_type=F32))
    emb = jnp.dot(p_ref[...].astype(BF16), wproj_ref[...], preferred_element_type=F32)
    x3 = x2 + gate * emb
    o_ref[...] = _rms(x3, gfin_ref[...]) if last_layer else x3


def _final(x1, y, rw, p, w_gate, w_proj, g_ple, g_fin, last_layer, tm=256):
    m, d = x1.shape
    nb = m // tm
    row = lambda w: pl.BlockSpec((tm, w), lambda i: (i, 0))
    const = lambda a: pl.BlockSpec(a.shape, lambda i: (0, 0))
    g_ple, g_fin = g_ple.reshape(1, d), g_fin.reshape(1, d)
    return pl.pallas_call(
        functools.partial(_final_kernel, last_layer=last_layer),
        grid=(nb,),
        in_specs=[row(d), row(d), pl.BlockSpec((tm, d), lambda i: (nb + i, 0)), row(LANES), row(p.shape[1]),
                  const(w_gate), const(w_proj), const(g_ple), const(g_fin)],
        out_specs=row(d),
        out_shape=jax.ShapeDtypeStruct((m, d), F32),
        compiler_params=_params(("parallel",), 56),
        name="final",
    )(x1, y, y, rw, p, w_gate, w_proj, g_ple, g_fin)


def kernel(x, p, attn_norm, w_in, b_in, hg_lb_logits, hg_out_norm, w_hg_up, w_attn_up, w_out, ffn_norm,
           w_router_group, b_router_group, w_router_expert, b_router_expert, w_exp_gate, w_exp_up,
           w_exp_down, ple_norm, w_ple_gate, w_ple_proj, final_norm):
    batch, seq, d = x.shape
    n = batch * seq
    depth = w_in.shape[0]
    o_hf, o_at = 0, 4 * HG_WIDTH
    o_iq = o_at + 3 * AT_WIDTH
    o_ik = o_iq + IDX_HEADS * IDX_DIM
    o_iw = o_ik + IDX_DIM
    o_g = o_iw + IDX_HEADS
    lbs = jnp.cumsum(jax.nn.softmax(hg_lb_logits.astype(F32), axis=0), axis=0)
    xf = x.reshape(n, d)
    moe_tm = min(256, n // 2)
    for li in range(depth):
        w, b = w_in[li], b_in[li]
        h = _rmsnorm(xf, attn_norm[li], BF16)
        zhg = _matmul(h, w[:, o_hf:o_at].astype(BF16), b[o_hf:o_at], F32, name="proj_hgrn")
        qkv_scale = jnp.concatenate([jnp.full((AT_WIDTH,), AT_HEAD_DIM ** -0.5, F32), jnp.ones((2 * AT_WIDTH,), F32)])
        zat = _matmul(h, (w[:, o_at:o_iq] * qkv_scale).astype(BF16), b[o_at:o_iq] * qkv_scale, BF16,
                      name="proj_attn")
        dup = lambda a: jnp.repeat(a.reshape(a.shape[:-1] + (IDX_HEADS, 1, IDX_DIM)), 2, axis=-2).reshape(
            a.shape[:-1] + (2 * IDX_HEADS * IDX_DIM,))
        ziq = _matmul(h, dup(w[:, o_iq:o_ik]).astype(BF16), dup(b[o_iq:o_ik]), F32, name="proj_iq")
        pad = LANES - IDX_DIM - IDX_HEADS
        zikw = _matmul(h, jnp.pad(w[:, o_ik:o_g], ((0, 0), (0, pad))).astype(BF16),
                       jnp.pad(b[o_ik:o_g], (0, pad)), F32, name="proj_ikw")
        gates = _matmul(h, w[:, o_g:].astype(BF16), b[o_g:], BF16, act="sigmoid", name="proj_gates")

        o_hg = _hgrn(zhg, lbs[li], hg_out_norm[li], batch, seq)
        ikt = zikw[:, :IDX_DIM].reshape(batch, seq, IDX_DIM).transpose(0, 2, 1)
        o_att = _dsa(ziq, ikt, zikw, zat, batch, seq)
        mixed = _merge(o_hg, w_hg_up[li].astype(BF16), o_att, w_attn_up[li].astype(BF16), gates)

        w_r = jnp.pad(jnp.concatenate([w_router_group[li], w_router_expert[li]], axis=1),
                      ((0, 0), (0, LANES - N_GROUPS - N_EXPERTS)))
        b_r = jnp.pad(jnp.concatenate([b_router_group[li], b_router_expert[li]]), (0, LANES - N_GROUPS - N_EXPERTS))
        x1, hn, route_e, route_w = _outproj(mixed, w_out[li].astype(BF16), xf, ffn_norm[li], w_r, b_r)
        tile_e, tile_v, pair = _dispatch_plan(route_e[:, :2], moe_tm)
        y = _experts(hn, tile_e, tile_v, pair, w_exp_gate[li], w_exp_up[li], w_exp_down[li], moe_tm)
        xf = _final(x1, y, route_w, p[li].reshape(n, -1), w_ple_gate[li].astype(BF16),
                    w_ple_proj[li].astype(BF16), ple_norm[li], final_norm, li == depth - 1)
    return xf.reshape(batch, seq, d)
```

```python
import functools

import numpy as np
import jax
import jax.numpy as jnp
from jax import lax
from jax.experimental import pallas as pl
from jax.experimental.pallas import tpu as pltpu

F32 = jnp.float32
BF16 = jnp.bfloat16
HIGHEST = lax.Precision.HIGHEST

HG_HEADS = 8
HG_DK = 128
HG_DV = 128
HG_WIDTH = HG_HEADS * HG_DK
HG_CHUNK = 64
AT_HEADS = 8
AT_HEAD_DIM = 128
AT_WIDTH = AT_HEADS * AT_HEAD_DIM
IDX_HEADS = 16
IDX_DIM = 64
TOPK_MAX = 256
N_GROUPS = 4
EXPERTS_PER_GROUP = 8
N_EXPERTS = N_GROUPS * EXPERTS_PER_GROUP
D_EXPERT = 512
RMS_EPS = 1e-6
LANES = 128
INT_MIN = -(2 ** 31)
NEG_BIG = -1e30

_NT = (((1,), (1,)), ((), ()))
_TN = (((0,), (0,)), ((), ()))


def _params(sem, vmem_mb=None):
    kw = dict(dimension_semantics=sem)
    if vmem_mb is not None:
        kw["vmem_limit_bytes"] = vmem_mb * 1024 * 1024
    return pltpu.CompilerParams(**kw)


def _sigmoid(x):
    return 1.0 / (1.0 + jnp.exp(-x))


def _rms(x, g):
    ms = jnp.mean(x * x, axis=-1, keepdims=True)
    return x * lax.rsqrt(ms + RMS_EPS) * g


def _rmsnorm_kernel(x_ref, g_ref, o_ref):
    o_ref[...] = _rms(x_ref[...], g_ref[...]).astype(o_ref.dtype)


def _rmsnorm(x, g, out_dtype, tm=512):
    m, d = x.shape
    return pl.pallas_call(
        _rmsnorm_kernel,
        grid=(m // tm,),
        in_specs=[pl.BlockSpec((tm, d), lambda i: (i, 0)), pl.BlockSpec((1, d), lambda i: (0, 0))],
        out_specs=pl.BlockSpec((tm, d), lambda i: (i, 0)),
        out_shape=jax.ShapeDtypeStruct((m, d), out_dtype),
        compiler_params=_params(("parallel",)),
        name="rmsnorm",
    )(x, g.reshape(1, d))


def _mm_kernel(a_ref, w_ref, b_ref, o_ref, *, act):
    acc = jnp.dot(a_ref[...], w_ref[...], preferred_element_type=F32) + b_ref[...]
    if act == "sigmoid":
        acc = _sigmoid(acc)
    o_ref[...] = acc.astype(o_ref.dtype)


def _matmul(a, w, b, out_dtype, act=None, tm=1024, tn=512, name="proj"):
    m, k = a.shape
    n = w.shape[1]
    tm, tn = min(tm, m), min(tn, n)
    return pl.pallas_call(
        functools.partial(_mm_kernel, act=act),
        grid=(m // tm, n // tn),
        in_specs=[pl.BlockSpec((tm, k), lambda i, j: (i, 0)),
                  pl.BlockSpec((k, tn), lambda i, j: (0, j)),
                  pl.BlockSpec((1, tn), lambda i, j: (0, j))],
        out_specs=pl.BlockSpec((tm, tn), lambda i, j: (i, j)),
        out_shape=jax.ShapeDtypeStruct((m, n), out_dtype),
        compiler_params=_params(("parallel", "arbitrary"), 48),
        name=name,
    )(a, w, b.reshape(1, n))


HG_GROUP = 8
HG_ROWS = 512


def _hgrn_tables():
    c = HG_CHUNK
    levels = [c >> i for i in range(int(np.log2(c)))]
    t = np.arange(c)
    u = np.arange(c)
    rall = [(u[None, :] <= t[:, None]), (u[None, :] > t[:, None])]
    low, mask = [], []
    for p in levels:
        half = p // 2
        m = (t // p) * p + half
        lower = (t % p) >= half
        r = np.where(lower[:, None], (u[None, :] > m[:, None]) & (u[None, :] <= t[:, None]),
                     (u[None, :] > t[:, None]) & (u[None, :] <= m[:, None]))
        rall.append(r)
        low.append(np.broadcast_to(lower[:, None], (c, LANES)))
        mask.append(((t[:, None] // p) == (t[None, :] // p)) & lower[:, None] & ~lower[None, :])
    mask.append(np.eye(c, dtype=bool))
    rall = np.concatenate(rall, 0).astype(np.float32)
    return (np.tile(rall, (1, 3)), np.stack(low).astype(np.float32), np.stack(mask).astype(np.float32),
            len(levels))


def _hgrn_kernel(q_ref, f_ref, i_ref, g_ref, lb_ref, gn_ref, rall_ref, low_ref, mask_ref, o_ref, st_ref,
                 *, n_chunks, n_levels, heads):
    c = HG_CHUNK

    @pl.when(pl.program_id(2) == 0)
    def _():
        st_ref[...] = jnp.zeros_like(st_ref)

    lb = lb_ref[...]
    one_m_lb = 1.0 - lb
    gn = gn_ref[...]

    def body(ci, carry):
        rows = pl.ds(pl.multiple_of(ci * c, c), c)
        z = f_ref[rows, :]
        q = q_ref[rows, :]
        e = jnp.exp(-jnp.abs(z))
        r = 1.0 / (1.0 + e)
        pos = z >= 0
        sig = jnp.where(pos, r, e * r)
        nsig = jnp.where(pos, e * r, r)
        logf = jnp.log(lb + one_m_lb * sig)
        k = one_m_lb * nsig
        qs = q * _sigmoid(q)
        l1 = logf.astype(BF16)
        r1 = logf - l1.astype(F32)
        l2 = r1.astype(BF16)
        l3 = (r1 - l2.astype(F32)).astype(BF16)
        ex = jnp.exp(jnp.dot(rall_ref[...], jnp.concatenate([l1, l2, l3], axis=0), preferred_element_type=F32))
        vb = i_ref[rows, :].astype(BF16)
        gate = g_ref[rows, :]
        gate = gate * _sigmoid(gate)
        for g in range(heads):
            sl = slice(g * HG_DK, (g + 1) * HG_DK)
            qg, kg, vg = qs[:, sl], k[:, sl], vb[:, sl]
            eb = ex[0:c, sl]
            a = mask_ref[n_levels] * lax.dot_general(qg.astype(BF16), kg.astype(BF16), _NT,
                                                     preferred_element_type=F32)
            for li in range(n_levels):
                w = (jnp.where(low_ref[li] != 0.0, qg, kg) * ex[(2 + li) * c:(3 + li) * c, sl]).astype(BF16)
                a = a + mask_ref[li] * lax.dot_general(w, w, _NT, preferred_element_type=F32)
            st = st_ref[g]
            o = (jnp.dot(a.astype(BF16), vg, preferred_element_type=F32)
                 + lax.dot_general((qg * eb).astype(BF16), st.astype(BF16), _NT, preferred_element_type=F32))
            st_ref[g] = st * eb[c - 1:c, :] + lax.dot_general(vg, (kg * ex[c:2 * c, sl]).astype(BF16), _TN,
                                                              preferred_element_type=F32)
            o_ref[rows, sl] = (_rms(o, gn) * gate[:, sl]).astype(o_ref.dtype)
        return carry

    lax.fori_loop(0, n_chunks, body, 0, unroll=2)


def _hgrn(zhg, lb, gn, batch, seq):
    rall, low, mask, n_levels = _hgrn_tables()
    hb = HG_HEADS // HG_GROUP
    gw = HG_GROUP * HG_DK
    tc = min(HG_ROWS, seq)
    nt = seq // tc
    col = lambda off: pl.BlockSpec((tc, gw), lambda b, hh, tt: (b * nt + tt, off + hh))
    full = lambda a: pl.BlockSpec(a.shape, lambda b, hh, tt: (0,) * a.ndim)
    return pl.pallas_call(
        functools.partial(_hgrn_kernel, n_chunks=tc // HG_CHUNK, n_levels=n_levels, heads=HG_GROUP),
        grid=(batch, hb, nt),
        in_specs=[col(0), col(hb), col(2 * hb), col(3 * hb),
                  pl.BlockSpec((1, gw), lambda b, hh, tt: (0, hh)),
                  pl.BlockSpec((1, HG_DV), lambda b, hh, tt: (0, 0)),
                  full(rall), full(low), full(mask)],
        out_specs=pl.BlockSpec((tc, gw), lambda b, hh, tt: (b * nt + tt, hh)),
        out_shape=jax.ShapeDtypeStruct((batch * seq, HG_HEADS * HG_DV), BF16),
        scratch_shapes=[pltpu.VMEM((HG_GROUP, HG_DV, HG_DK), F32)],
        compiler_params=_params(("parallel", "parallel", "arbitrary"), 48),
        name="hgrn2",
    )(zhg, zhg, zhg, zhg, lb.reshape(1, HG_WIDTH), gn.reshape(1, HG_DV),
      jnp.asarray(rall, BF16), jnp.asarray(low), jnp.asarray(mask))


def _alibi_tables(seq):
    assert 8 % AT_HEADS == 0 and seq <= 64 * 256
    pos = jnp.arange(seq, dtype=jnp.int32)
    hi, lo = (pos // 64).astype(F32), (pos % 64).astype(F32)
    one = jnp.ones((seq,), F32)
    kext = jnp.stack([one, one, hi, lo], axis=1)
    kext = jnp.pad(kext, ((0, 0), (0, AT_HEAD_DIM - 4))).astype(BF16)
    qext = []
    for h in range(AT_HEADS):
        slope = 2.0 ** (-8.0 * (h + 1) / AT_HEADS)
        cols = jnp.stack([-slope * 64.0 * hi, -slope * lo, slope * 64.0 * one, slope * one], axis=1)
        qext.append(jnp.pad(cols, ((0, 0), (0, AT_HEAD_DIM - 4))))
    return jnp.concatenate(qext, axis=1).astype(BF16), kext


def _dsa_kernel(iq_ref, ikt_ref, iw_ref, q_ref, qext_ref, k_ref, kext_ref, v_ref, o_ref, acc_ref, key_ref, bias_ref,
                *, tq, seq, ksel, col_step):
    first_row = pl.program_id(1) * tq
    for g in range(seq // col_step):
        @pl.when(first_row // col_step == g)
        def _(width=(g + 1) * col_step):
            _dsa_body(iq_ref, ikt_ref, iw_ref, q_ref, qext_ref, k_ref, kext_ref, v_ref, o_ref, acc_ref, key_ref,
                      bias_ref, tq=tq, ksel=ksel, width=width, first_row=first_row)


def _dsa_body(iq_ref, ikt_ref, iw_ref, q_ref, qext_ref, k_ref, kext_ref, v_ref, o_ref, acc_ref, key_ref, bias_ref,
              *, tq, ksel, width, first_row):
    seq = width
    t0 = first_row
    acc_ref, key_ref, bias_ref = acc_ref.at[:, :width], key_ref.at[:, :width], bias_ref.at[:, :width]
    ikt = ikt_ref[0, :, :width]
    ik_hi = ikt.astype(BF16)
    ik_lo = (ikt - ik_hi.astype(F32)).astype(BF16)
    rhs = jnp.concatenate([ik_hi, ik_hi, ik_lo, ik_lo], axis=0)
    lane = lax.broadcasted_iota(jnp.int32, (tq, LANES), 1)
    w_all = iw_ref[...] * (IDX_HEADS ** -0.5 * IDX_DIM ** -0.5)
    for h in range(IDX_HEADS):
        x = iq_ref[:, h * LANES:(h + 1) * LANES]
        x_hi = x.astype(BF16).astype(F32)
        u = jnp.where(lane < IDX_DIM, x_hi, x - x_hi).astype(BF16)
        s = jnp.dot(jnp.concatenate([u, u], axis=1), rhs, preferred_element_type=F32)
        term = jnp.maximum(s, 0.0) * w_all[:, IDX_DIM + h:IDX_DIM + h + 1]
        if h == 0:
            acc_ref[...] = term
        else:
            acc_ref[...] += term

    col = lax.broadcasted_iota(jnp.int32, (tq, seq), 1)
    row = t0 + lax.broadcasted_iota(jnp.int32, (tq, seq), 0)
    vis = col <= row
    bits = pltpu.bitcast(acc_ref[...], jnp.int32)
    key = bits ^ ((bits >> 31) & 0x7FFFFFFF)
    key_ref[...] = jnp.where(vis, key, INT_MIN)

    def bisect(it, c):
        cand = c | jnp.left_shift(jnp.int32(1), 31 - it)
        cnt = jnp.sum((key_ref[...] >= (cand ^ INT_MIN)).astype(jnp.int32), axis=1, keepdims=True)
        return jnp.where(cnt >= ksel, cand, c)

    c = lax.fori_loop(0, 32, bisect, jnp.zeros((tq, 1), jnp.int32))
    thr = c ^ INT_MIN
    bias_ref[...] = jnp.where(vis, jnp.where(key_ref[...] >= thr, 0.0, NEG_BIG), NEG_BIG)

    kext = kext_ref[:width, :]
    for h in range(AT_HEADS):
        hs = slice(h * AT_HEAD_DIM, (h + 1) * AT_HEAD_DIM)
        qa = jnp.concatenate([q_ref[:, hs], qext_ref[:, hs]], axis=1)
        ka = jnp.concatenate([k_ref[:width, hs], kext], axis=1)
        logit = lax.dot_general(qa, ka, _NT, preferred_element_type=F32) + bias_ref[...]
        m = jnp.max(logit, axis=1, keepdims=True)
        p = jnp.exp(logit - m)
        den = jnp.sum(p, axis=1, keepdims=True)
        oh = jnp.dot(p.astype(BF16), v_ref[:width, hs], preferred_element_type=F32) / den
        o_ref[:, hs] = oh.astype(o_ref.dtype)


def _dsa(iq_dup, ikt, ikw, zat, batch, seq, tq=128):
    nq = seq // tq
    ksel = min(TOPK_MAX, seq // 4)
    col_step = max(seq // 4, tq)
    qext, kext = _alibi_tables(seq)
    return pl.pallas_call(
        functools.partial(_dsa_kernel, tq=tq, seq=seq, ksel=ksel, col_step=col_step),
        grid=(batch, nq),
        in_specs=[pl.BlockSpec((tq, IDX_HEADS * LANES), lambda b, i: (b * nq + i, 0)),
                  pl.BlockSpec((1, IDX_DIM, seq), lambda b, i: (b, 0, 0)),
                  pl.BlockSpec((tq, LANES), lambda b, i: (b * nq + i, 0)),
                  pl.BlockSpec((tq, AT_WIDTH), lambda b, i: (b * nq + i, 0)),
                  pl.BlockSpec((tq, AT_WIDTH), lambda b, i: (i, 0)),
                  pl.BlockSpec((seq, AT_WIDTH), lambda b, i: (b, 1)),
                  pl.BlockSpec((seq, AT_HEAD_DIM), lambda b, i: (0, 0)),
                  pl.BlockSpec((seq, AT_WIDTH), lambda b, i: (b, 2))],
        out_specs=pl.BlockSpec((tq, AT_WIDTH), lambda b, i: (b * nq + i, 0)),
        out_shape=jax.ShapeDtypeStruct((batch * seq, AT_WIDTH), BF16),
        scratch_shapes=[pltpu.VMEM((tq, seq), F32), pltpu.VMEM((tq, seq), jnp.int32),
                        pltpu.VMEM((tq, seq), F32)],
        compiler_params=_params(("parallel", "arbitrary"), 48),
        name="dsa",
    )(iq_dup, ikt, ikw, zat, qext, zat, kext, zat)


def _merge_kernel(a1_ref, w1_ref, a2_ref, w2_ref, g1_ref, g2_ref, o_ref):
    y1 = jnp.dot(a1_ref[...], w1_ref[...], preferred_element_type=F32)
    y2 = jnp.dot(a2_ref[...], w2_ref[...], preferred_element_type=F32)
    o_ref[...] = (g1_ref[...].astype(F32) * y1 + g2_ref[...].astype(F32) * y2).astype(o_ref.dtype)


def _merge(o_hg, w_hg, o_at, w_at, gates, tm=1024, tn=512):
    m, k1 = o_hg.shape
    k2 = o_at.shape[1]
    n = w_hg.shape[1]
    tm = min(tm, m)
    nb = n // tn
    return pl.pallas_call(
        _merge_kernel,
        grid=(m // tm, nb),
        in_specs=[pl.BlockSpec((tm, k1), lambda i, j: (i, 0)), pl.BlockSpec((k1, tn), lambda i, j: (0, j)),
                  pl.BlockSpec((tm, k2), lambda i, j: (i, 0)), pl.BlockSpec((k2, tn), lambda i, j: (0, j)),
                  pl.BlockSpec((tm, tn), lambda i, j: (i, j)), pl.BlockSpec((tm, tn), lambda i, j: (i, nb + j))],
        out_specs=pl.BlockSpec((tm, tn), lambda i, j: (i, j)),
        out_shape=jax.ShapeDtypeStruct((m, n), BF16),
        compiler_params=_params(("parallel", "arbitrary"), 48),
        name="merge",
    )(o_hg, w_hg, o_at, w_at, gates, gates)


def _outproj_kernel(m_ref, w_ref, x_ref, gn_ref, wr_ref, br_ref, x1_ref, hn_ref, ri_ref, rw_ref):
    x1 = x_ref[...] + jnp.dot(m_ref[...], w_ref[...], preferred_element_type=F32)
    x1_ref[...] = x1
    hn = _rms(x1, gn_ref[...])
    hn_ref[...] = hn
    logit = jnp.dot(hn, wr_ref[...], precision=HIGHEST, preferred_element_type=F32) + br_ref[...]
    lane = lax.broadcasted_iota(jnp.int32, logit.shape, 1)
    lane_f = lane.astype(F32)
    is_g = lane < N_GROUPS
    lg = jnp.where(is_g, logit, -jnp.inf)
    mg = jnp.max(lg, axis=1, keepdims=True)
    pg_top = 1.0 / jnp.sum(jnp.exp(lg - mg), axis=1, keepdims=True)
    gid = jnp.min(jnp.where(lg == mg, lane_f, float(LANES)), axis=1, keepdims=True).astype(jnp.int32)
    in_grp = ((lane - N_GROUPS) >> 3) == gid
    le = jnp.where(in_grp, logit, -jnp.inf)
    m1 = jnp.max(le, axis=1, keepdims=True)
    i1 = jnp.min(jnp.where(le == m1, lane_f, float(LANES)), axis=1, keepdims=True)
    le2 = jnp.where(lane_f == i1, -jnp.inf, le)
    m2 = jnp.max(le2, axis=1, keepdims=True)
    i2 = jnp.min(jnp.where(le2 == m2, lane_f, float(LANES)), axis=1, keepdims=True)
    r = jnp.exp(m2 - m1)
    w1 = pg_top / (1.0 + r)
    w2 = pg_top * r / (1.0 + r)
    e1 = i1.astype(jnp.int32) - N_GROUPS
    e2 = i2.astype(jnp.int32) - N_GROUPS
    ri_ref[...] = jnp.where(lane == 0, e1, jnp.where(lane == 1, e2, 0))
    rw_ref[...] = jnp.where(lane == 0, w1, jnp.where(lane == 1, w2, 0.0))


def _outproj(mixed, w_out, x, gn, w_r, b_r, tm=256):
    m, d = x.shape
    row = lambda w: pl.BlockSpec((tm, w), lambda i: (i, 0))
    const = lambda a: pl.BlockSpec(a.shape, lambda i: (0, 0))
    gn = gn.reshape(1, d)
    b_r = b_r.reshape(1, LANES)
    return pl.pallas_call(
        _outproj_kernel,
        grid=(m // tm,),
        in_specs=[row(d), const(w_out), row(d), const(gn), const(w_r), const(b_r)],
        out_specs=[row(d), row(d), row(LANES), row(LANES)],
        out_shape=[jax.ShapeDtypeStruct((m, d), F32), jax.ShapeDtypeStruct((m, d), F32),
                   jax.ShapeDtypeStruct((m, LANES), jnp.int32), jax.ShapeDtypeStruct((m, LANES), F32)],
        compiler_params=_params(("parallel",), 56),
        name="outproj_router",
    )(mixed, w_out, x, gn, w_r, b_r)


def _expert_kernel(te_ref, tv_ref, src_ref, dst_ref, hn_hbm, wg_ref, wu_ref, wd_ref, y_hbm,
                   xbuf, ybuf, wgb, wub, wdb, gsem, ssem, *, tm, n_tok):
    t = pl.program_id(0)
    slot = lax.rem(t, 2)
    valid = tv_ref[t] > 0
    prev_valid = jnp.logical_and(t >= 1, tv_ref[jnp.maximum(t - 1, 0)] > 0)

    def gather_start(tile, s):
        base = tile * tm

        def body(r, carry):
            tok = src_ref[base + r]
            pltpu.make_async_copy(hn_hbm.at[pl.ds(tok, 1), :], xbuf.at[s, pl.ds(r, 1), :], gsem.at[s]).start()
            return carry

        lax.fori_loop(0, tm, body, 0, unroll=8)

    def gather_wait(s):
        pltpu.make_async_copy(hn_hbm.at[pl.ds(0, tm), :], xbuf.at[s], gsem.at[s]).wait()

    def scatter_start(tile, s):
        base = tile * tm

        def body(r, carry):
            dst = dst_ref[base + r]
            pltpu.make_async_copy(ybuf.at[s, pl.ds(r, 1), :], y_hbm.at[pl.ds(dst, 1), :], ssem.at[s]).start()
            return carry

        lax.fori_loop(0, tm, body, 0, unroll=8)

    def scatter_wait(s):
        pltpu.make_async_copy(ybuf.at[s], y_hbm.at[pl.ds(0, tm), :], ssem.at[s]).wait()

    @pl.when(t == 0)
    def _():
        gather_start(0, 0)
        ybuf[1] = jnp.zeros((tm, ybuf.shape[2]), F32)
        spare = pltpu.make_async_copy(ybuf.at[1], y_hbm.at[pl.ds(2 * n_tok, tm), :], ssem.at[1])
        spare.start()
        spare.wait()

    @pl.when(valid)
    def _():
        gather_wait(slot)
        gather_start(t + 1, 1 - slot)

        @pl.when(jnp.logical_or(t == 0, te_ref[t] != te_ref[jnp.maximum(t - 1, 0)]))
        def _():
            wgb[...] = wg_ref[0].astype(BF16)
            wub[...] = wu_ref[0].astype(BF16)
            wdb[...] = wd_ref[0].astype(BF16)

        xb = xbuf[slot].astype(BF16)
        g = jnp.dot(xb, wgb[...], preferred_element_type=F32)
        u = jnp.dot(xb, wub[...], preferred_element_type=F32)
        hid = (g * _sigmoid(g) * u).astype(BF16)
        ybuf[slot] = jnp.dot(hid, wdb[...], preferred_element_type=F32)
        scatter_start(t, slot)

    @pl.when(jnp.logical_and(jnp.logical_not(valid), prev_valid))
    def _():
        gather_wait(slot)

    @pl.when(prev_valid)
    def _():
        scatter_wait(1 - slot)


def _experts(hn, tile_expert, tile_valid, src, dst, w_g, w_u, w_d, tm):
    n_tok, d = hn.shape
    n_steps = tile_expert.shape[0]
    wspec = lambda a: pl.BlockSpec((1,) + a.shape[1:], lambda t, te, tv, sr, ds: (te[t], 0, 0))
    return pl.pallas_call(
        functools.partial(_expert_kernel, tm=tm, n_tok=n_tok),
        grid_spec=pltpu.PrefetchScalarGridSpec(
            num_scalar_prefetch=4,
            grid=(n_steps,),
            in_specs=[pl.BlockSpec(memory_space=pl.ANY), wspec(w_g), wspec(w_u), wspec(w_d)],
            out_specs=pl.BlockSpec(memory_space=pl.ANY),
            scratch_shapes=[pltpu.VMEM((2, tm, d), F32), pltpu.VMEM((2, tm, d), F32),
                            pltpu.VMEM(w_g.shape[1:], BF16), pltpu.VMEM(w_u.shape[1:], BF16),
                            pltpu.VMEM(w_d.shape[1:], BF16),
                            pltpu.SemaphoreType.DMA((2,)), pltpu.SemaphoreType.DMA((2,))]),
        out_shape=jax.ShapeDtypeStruct((2 * n_tok + tm, d), F32),
        compiler_params=_params(("arbitrary",), 56),
        name="experts",
    )(tile_expert, tile_valid, src, dst, hn, w_g, w_u, w_d)


def _dispatch_plan(route_e, tm):
    n_tok = route_e.shape[0]
    n_pairs = 2 * n_tok
    n_tiles = n_pairs // tm + N_EXPERTS + 1
    eflat = route_e.T.reshape(-1)
    order = jnp.argsort(eflat, stable=True).astype(jnp.int32)
    counts = jnp.sum(eflat[:, None] == jnp.arange(N_EXPERTS, dtype=jnp.int32)[None, :], axis=0, dtype=jnp.int32)
    tiles_e = (counts + tm - 1) // tm
    tile_end = jnp.cumsum(tiles_e)
    tile_off = tile_end - tiles_e
    start = jnp.cumsum(counts) - counts
    tid = jnp.arange(n_tiles, dtype=jnp.int32)
    te = jnp.minimum(jnp.sum(tile_end[None, :] <= tid[:, None], axis=1), N_EXPERTS - 1).astype(jnp.int32)
    tv = jnp.clip(counts[te] - (tid - tile_off[te]) * tm, 0, tm)
    used = tid < tile_end[-1]
    last_e = te[jnp.maximum(tile_end[-1] - 1, 0)]
    te = jnp.where(used, te, last_e).astype(jnp.int32)
    tv = jnp.where(used, tv, 0).astype(jnp.int32)
    r = jnp.arange(tm, dtype=jnp.int32)[None, :]
    src = (start[te] + (tid - tile_off[te]) * tm)[:, None] + r
    dst = jnp.where(r < tv[:, None], order[jnp.clip(src, 0, n_pairs - 1)], n_pairs + r).reshape(-1).astype(jnp.int32)
    return te, tv, dst % n_tok, dst


def _final_kernel(x1_ref, y0_ref, y1_ref, rw_ref, p_ref, wgate_ref, wproj_ref, gple_ref, gfin_ref, o_ref,
                  *, last_layer):
    rw = rw_ref[...]
    x2 = x1_ref[...] + rw[:, 0:1] * y0_ref[...] + rw[:, 1:2] * y1_ref[...]
    hn = _rms(x2, gple_ref[...]).astype(BF16)
    gate = _sigmoid(jnp.dot(hn, wgate_ref[...], preferred_element_type=F32))
    emb = jnp.dot(p_ref[...].astype(BF16), wproj_ref[...], preferred_element_type=F32)
    x3 = x2 + gate * emb
    o_ref[...] = _rms(x3, gfin_ref[...]) if last_layer else x3


def _final(x1, y, rw, p, w_gate, w_proj, g_ple, g_fin, last_layer, tm=256):
    m, d = x1.shape
    nb = m // tm
    row = lambda w: pl.BlockSpec((tm, w), lambda i: (i, 0))
    const = lambda a: pl.BlockSpec(a.shape, lambda i: (0, 0))
    g_ple, g_fin = g_ple.reshape(1, d), g_fin.reshape(1, d)
    return pl.pallas_call(
        functools.partial(_final_kernel, last_layer=last_layer),
        grid=(nb,),
        in_specs=[row(d), row(d), pl.BlockSpec((tm, d), lambda i: (nb + i, 0)), row(LANES), row(p.shape[1]),
                  const(w_gate), const(w_proj), const(g_ple), const(g_fin)],
        out_specs=row(d),
        out_shape=jax.ShapeDtypeStruct((m, d), F32),
        compiler_params=_params(("parallel",), 56),
        name="final",
    )(x1, y, y, rw, p, w_gate, w_proj, g_ple, g_fin)


def kernel(x, p, attn_norm, w_in, b_in, hg_lb_logits, hg_out_norm, w_hg_up, w_attn_up, w_out, ffn_norm,
           w_router_group, b_router_group, w_router_expert, b_router_expert, w_exp_gate, w_exp_up,
           w_exp_down, ple_norm, w_ple_gate, w_ple_proj, final_norm):
    batch, seq, d = x.shape
    n = batch * seq
    depth = w_in.shape[0]
    o_hf, o_at = 0, 4 * HG_WIDTH
    o_iq = o_at + 3 * AT_WIDTH
    o_ik = o_iq + IDX_HEADS * IDX_DIM
    o_iw = o_ik + IDX_DIM
    o_g = o_iw + IDX_HEADS
    lbs = jnp.cumsum(jax.nn.softmax(hg_lb_logits.astype(F32), axis=0), axis=0)
    xf = x.reshape(n, d)
    moe_tm = min(256, n // 2)
    for li in range(depth):
        w, b = w_in[li], b_in[li]
        h = _rmsnorm(xf, attn_norm[li], BF16)
        zhg = _matmul(h, w[:, o_hf:o_at].astype(BF16), b[o_hf:o_at], F32, name="proj_hgrn")
        qkv_scale = jnp.concatenate([jnp.full((AT_WIDTH,), AT_HEAD_DIM ** -0.5, F32), jnp.ones((2 * AT_WIDTH,), F32)])
        zat = _matmul(h, (w[:, o_at:o_iq] * qkv_scale).astype(BF16), b[o_at:o_iq] * qkv_scale, BF16,
                      name="proj_attn")
        dup = lambda a: jnp.repeat(a.reshape(a.shape[:-1] + (IDX_HEADS, 1, IDX_DIM)), 2, axis=-2).reshape(
            a.shape[:-1] + (2 * IDX_HEADS * IDX_DIM,))
        ziq = _matmul(h, dup(w[:, o_iq:o_ik]).astype(BF16), dup(b[o_iq:o_ik]), F32, name="proj_iq")
        pad = LANES - IDX_DIM - IDX_HEADS
        zikw = _matmul(h, jnp.pad(w[:, o_ik:o_g], ((0, 0), (0, pad))).astype(BF16),
                       jnp.pad(b[o_ik:o_g], (0, pad)), F32, name="proj_ikw")
        gates = _matmul(h, w[:, o_g:].astype(BF16), b[o_g:], BF16, act="sigmoid", name="proj_gates")

        o_hg = _hgrn(zhg, lbs[li], hg_out_norm[li], batch, seq)
        ikt = zikw[:, :IDX_DIM].reshape(batch, seq, IDX_DIM).transpose(0, 2, 1)
        o_att = _dsa(ziq, ikt, zikw, zat, batch, seq)
        mixed = _merge(o_hg, w_hg_up[li].astype(BF16), o_att, w_attn_up[li].astype(BF16), gates)

        w_r = jnp.pad(jnp.concatenate([w_router_group[li], w_router_expert[li]], axis=1),
                      ((0, 0), (0, LANES - N_GROUPS - N_EXPERTS)))
        b_r = jnp.pad(jnp.concatenate([b_router_group[li], b_router_expert[li]]), (0, LANES - N_GROUPS - N_EXPERTS))
        x1, hn, route_e, route_w = _outproj(mixed, w_out[li].astype(BF16), xf, ffn_norm[li], w_r, b_r)
        tile_e, tile_v, src, dst = _dispatch_plan(route_e[:, :2], moe_tm)
        y = _experts(hn, tile_e, tile_v, src, dst, w_exp_gate[li], w_exp_up[li], w_exp_down[li], moe_tm)
        xf = _final(x1, y, route_w, p[li].reshape(n, -1), w_ple_gate[li].astype(BF16),
                    w_ple_proj[li].astype(BF16), ple_norm[li], final_norm, li == depth - 1)
    return xf.reshape(batch, seq, d)
```

```python
import functools

import numpy as np
import jax
import jax.numpy as jnp
from jax import lax
from jax.experimental import pallas as pl
from jax.experimental.pallas import tpu as pltpu

F32 = jnp.float32
BF16 = jnp.bfloat16
HIGHEST = lax.Precision.HIGHEST

HG_HEADS = 8
HG_DK = 128
HG_DV = 128
HG_WIDTH = HG_HEADS * HG_DK
HG_CHUNK = 64
AT_HEADS = 8
AT_HEAD_DIM = 128
AT_WIDTH = AT_HEADS * AT_HEAD_DIM
IDX_HEADS = 16
IDX_DIM = 64
TOPK_MAX = 256
N_GROUPS = 4
EXPERTS_PER_GROUP = 8
N_EXPERTS = N_GROUPS * EXPERTS_PER_GROUP
D_EXPERT = 512
RMS_EPS = 1e-6
LANES = 128
INT_MIN = -(2 ** 31)
NEG_BIG = -1e30

_NT = (((1,), (1,)), ((), ()))
_TN = (((0,), (0,)), ((), ()))


def _params(sem, vmem_mb=None):
    kw = dict(dimension_semantics=sem)
    if vmem_mb is not None:
        kw["vmem_limit_bytes"] = vmem_mb * 1024 * 1024
    return pltpu.CompilerParams(**kw)


def _sigmoid(x):
    return 1.0 / (1.0 + jnp.exp(-x))


def _rms(x, g):
    ms = jnp.mean(x * x, axis=-1, keepdims=True)
    return x * lax.rsqrt(ms + RMS_EPS) * g


def _rmsnorm_kernel(x_ref, g_ref, o_ref):
    o_ref[...] = _rms(x_ref[...], g_ref[...]).astype(o_ref.dtype)


def _rmsnorm(x, g, out_dtype, tm=512):
    m, d = x.shape
    return pl.pallas_call(
        _rmsnorm_kernel,
        grid=(m // tm,),
        in_specs=[pl.BlockSpec((tm, d), lambda i: (i, 0)), pl.BlockSpec((1, d), lambda i: (0, 0))],
        out_specs=pl.BlockSpec((tm, d), lambda i: (i, 0)),
        out_shape=jax.ShapeDtypeStruct((m, d), out_dtype),
        compiler_params=_params(("parallel",)),
        name="rmsnorm",
    )(x, g.reshape(1, d))


def _mm_kernel(a_ref, w_ref, b_ref, o_ref, *, act):
    acc = jnp.dot(a_ref[...], w_ref[...], preferred_element_type=F32) + b_ref[...]
    if act == "sigmoid":
        acc = _sigmoid(acc)
    o_ref[...] = acc.astype(o_ref.dtype)


def _matmul(a, w, b, out_dtype, act=None, tm=1024, tn=512, name="proj"):
    m, k = a.shape
    n = w.shape[1]
    tm, tn = min(tm, m), min(tn, n)
    return pl.pallas_call(
        functools.partial(_mm_kernel, act=act),
        grid=(m // tm, n // tn),
        in_specs=[pl.BlockSpec((tm, k), lambda i, j: (i, 0)),
                  pl.BlockSpec((k, tn), lambda i, j: (0, j)),
                  pl.BlockSpec((1, tn), lambda i, j: (0, j))],
        out_specs=pl.BlockSpec((tm, tn), lambda i, j: (i, j)),
        out_shape=jax.ShapeDtypeStruct((m, n), out_dtype),
        compiler_params=_params(("parallel", "arbitrary"), 48),
        name=name,
    )(a, w, b.reshape(1, n))


HG_GROUP = 8
HG_ROWS = 512


def _hgrn_tables():
    c = HG_CHUNK
    levels = [c >> i for i in range(int(np.log2(c)))]
    t = np.arange(c)
    u = np.arange(c)
    rall = [(u[None, :] <= t[:, None]), (u[None, :] > t[:, None])]
    low, mask = [], []
    for p in levels:
        half = p // 2
        m = (t // p) * p + half
        lower = (t % p) >= half
        r = np.where(lower[:, None], (u[None, :] > m[:, None]) & (u[None, :] <= t[:, None]),
                     (u[None, :] > t[:, None]) & (u[None, :] <= m[:, None]))
        rall.append(r)
        low.append(np.broadcast_to(lower[:, None], (c, LANES)))
        mask.append(((t[:, None] // p) == (t[None, :] // p)) & lower[:, None] & ~lower[None, :])
    mask.append(np.eye(c, dtype=bool))
    rall = np.concatenate(rall, 0).astype(np.float32)
    return (np.tile(rall, (1, 3)), np.stack(low).astype(np.float32), np.stack(mask).astype(np.float32),
            len(levels))


def _hgrn_kernel(q_ref, f_ref, i_ref, g_ref, lb_ref, gn_ref, rall_ref, low_ref, mask_ref, o_ref, st_ref,
                 *, n_chunks, n_levels, heads):
    c = HG_CHUNK

    @pl.when(pl.program_id(2) == 0)
    def _():
        st_ref[...] = jnp.zeros_like(st_ref)

    lb = lb_ref[...]
    one_m_lb = 1.0 - lb
    gn = gn_ref[...]

    def body(ci, carry):
        rows = pl.ds(pl.multiple_of(ci * c, c), c)
        z = f_ref[rows, :]
        q = q_ref[rows, :]
        e = jnp.exp(-jnp.abs(z))
        r = 1.0 / (1.0 + e)
        pos = z >= 0
        sig = jnp.where(pos, r, e * r)
        nsig = jnp.where(pos, e * r, r)
        logf = jnp.log(lb + one_m_lb * sig)
        k = one_m_lb * nsig
        qs = q * _sigmoid(q)
        l1 = logf.astype(BF16)
        r1 = logf - l1.astype(F32)
        l2 = r1.astype(BF16)
        l3 = (r1 - l2.astype(F32)).astype(BF16)
        ex = jnp.exp(jnp.dot(rall_ref[...], jnp.concatenate([l1, l2, l3], axis=0), preferred_element_type=F32))
        vb = i_ref[rows, :].astype(BF16)
        gate = g_ref[rows, :]
        gate = gate * _sigmoid(gate)
        for g in range(heads):
            sl = slice(g * HG_DK, (g + 1) * HG_DK)
            qg, kg, vg = qs[:, sl], k[:, sl], vb[:, sl]
            eb = ex[0:c, sl]
            a = mask_ref[n_levels] * lax.dot_general(qg.astype(BF16), kg.astype(BF16), _NT,
                                                     preferred_element_type=F32)
            for li in range(n_levels):
                w = (jnp.where(low_ref[li] != 0.0, qg, kg) * ex[(2 + li) * c:(3 + li) * c, sl]).astype(BF16)
                a = a + mask_ref[li] * lax.dot_general(w, w, _NT, preferred_element_type=F32)
            st = st_ref[g]
            o = (jnp.dot(a.astype(BF16), vg, preferred_element_type=F32)
                 + lax.dot_general((qg * eb).astype(BF16), st.astype(BF16), _NT, preferred_element_type=F32))
            st_ref[g] = st * eb[c - 1:c, :] + lax.dot_general(vg, (kg * ex[c:2 * c, sl]).astype(BF16), _TN,
                                                              preferred_element_type=F32)
            o_ref[rows, sl] = (_rms(o, gn) * gate[:, sl]).astype(o_ref.dtype)
        return carry

    lax.fori_loop(0, n_chunks, body, 0, unroll=2)


def _hgrn(zhg, lb, gn, batch, seq):
    rall, low, mask, n_levels = _hgrn_tables()
    hb = HG_HEADS // HG_GROUP
    gw = HG_GROUP * HG_DK
    tc = min(HG_ROWS, seq)
    nt = seq // tc
    col = lambda off: pl.BlockSpec((tc, gw), lambda b, hh, tt: (b * nt + tt, off + hh))
    full = lambda a: pl.BlockSpec(a.shape, lambda b, hh, tt: (0,) * a.ndim)
    return pl.pallas_call(
        functools.partial(_hgrn_kernel, n_chunks=tc // HG_CHUNK, n_levels=n_levels, heads=HG_GROUP),
        grid=(batch, hb, nt),
        in_specs=[col(0), col(hb), col(2 * hb), col(3 * hb),
                  pl.BlockSpec((1, gw), lambda b, hh, tt: (0, hh)),
                  pl.BlockSpec((1, HG_DV), lambda b, hh, tt: (0, 0)),
                  full(rall), full(low), full(mask)],
        out_specs=pl.BlockSpec((tc, gw), lambda b, hh, tt: (b * nt + tt, hh)),
        out_shape=jax.ShapeDtypeStruct((batch * seq, HG_HEADS * HG_DV), BF16),
        scratch_shapes=[pltpu.VMEM((HG_GROUP, HG_DV, HG_DK), F32)],
        compiler_params=_params(("parallel", "parallel", "arbitrary"), 48),
        name="hgrn2",
    )(zhg, zhg, zhg, zhg, lb.reshape(1, HG_WIDTH), gn.reshape(1, HG_DV),
      jnp.asarray(rall, BF16), jnp.asarray(low), jnp.asarray(mask))


DSA_TQ = 256
DSA_ROW_GROUPS = 4


def _alibi_tables(seq):
    assert 8 % AT_HEADS == 0 and seq <= 64 * 256
    pos = jnp.arange(seq, dtype=jnp.int32)
    hi, lo = (pos // 64).astype(F32), (pos % 64).astype(F32)
    one = jnp.ones((seq,), F32)
    kext = jnp.stack([one, one, hi, lo], axis=1)
    kext = jnp.pad(kext, ((0, 0), (0, AT_HEAD_DIM - 4))).astype(BF16)
    qext = []
    for h in range(AT_HEADS):
        slope = 2.0 ** (-8.0 * (h + 1) / AT_HEADS)
        cols = jnp.stack([-slope * 64.0 * hi, -slope * lo, slope * 64.0 * one, slope * one], axis=1)
        qext.append(jnp.pad(cols, ((0, 0), (0, AT_HEAD_DIM - 4))))
    return jnp.concatenate(qext, axis=1).astype(BF16), kext


def _dsa_kernel(iq_ref, ikt_ref, iw_ref, q_ref, qext_ref, k_ref, kext_ref, v_ref, o_ref, acc_ref, key_ref, bias_ref,
                *, tq, seq, ksel, col_step):
    first_row = pl.program_id(1) * tq
    for g in range(seq // col_step):
        @pl.when(first_row // col_step == g)
        def _(width=(g + 1) * col_step):
            _dsa_body(iq_ref, ikt_ref, iw_ref, q_ref, qext_ref, k_ref, kext_ref, v_ref, o_ref, acc_ref, key_ref,
                      bias_ref, tq=tq, ksel=ksel, width=width, first_row=first_row)


def _dsa_body(iq_ref, ikt_ref, iw_ref, q_ref, qext_ref, k_ref, kext_ref, v_ref, o_ref, acc_ref, key_ref, bias_ref,
              *, tq, ksel, width, first_row):
    seq = width
    t0 = first_row
    acc_ref, key_ref, bias_ref = acc_ref.at[:, :width], key_ref.at[:, :width], bias_ref.at[:, :width]
    ikt = ikt_ref[0, :, :width]
    ik_hi = ikt.astype(BF16)
    ik_lo = (ikt - ik_hi.astype(F32)).astype(BF16)
    rhs = jnp.concatenate([ik_hi, ik_hi, ik_lo, ik_lo], axis=0)
    lane = lax.broadcasted_iota(jnp.int32, (tq, LANES), 1)
    w_all = iw_ref[...] * (IDX_HEADS ** -0.5 * IDX_DIM ** -0.5)
    for h in range(IDX_HEADS):
        x = iq_ref[:, h * LANES:(h + 1) * LANES]
        x_hi = x.astype(BF16).astype(F32)
        u = jnp.where(lane < IDX_DIM, x_hi, x - x_hi).astype(BF16)
        s = jnp.dot(jnp.concatenate([u, u], axis=1), rhs, preferred_element_type=F32)
        term = jnp.maximum(s, 0.0) * w_all[:, IDX_DIM + h:IDX_DIM + h + 1]
        if h == 0:
            acc_ref[...] = term
        else:
            acc_ref[...] += term

    col = lax.broadcasted_iota(jnp.int32, (tq, seq), 1)
    row = t0 + lax.broadcasted_iota(jnp.int32, (tq, seq), 0)
    vis = col <= row
    bits = pltpu.bitcast(acc_ref[...], jnp.int32)
    key = bits ^ ((bits >> 31) & 0x7FFFFFFF)
    key_ref[...] = jnp.where(vis, key, INT_MIN)

    rg = tq // DSA_ROW_GROUPS

    def bisect(it, cs):
        bit = jnp.left_shift(jnp.int32(1), 31 - it)
        out = []
        for gi, c in enumerate(cs):
            cand = c | bit
            keys = key_ref[gi * rg:(gi + 1) * rg, :]
            cnt = jnp.sum((keys >= (cand ^ INT_MIN)).astype(jnp.int32), axis=1, keepdims=True)
            out.append(jnp.where(cnt >= ksel, cand, c))
        return tuple(out)

    cs = lax.fori_loop(0, 32, bisect, tuple(jnp.zeros((rg, 1), jnp.int32) for _ in range(DSA_ROW_GROUPS)))
    thr = jnp.concatenate(cs, axis=0) ^ INT_MIN
    bias_ref[...] = jnp.where(vis, jnp.where(key_ref[...] >= thr, 0.0, NEG_BIG), NEG_BIG)

    kext = kext_ref[:width, :]
    for h in range(AT_HEADS):
        hs = slice(h * AT_HEAD_DIM, (h + 1) * AT_HEAD_DIM)
        qa = jnp.concatenate([q_ref[:, hs], qext_ref[:, hs]], axis=1)
        ka = jnp.concatenate([k_ref[:width, hs], kext], axis=1)
        logit = lax.dot_general(qa, ka, _NT, preferred_element_type=F32) + bias_ref[...]
        m = jnp.max(logit, axis=1, keepdims=True)
        p = jnp.exp(logit - m)
        den = jnp.sum(p, axis=1, keepdims=True)
        oh = jnp.dot(p.astype(BF16), v_ref[:width, hs], preferred_element_type=F32) / den
        o_ref[:, hs] = oh.astype(o_ref.dtype)


def _dsa(iq_dup, ikt, ikw, zat, batch, seq, tq=DSA_TQ):
    nq = seq // tq
    ksel = min(TOPK_MAX, seq // 4)
    col_step = max(seq // 4, tq)
    qext, kext = _alibi_tables(seq)
    return pl.pallas_call(
        functools.partial(_dsa_kernel, tq=tq, seq=seq, ksel=ksel, col_step=col_step),
        grid=(batch, nq),
        in_specs=[pl.BlockSpec((tq, IDX_HEADS * LANES), lambda b, i: (b * nq + i, 0)),
                  pl.BlockSpec((1, IDX_DIM, seq), lambda b, i: (b, 0, 0)),
                  pl.BlockSpec((tq, LANES), lambda b, i: (b * nq + i, 0)),
                  pl.BlockSpec((tq, AT_WIDTH), lambda b, i: (b * nq + i, 0)),
                  pl.BlockSpec((tq, AT_WIDTH), lambda b, i: (i, 0)),
                  pl.BlockSpec((seq, AT_WIDTH), lambda b, i: (b, 1)),
                  pl.BlockSpec((seq, AT_HEAD_DIM), lambda b, i: (0, 0)),
                  pl.BlockSpec((seq, AT_WIDTH), lambda b, i: (b, 2))],
        out_specs=pl.BlockSpec((tq, AT_WIDTH), lambda b, i: (b * nq + i, 0)),
        out_shape=jax.ShapeDtypeStruct((batch * seq, AT_WIDTH), BF16),
        scratch_shapes=[pltpu.VMEM((tq, seq), F32), pltpu.VMEM((tq, seq), jnp.int32),
                        pltpu.VMEM((tq, seq), F32)],
        compiler_params=_params(("parallel", "arbitrary"), 48),
        name="dsa",
    )(iq_dup, ikt, ikw, zat, qext, zat, kext, zat)


def _merge_kernel(a1_ref, w1_ref, a2_ref, w2_ref, g1_ref, g2_ref, o_ref):
    y1 = jnp.dot(a1_ref[...], w1_ref[...], preferred_element_type=F32)
    y2 = jnp.dot(a2_ref[...], w2_ref[...], preferred_element_type=F32)
    o_ref[...] = (g1_ref[...].astype(F32) * y1 + g2_ref[...].astype(F32) * y2).astype(o_ref.dtype)


def _merge(o_hg, w_hg, o_at, w_at, gates, tm=1024, tn=512):
    m, k1 = o_hg.shape
    k2 = o_at.shape[1]
    n = w_hg.shape[1]
    tm = min(tm, m)
    nb = n // tn
    return pl.pallas_call(
        _merge_kernel,
        grid=(m // tm, nb),
        in_specs=[pl.BlockSpec((tm, k1), lambda i, j: (i, 0)), pl.BlockSpec((k1, tn), lambda i, j: (0, j)),
                  pl.BlockSpec((tm, k2), lambda i, j: (i, 0)), pl.BlockSpec((k2, tn), lambda i, j: (0, j)),
                  pl.BlockSpec((tm, tn), lambda i, j: (i, j)), pl.BlockSpec((tm, tn), lambda i, j: (i, nb + j))],
        out_specs=pl.BlockSpec((tm, tn), lambda i, j: (i, j)),
        out_shape=jax.ShapeDtypeStruct((m, n), BF16),
        compiler_params=_params(("parallel", "arbitrary"), 48),
        name="merge",
    )(o_hg, w_hg, o_at, w_at, gates, gates)


def _outproj_kernel(m_ref, w_ref, x_ref, gn_ref, wr_ref, br_ref, x1_ref, hn_ref, ri_ref, rw_ref):
    x1 = x_ref[...] + jnp.dot(m_ref[...], w_ref[...], preferred_element_type=F32)
    x1_ref[...] = x1
    hn = _rms(x1, gn_ref[...])
    hn_ref[...] = hn
    h_hi = hn.astype(BF16)
    h_lo = (hn - h_hi.astype(F32)).astype(BF16)
    logit = jnp.dot(jnp.concatenate([h_hi, h_hi, h_lo], axis=1), wr_ref[...],
                    preferred_element_type=F32) + br_ref[...]
    lane = lax.broadcasted_iota(jnp.int32, logit.shape, 1)
    lane_f = lane.astype(F32)
    is_g = lane < N_GROUPS
    lg = jnp.where(is_g, logit, -jnp.inf)
    mg = jnp.max(lg, axis=1, keepdims=True)
    pg_top = 1.0 / jnp.sum(jnp.exp(lg - mg), axis=1, keepdims=True)
    gid = jnp.min(jnp.where(lg == mg, lane_f, float(LANES)), axis=1, keepdims=True).astype(jnp.int32)
    in_grp = ((lane - N_GROUPS) >> 3) == gid
    le = jnp.where(in_grp, logit, -jnp.inf)
    m1 = jnp.max(le, axis=1, keepdims=True)
    i1 = jnp.min(jnp.where(le == m1, lane_f, float(LANES)), axis=1, keepdims=True)
    le2 = jnp.where(lane_f == i1, -jnp.inf, le)
    m2 = jnp.max(le2, axis=1, keepdims=True)
    i2 = jnp.min(jnp.where(le2 == m2, lane_f, float(LANES)), axis=1, keepdims=True)
    r = jnp.exp(m2 - m1)
    w1 = pg_top / (1.0 + r)
    w2 = pg_top * r / (1.0 + r)
    e1 = i1.astype(jnp.int32) - N_GROUPS
    e2 = i2.astype(jnp.int32) - N_GROUPS
    ri_ref[...] = jnp.where(lane == 0, e1, jnp.where(lane == 1, e2, 0))
    rw_ref[...] = jnp.where(lane == 0, w1, jnp.where(lane == 1, w2, 0.0))


def _outproj(mixed, w_out, x, gn, w_r, b_r, tm=256):
    m, d = x.shape
    row = lambda w: pl.BlockSpec((tm, w), lambda i: (i, 0))
    const = lambda a: pl.BlockSpec(a.shape, lambda i: (0, 0))
    gn = gn.reshape(1, d)
    b_r = b_r.reshape(1, LANES)
    return pl.pallas_call(
        _outproj_kernel,
        grid=(m // tm,),
        in_specs=[row(d), const(w_out), row(d), const(gn), const(w_r), const(b_r)],
        out_specs=[row(d), row(d), row(LANES), row(LANES)],
        out_shape=[jax.ShapeDtypeStruct((m, d), F32), jax.ShapeDtypeStruct((m, d), F32),
                   jax.ShapeDtypeStruct((m, LANES), jnp.int32), jax.ShapeDtypeStruct((m, LANES), F32)],
        compiler_params=_params(("parallel",), 56),
        name="outproj_router",
    )(mixed, w_out, x, gn, w_r, b_r)


def _expert_kernel(te_ref, tv_ref, src_ref, dst_ref, hn_hbm, wg_ref, wu_ref, wd_ref, y_hbm,
                   xbuf, ybuf, wgb, wub, wdb, gsem, ssem, *, tm, n_tok):
    t = pl.program_id(0)
    slot = lax.rem(t, 2)
    valid = tv_ref[t] > 0
    prev_valid = jnp.logical_and(t >= 1, tv_ref[jnp.maximum(t - 1, 0)] > 0)

    def gather_start(tile, s):
        base = tile * tm

        def body(r, carry):
            tok = src_ref[base + r]
            pltpu.make_async_copy(hn_hbm.at[pl.ds(tok, 1), :], xbuf.at[s, pl.ds(r, 1), :], gsem.at[s]).start()
            return carry

        lax.fori_loop(0, tm, body, 0, unroll=8)

    def gather_wait(s):
        pltpu.make_async_copy(hn_hbm.at[pl.ds(0, tm), :], xbuf.at[s], gsem.at[s]).wait()

    def scatter_start(tile, s):
        base = tile * tm

        def body(r, carry):
            dst = dst_ref[base + r]
            pltpu.make_async_copy(ybuf.at[s, pl.ds(r, 1), :], y_hbm.at[pl.ds(dst, 1), :], ssem.at[s]).start()
            return carry

        lax.fori_loop(0, tm, body, 0, unroll=8)

    def scatter_wait(s):
        pltpu.make_async_copy(ybuf.at[s], y_hbm.at[pl.ds(0, tm), :], ssem.at[s]).wait()

    @pl.when(t == 0)
    def _():
        gather_start(0, 0)
        ybuf[1] = jnp.zeros((tm, ybuf.shape[2]), F32)
        spare = pltpu.make_async_copy(ybuf.at[1], y_hbm.at[pl.ds(2 * n_tok, tm), :], ssem.at[1])
        spare.start()
        spare.wait()

    @pl.when(valid)
    def _():
        gather_wait(slot)
        gather_start(t + 1, 1 - slot)

        @pl.when(jnp.logical_or(t == 0, te_ref[t] != te_ref[jnp.maximum(t - 1, 0)]))
        def _():
            wgb[...] = wg_ref[0].astype(BF16)
            wub[...] = wu_ref[0].astype(BF16)
            wdb[...] = wd_ref[0].astype(BF16)

        xb = xbuf[slot].astype(BF16)
        g = jnp.dot(xb, wgb[...], preferred_element_type=F32)
        u = jnp.dot(xb, wub[...], preferred_element_type=F32)
        hid = (g * _sigmoid(g) * u).astype(BF16)
        ybuf[slot] = jnp.dot(hid, wdb[...], preferred_element_type=F32)
        scatter_start(t, slot)

    @pl.when(jnp.logical_and(jnp.logical_not(valid), prev_valid))
    def _():
        gather_wait(slot)

    @pl.when(prev_valid)
    def _():
        scatter_wait(1 - slot)


def _experts(hn, tile_expert, tile_valid, src, dst, w_g, w_u, w_d, tm):
    n_tok, d = hn.shape
    n_steps = tile_expert.shape[0]
    wspec = lambda a: pl.BlockSpec((1,) + a.shape[1:], lambda t, te, tv, sr, ds: (te[t], 0, 0))
    return pl.pallas_call(
        functools.partial(_expert_kernel, tm=tm, n_tok=n_tok),
        grid_spec=pltpu.PrefetchScalarGridSpec(
            num_scalar_prefetch=4,
            grid=(n_steps,),
            in_specs=[pl.BlockSpec(memory_space=pl.ANY), wspec(w_g), wspec(w_u), wspec(w_d)],
            out_specs=pl.BlockSpec(memory_space=pl.ANY),
            scratch_shapes=[pltpu.VMEM((2, tm, d), F32), pltpu.VMEM((2, tm, d), F32),
                            pltpu.VMEM(w_g.shape[1:], BF16), pltpu.VMEM(w_u.shape[1:], BF16),
                            pltpu.VMEM(w_d.shape[1:], BF16),
                            pltpu.SemaphoreType.DMA((2,)), pltpu.SemaphoreType.DMA((2,))]),
        out_shape=jax.ShapeDtypeStruct((2 * n_tok + tm, d), F32),
        compiler_params=_params(("arbitrary",), 56),
        name="experts",
    )(tile_expert, tile_valid, src, dst, hn, w_g, w_u, w_d)


def _dispatch_plan(route_e, tm):
    n_tok = route_e.shape[0]
    n_pairs = 2 * n_tok
    n_tiles = n_pairs // tm + N_EXPERTS + 1
    eflat = route_e.T.reshape(-1)
    order = jnp.argsort(eflat, stable=True).astype(jnp.int32)
    counts = jnp.sum(eflat[:, None] == jnp.arange(N_EXPERTS, dtype=jnp.int32)[None, :], axis=0, dtype=jnp.int32)
    tiles_e = (counts + tm - 1) // tm
    tile_end = jnp.cumsum(tiles_e)
    tile_off = tile_end - tiles_e
    start = jnp.cumsum(counts) - counts
    tid = jnp.arange(n_tiles, dtype=jnp.int32)
    te = jnp.minimum(jnp.sum(tile_end[None, :] <= tid[:, None], axis=1), N_EXPERTS - 1).astype(jnp.int32)
    tv = jnp.clip(counts[te] - (tid - tile_off[te]) * tm, 0, tm)
    used = tid < tile_end[-1]
    last_e = te[jnp.maximum(tile_end[-1] - 1, 0)]
    te = jnp.where(used, te, last_e).astype(jnp.int32)
    tv = jnp.where(used, tv, 0).astype(jnp.int32)
    r = jnp.arange(tm, dtype=jnp.int32)[None, :]
    src = (start[te] + (tid - tile_off[te]) * tm)[:, None] + r
    dst = jnp.where(r < tv[:, None], order[jnp.clip(src, 0, n_pairs - 1)], n_pairs + r).reshape(-1).astype(jnp.int32)
    return te, tv, dst % n_tok, dst


def _final_kernel(x1_ref, y0_ref, y1_ref, rw_ref, p_ref, wgate_ref, wproj_ref, gple_ref, gfin_ref, o_ref,
                  *, last_layer):
    rw = rw_ref[...]
    x2 = x1_ref[...] + rw[:, 0:1] * y0_ref[...] + rw[:, 1:2] * y1_ref[...]
    hn = _rms(x2, gple_ref[...]).astype(BF16)
    gate = _sigmoid(jnp.dot(hn, wgate_ref[...], preferred_element_type=F32))
    emb = jnp.dot(p_ref[...].astype(BF16), wproj_ref[...], preferred_element_type=F32)
    x3 = x2 + gate * emb
    o_ref[...] = _rms(x3, gfin_ref[...]) if last_layer else x3


def _final(x1, y, rw, p, w_gate, w_proj, g_ple, g_fin, last_layer, tm=256):
    m, d = x1.shape
    nb = m // tm
    row = lambda w: pl.BlockSpec((tm, w), lambda i: (i, 0))
    const = lambda a: pl.BlockSpec(a.shape, lambda i: (0, 0))
    g_ple, g_fin = g_ple.reshape(1, d), g_fin.reshape(1, d)
    return pl.pallas_call(
        functools.partial(_final_kernel, last_layer=last_layer),
        grid=(nb,),
        in_specs=[row(d), row(d), pl.BlockSpec((tm, d), lambda i: (nb + i, 0)), row(LANES), row(p.shape[1]),
                  const(w_gate), const(w_proj), const(g_ple), const(g_fin)],
        out_specs=row(d),
        out_shape=jax.ShapeDtypeStruct((m, d), F32),
        compiler_params=_params(("parallel",), 56),
        name="final",
    )(x1, y, y, rw, p, w_gate, w_proj, g_ple, g_fin)


def kernel(x, p, attn_norm, w_in, b_in, hg_lb_logits, hg_out_norm, w_hg_up, w_attn_up, w_out, ffn_norm,
           w_router_group, b_router_group, w_router_expert, b_router_expert, w_exp_gate, w_exp_up,
           w_exp_down, ple_norm, w_ple_gate, w_ple_proj, final_norm):
    batch, seq, d = x.shape
    n = batch * seq
    depth = w_in.shape[0]
    o_hf, o_at = 0, 4 * HG_WIDTH
    o_iq = o_at + 3 * AT_WIDTH
    o_ik = o_iq + IDX_HEADS * IDX_DIM
    o_iw = o_ik + IDX_DIM
    o_g = o_iw + IDX_HEADS
    lbs = jnp.cumsum(jax.nn.softmax(hg_lb_logits.astype(F32), axis=0), axis=0)
    xf = x.reshape(n, d)
    moe_tm = min(256, n // 2)
    for li in range(depth):
        w, b = w_in[li], b_in[li]
        h = _rmsnorm(xf, attn_norm[li], BF16)
        zhg = _matmul(h, w[:, o_hf:o_at].astype(BF16), b[o_hf:o_at], F32, name="proj_hgrn")
        qkv_scale = jnp.concatenate([jnp.full((AT_WIDTH,), AT_HEAD_DIM ** -0.5, F32), jnp.ones((2 * AT_WIDTH,), F32)])
        zat = _matmul(h, (w[:, o_at:o_iq] * qkv_scale).astype(BF16), b[o_at:o_iq] * qkv_scale, BF16,
                      name="proj_attn")
        dup = lambda a: jnp.repeat(a.reshape(a.shape[:-1] + (IDX_HEADS, 1, IDX_DIM)), 2, axis=-2).reshape(
            a.shape[:-1] + (2 * IDX_HEADS * IDX_DIM,))
        ziq = _matmul(h, dup(w[:, o_iq:o_ik]).astype(BF16), dup(b[o_iq:o_ik]), F32, name="proj_iq")
        pad = LANES - IDX_DIM - IDX_HEADS
        zikw = _matmul(h, jnp.pad(w[:, o_ik:o_g], ((0, 0), (0, pad))).astype(BF16),
                       jnp.pad(b[o_ik:o_g], (0, pad)), F32, name="proj_ikw")
        gates = _matmul(h, w[:, o_g:].astype(BF16), b[o_g:], BF16, act="sigmoid", name="proj_gates")

        o_hg = _hgrn(zhg, lbs[li], hg_out_norm[li], batch, seq)
        ikt = zikw[:, :IDX_DIM].reshape(batch, seq, IDX_DIM).transpose(0, 2, 1)
        o_att = _dsa(ziq, ikt, zikw, zat, batch, seq)
        mixed = _merge(o_hg, w_hg_up[li].astype(BF16), o_att, w_attn_up[li].astype(BF16), gates)

        w_r = jnp.pad(jnp.concatenate([w_router_group[li], w_router_expert[li]], axis=1),
                      ((0, 0), (0, LANES - N_GROUPS - N_EXPERTS)))
        b_r = jnp.pad(jnp.concatenate([b_router_group[li], b_router_expert[li]]), (0, LANES - N_GROUPS - N_EXPERTS))
        w_r_hi = w_r.astype(BF16)
        w_r_lo = (w_r - w_r_hi.astype(F32)).astype(BF16)
        w_r = jnp.concatenate([w_r_hi, w_r_lo, w_r_hi], axis=0)
        x1, hn, route_e, route_w = _outproj(mixed, w_out[li].astype(BF16), xf, ffn_norm[li], w_r, b_r)
        tile_e, tile_v, src, dst = _dispatch_plan(route_e[:, :2], moe_tm)
        y = _experts(hn, tile_e, tile_v, src, dst, w_exp_gate[li], w_exp_up[li], w_exp_down[li], moe_tm)
        xf = _final(x1, y, route_w, p[li].reshape(n, -1), w_ple_gate[li].astype(BF16),
                    w_ple_proj[li].astype(BF16), ple_norm[li], final_norm, li == depth - 1)
    return xf.reshape(batch, seq, d)
```

```python
import functools

import numpy as np
import jax
import jax.numpy as jnp
from jax import lax
from jax.experimental import pallas as pl
from jax.experimental.pallas import tpu as pltpu

F32 = jnp.float32
BF16 = jnp.bfloat16
HIGHEST = lax.Precision.HIGHEST

HG_HEADS = 8
HG_DK = 128
HG_DV = 128
HG_WIDTH = HG_HEADS * HG_DK
HG_CHUNK = 64
AT_HEADS = 8
AT_HEAD_DIM = 128
AT_WIDTH = AT_HEADS * AT_HEAD_DIM
IDX_HEADS = 16
IDX_DIM = 64
TOPK_MAX = 256
N_GROUPS = 4
EXPERTS_PER_GROUP = 8
N_EXPERTS = N_GROUPS * EXPERTS_PER_GROUP
D_EXPERT = 512
RMS_EPS = 1e-6
LANES = 128
INT_MIN = -(2 ** 31)
NEG_BIG = -1e30

_NT = (((1,), (1,)), ((), ()))
_TN = (((0,), (0,)), ((), ()))


def _params(sem, vmem_mb=None):
    kw = dict(dimension_semantics=sem)
    if vmem_mb is not None:
        kw["vmem_limit_bytes"] = vmem_mb * 1024 * 1024
    return pltpu.CompilerParams(**kw)


def _sigmoid(x):
    return 1.0 / (1.0 + jnp.exp(-x))


def _rms(x, g):
    ms = jnp.mean(x * x, axis=-1, keepdims=True)
    return x * lax.rsqrt(ms + RMS_EPS) * g


def _rmsnorm_kernel(x_ref, g_ref, o_ref):
    o_ref[...] = _rms(x_ref[...], g_ref[...]).astype(o_ref.dtype)


def _rmsnorm(x, g, out_dtype, tm=512):
    m, d = x.shape
    return pl.pallas_call(
        _rmsnorm_kernel,
        grid=(m // tm,),
        in_specs=[pl.BlockSpec((tm, d), lambda i: (i, 0)), pl.BlockSpec((1, d), lambda i: (0, 0))],
        out_specs=pl.BlockSpec((tm, d), lambda i: (i, 0)),
        out_shape=jax.ShapeDtypeStruct((m, d), out_dtype),
        compiler_params=_params(("parallel",)),
        name="rmsnorm",
    )(x, g.reshape(1, d))


def _mm_kernel(a_ref, w_ref, b_ref, o_ref, *, act):
    acc = jnp.dot(a_ref[...], w_ref[...], preferred_element_type=F32) + b_ref[...]
    if act == "sigmoid":
        acc = _sigmoid(acc)
    o_ref[...] = acc.astype(o_ref.dtype)


def _matmul(a, w, b, out_dtype, act=None, tm=1024, tn=512, name="proj"):
    m, k = a.shape
    n = w.shape[1]
    tm, tn = min(tm, m), min(tn, n)
    return pl.pallas_call(
        functools.partial(_mm_kernel, act=act),
        grid=(m // tm, n // tn),
        in_specs=[pl.BlockSpec((tm, k), lambda i, j: (i, 0)),
                  pl.BlockSpec((k, tn), lambda i, j: (0, j)),
                  pl.BlockSpec((1, tn), lambda i, j: (0, j))],
        out_specs=pl.BlockSpec((tm, tn), lambda i, j: (i, j)),
        out_shape=jax.ShapeDtypeStruct((m, n), out_dtype),
        compiler_params=_params(("parallel", "arbitrary"), 48),
        name=name,
    )(a, w, b.reshape(1, n))


HG_GROUP = 8
HG_ROWS = 512


def _hgrn_tables():
    c = HG_CHUNK
    levels = [c >> i for i in range(int(np.log2(c)))]
    t = np.arange(c)
    u = np.arange(c)
    rall = [(u[None, :] <= t[:, None]), (u[None, :] > t[:, None])]
    low, mask = [], []
    for p in levels:
        half = p // 2
        m = (t // p) * p + half
        lower = (t % p) >= half
        r = np.where(lower[:, None], (u[None, :] > m[:, None]) & (u[None, :] <= t[:, None]),
                     (u[None, :] > t[:, None]) & (u[None, :] <= m[:, None]))
        rall.append(r)
        low.append(np.broadcast_to(lower[:, None], (c, LANES)))
        mask.append(((t[:, None] // p) == (t[None, :] // p)) & lower[:, None] & ~lower[None, :])
    mask.append(np.eye(c, dtype=bool))
    rall = np.concatenate(rall, 0).astype(np.float32)
    return (np.tile(rall, (1, 3)), np.stack(low).astype(np.float32), np.stack(mask).astype(np.float32),
            len(levels))


def _hgrn_kernel(q_ref, f_ref, i_ref, g_ref, lb_ref, gn_ref, rall_ref, low_ref, mask_ref, o_ref, st_ref,
                 *, n_chunks, n_levels, heads):
    c = HG_CHUNK

    @pl.when(pl.program_id(2) == 0)
    def _():
        st_ref[...] = jnp.zeros_like(st_ref)

    lb = lb_ref[...]
    one_m_lb = 1.0 - lb
    gn = gn_ref[...]

    def body(ci, carry):
        rows = pl.ds(pl.multiple_of(ci * c, c), c)
        z = f_ref[rows, :]
        q = q_ref[rows, :]
        e = jnp.exp(-jnp.abs(z))
        r = 1.0 / (1.0 + e)
        pos = z >= 0
        sig = jnp.where(pos, r, e * r)
        nsig = jnp.where(pos, e * r, r)
        logf = jnp.log(lb + one_m_lb * sig)
        k = one_m_lb * nsig
        qs = q * _sigmoid(q)
        l1 = logf.astype(BF16)
        r1 = logf - l1.astype(F32)
        l2 = r1.astype(BF16)
        l3 = (r1 - l2.astype(F32)).astype(BF16)
        ex = jnp.exp(jnp.dot(rall_ref[...], jnp.concatenate([l1, l2, l3], axis=0), preferred_element_type=F32))
        vb = i_ref[rows, :].astype(BF16)
        gate = g_ref[rows, :]
        gate = gate * _sigmoid(gate)
        for g in range(heads):
            sl = slice(g * HG_DK, (g + 1) * HG_DK)
            qg, kg, vg = qs[:, sl], k[:, sl], vb[:, sl]
            eb = ex[0:c, sl]
            a = mask_ref[n_levels] * lax.dot_general(qg.astype(BF16), kg.astype(BF16), _NT,
                                                     preferred_element_type=F32)
            for li in range(n_levels):
                w = (jnp.where(low_ref[li] != 0.0, qg, kg) * ex[(2 + li) * c:(3 + li) * c, sl]).astype(BF16)
                a = a + mask_ref[li] * lax.dot_general(w, w, _NT, preferred_element_type=F32)
            st = st_ref[g]
            o = (jnp.dot(a.astype(BF16), vg, preferred_element_type=F32)
                 + lax.dot_general((qg * eb).astype(BF16), st.astype(BF16), _NT, preferred_element_type=F32))
            st_ref[g] = st * eb[c - 1:c, :] + lax.dot_general(vg, (kg * ex[c:2 * c, sl]).astype(BF16), _TN,
                                                              preferred_element_type=F32)
            o_ref[rows, sl] = (_rms(o, gn) * gate[:, sl]).astype(o_ref.dtype)
        return carry

    lax.fori_loop(0, n_chunks, body, 0, unroll=2)


def _hgrn(zhg, lb, gn, batch, seq):
    rall, low, mask, n_levels = _hgrn_tables()
    hb = HG_HEADS // HG_GROUP
    gw = HG_GROUP * HG_DK
    tc = min(HG_ROWS, seq)
    nt = seq // tc
    col = lambda off: pl.BlockSpec((tc, gw), lambda b, hh, tt: (b * nt + tt, off + hh))
    full = lambda a: pl.BlockSpec(a.shape, lambda b, hh, tt: (0,) * a.ndim)
    return pl.pallas_call(
        functools.partial(_hgrn_kernel, n_chunks=tc // HG_CHUNK, n_levels=n_levels, heads=HG_GROUP),
        grid=(batch, hb, nt),
        in_specs=[col(0), col(hb), col(2 * hb), col(3 * hb),
                  pl.BlockSpec((1, gw), lambda b, hh, tt: (0, hh)),
                  pl.BlockSpec((1, HG_DV), lambda b, hh, tt: (0, 0)),
                  full(rall), full(low), full(mask)],
        out_specs=pl.BlockSpec((tc, gw), lambda b, hh, tt: (b * nt + tt, hh)),
        out_shape=jax.ShapeDtypeStruct((batch * seq, HG_HEADS * HG_DV), BF16),
        scratch_shapes=[pltpu.VMEM((HG_GROUP, HG_DV, HG_DK), F32)],
        compiler_params=_params(("parallel", "parallel", "arbitrary"), 48),
        name="hgrn2",
    )(zhg, zhg, zhg, zhg, lb.reshape(1, HG_WIDTH), gn.reshape(1, HG_DV),
      jnp.asarray(rall, BF16), jnp.asarray(low), jnp.asarray(mask))


DSA_TQ = 256
DSA_ROW_GROUPS = 4


def _alibi_tables(seq):
    assert 8 % AT_HEADS == 0 and seq <= 64 * 256
    pos = jnp.arange(seq, dtype=jnp.int32)
    hi, lo = (pos // 64).astype(F32), (pos % 64).astype(F32)
    one = jnp.ones((seq,), F32)
    kext = jnp.stack([one, one, hi, lo], axis=1)
    kext = jnp.pad(kext, ((0, 0), (0, AT_HEAD_DIM - 4))).astype(BF16)
    qext = []
    for h in range(AT_HEADS):
        slope = 2.0 ** (-8.0 * (h + 1) / AT_HEADS)
        cols = jnp.stack([-slope * 64.0 * hi, -slope * lo, slope * 64.0 * one, slope * one], axis=1)
        qext.append(jnp.pad(cols, ((0, 0), (0, AT_HEAD_DIM - 4))))
    return jnp.concatenate(qext, axis=1).astype(BF16), kext


def _dsa_kernel(iq_ref, ikt_ref, iw_ref, q_ref, qext_ref, k_ref, kext_ref, v_ref, o_ref, acc_ref, key_ref, bias_ref,
                *, tq, seq, ksel, col_step):
    first_row = pl.program_id(1) * tq
    for g in range(seq // col_step):
        @pl.when(first_row // col_step == g)
        def _(width=(g + 1) * col_step):
            _dsa_body(iq_ref, ikt_ref, iw_ref, q_ref, qext_ref, k_ref, kext_ref, v_ref, o_ref, acc_ref, key_ref,
                      bias_ref, tq=tq, ksel=ksel, width=width, first_row=first_row)


def _dsa_body(iq_ref, ikt_ref, iw_ref, q_ref, qext_ref, k_ref, kext_ref, v_ref, o_ref, acc_ref, key_ref, bias_ref,
              *, tq, ksel, width, first_row):
    seq = width
    t0 = first_row
    acc_ref, key_ref, bias_ref = acc_ref.at[:, :width], key_ref.at[:, :width], bias_ref.at[:, :width]
    ikt = ikt_ref[0, :, :width]
    ik_hi = ikt.astype(BF16)
    ik_lo = (ikt - ik_hi.astype(F32)).astype(BF16)
    rhs = jnp.concatenate([ik_hi, ik_hi, ik_lo, ik_lo], axis=0)
    lane = lax.broadcasted_iota(jnp.int32, (tq, LANES), 1)
    w_all = iw_ref[...] * (IDX_HEADS ** -0.5 * IDX_DIM ** -0.5)
    for h in range(IDX_HEADS):
        x = iq_ref[:, h * LANES:(h + 1) * LANES]
        x_hi = x.astype(BF16).astype(F32)
        u = jnp.where(lane < IDX_DIM, x_hi, x - x_hi).astype(BF16)
        s = jnp.dot(jnp.concatenate([u, u], axis=1), rhs, preferred_element_type=F32)
        term = jnp.maximum(s, 0.0) * w_all[:, IDX_DIM + h:IDX_DIM + h + 1]
        if h == 0:
            acc_ref[...] = term
        else:
            acc_ref[...] += term

    col = lax.broadcasted_iota(jnp.int32, (tq, seq), 1)
    row = t0 + lax.broadcasted_iota(jnp.int32, (tq, seq), 0)
    vis = col <= row
    bits = pltpu.bitcast(acc_ref[...], jnp.int32)
    key = bits ^ ((bits >> 31) & 0x7FFFFFFF)
    key_ref[...] = jnp.where(vis, key, INT_MIN)

    rg = tq // DSA_ROW_GROUPS

    def bisect(it, cs):
        bit = jnp.left_shift(jnp.int32(1), 31 - it)
        out = []
        for gi, c in enumerate(cs):
            cand = c | bit
            keys = key_ref[gi * rg:(gi + 1) * rg, :]
            cnt = jnp.sum((keys >= (cand ^ INT_MIN)).astype(jnp.int32), axis=1, keepdims=True)
            out.append(jnp.where(cnt >= ksel, cand, c))
        return tuple(out)

    cs = lax.fori_loop(0, 32, bisect, tuple(jnp.zeros((rg, 1), jnp.int32) for _ in range(DSA_ROW_GROUPS)))
    thr = jnp.concatenate(cs, axis=0) ^ INT_MIN
    bias_ref[...] = jnp.where(vis, jnp.where(key_ref[...] >= thr, 0.0, NEG_BIG), NEG_BIG)

    kext = kext_ref[:width, :]
    for h in range(AT_HEADS):
        hs = slice(h * AT_HEAD_DIM, (h + 1) * AT_HEAD_DIM)
        qa = jnp.concatenate([q_ref[:, hs], qext_ref[:, hs]], axis=1)
        ka = jnp.concatenate([k_ref[:width, hs], kext], axis=1)
        logit = lax.dot_general(qa, ka, _NT, preferred_element_type=F32) + bias_ref[...]
        m = jnp.max(logit, axis=1, keepdims=True)
        p = jnp.exp(logit - m)
        den = jnp.sum(p, axis=1, keepdims=True)
        oh = jnp.dot(p.astype(BF16), v_ref[:width, hs], preferred_element_type=F32) / den
        o_ref[:, hs] = oh.astype(o_ref.dtype)


def _dsa(iq_dup, ikt, ikw, zat, batch, seq, tq=DSA_TQ):
    nq = seq // tq
    ksel = min(TOPK_MAX, seq // 4)
    col_step = max(seq // 4, tq)
    qext, kext = _alibi_tables(seq)
    return pl.pallas_call(
        functools.partial(_dsa_kernel, tq=tq, seq=seq, ksel=ksel, col_step=col_step),
        grid=(batch, nq),
        in_specs=[pl.BlockSpec((tq, IDX_HEADS * LANES), lambda b, i: (b * nq + i, 0)),
                  pl.BlockSpec((1, IDX_DIM, seq), lambda b, i: (b, 0, 0)),
                  pl.BlockSpec((tq, LANES), lambda b, i: (b * nq + i, 0)),
                  pl.BlockSpec((tq, AT_WIDTH), lambda b, i: (b * nq + i, 0)),
                  pl.BlockSpec((tq, AT_WIDTH), lambda b, i: (i, 0)),
                  pl.BlockSpec((seq, AT_WIDTH), lambda b, i: (b, 1)),
                  pl.BlockSpec((seq, AT_HEAD_DIM), lambda b, i: (0, 0)),
                  pl.BlockSpec((seq, AT_WIDTH), lambda b, i: (b, 2))],
        out_specs=pl.BlockSpec((tq, AT_WIDTH), lambda b, i: (b * nq + i, 0)),
        out_shape=jax.ShapeDtypeStruct((batch * seq, AT_WIDTH), BF16),
        scratch_shapes=[pltpu.VMEM((tq, seq), F32), pltpu.VMEM((tq, seq), jnp.int32),
                        pltpu.VMEM((tq, seq), F32)],
        compiler_params=_params(("parallel", "arbitrary"), 48),
        name="dsa",
    )(iq_dup, ikt, ikw, zat, qext, zat, kext, zat)


def _merge_kernel(a1_ref, w1_ref, a2_ref, w2_ref, g1_ref, g2_ref, o_ref):
    y1 = jnp.dot(a1_ref[...], w1_ref[...], preferred_element_type=F32)
    y2 = jnp.dot(a2_ref[...], w2_ref[...], preferred_element_type=F32)
    o_ref[...] = (g1_ref[...].astype(F32) * y1 + g2_ref[...].astype(F32) * y2).astype(o_ref.dtype)


def _merge(o_hg, w_hg, o_at, w_at, gates, tm=1024, tn=512):
    m, k1 = o_hg.shape
    k2 = o_at.shape[1]
    n = w_hg.shape[1]
    tm = min(tm, m)
    nb = n // tn
    return pl.pallas_call(
        _merge_kernel,
        grid=(m // tm, nb),
        in_specs=[pl.BlockSpec((tm, k1), lambda i, j: (i, 0)), pl.BlockSpec((k1, tn), lambda i, j: (0, j)),
                  pl.BlockSpec((tm, k2), lambda i, j: (i, 0)), pl.BlockSpec((k2, tn), lambda i, j: (0, j)),
                  pl.BlockSpec((tm, tn), lambda i, j: (i, j)), pl.BlockSpec((tm, tn), lambda i, j: (i, nb + j))],
        out_specs=pl.BlockSpec((tm, tn), lambda i, j: (i, j)),
        out_shape=jax.ShapeDtypeStruct((m, n), BF16),
        compiler_params=_params(("parallel", "arbitrary"), 48),
        name="merge",
    )(o_hg, w_hg, o_at, w_at, gates, gates)


def _outproj_kernel(m_ref, w_ref, x_ref, gn_ref, wr_ref, br_ref, x1_ref, hn_ref, ri_ref, rw_ref):
    x1 = x_ref[...] + jnp.dot(m_ref[...], w_ref[...], preferred_element_type=F32)
    x1_ref[...] = x1
    hn = _rms(x1, gn_ref[...])
    hn_ref[...] = hn
    h_hi = hn.astype(BF16)
    h_lo = (hn - h_hi.astype(F32)).astype(BF16)
    logit = jnp.dot(jnp.concatenate([h_hi, h_hi, h_lo], axis=1), wr_ref[...],
                    preferred_element_type=F32) + br_ref[...]
    lane = lax.broadcasted_iota(jnp.int32, logit.shape, 1)
    lane_f = lane.astype(F32)
    is_g = lane < N_GROUPS
    lg = jnp.where(is_g, logit, -jnp.inf)
    mg = jnp.max(lg, axis=1, keepdims=True)
    pg_top = 1.0 / jnp.sum(jnp.exp(lg - mg), axis=1, keepdims=True)
    gid = jnp.min(jnp.where(lg == mg, lane_f, float(LANES)), axis=1, keepdims=True).astype(jnp.int32)
    in_grp = ((lane - N_GROUPS) >> 3) == gid
    le = jnp.where(in_grp, logit, -jnp.inf)
    m1 = jnp.max(le, axis=1, keepdims=True)
    i1 = jnp.min(jnp.where(le == m1, lane_f, float(LANES)), axis=1, keepdims=True)
    le2 = jnp.where(lane_f == i1, -jnp.inf, le)
    m2 = jnp.max(le2, axis=1, keepdims=True)
    i2 = jnp.min(jnp.where(le2 == m2, lane_f, float(LANES)), axis=1, keepdims=True)
    r = jnp.exp(m2 - m1)
    w1 = pg_top / (1.0 + r)
    w2 = pg_top * r / (1.0 + r)
    e1 = i1.astype(jnp.int32) - N_GROUPS
    e2 = i2.astype(jnp.int32) - N_GROUPS
    ri_ref[...] = jnp.where(lane == 0, e1, jnp.where(lane == 1, e2, 0))
    rw_ref[...] = jnp.where(lane == 0, w1, jnp.where(lane == 1, w2, 0.0))


def _outproj(mixed, w_out, x, gn, w_r, b_r, tm=256):
    m, d = x.shape
    row = lambda w: pl.BlockSpec((tm, w), lambda i: (i, 0))
    const = lambda a: pl.BlockSpec(a.shape, lambda i: (0, 0))
    gn = gn.reshape(1, d)
    b_r = b_r.reshape(1, LANES)
    return pl.pallas_call(
        _outproj_kernel,
        grid=(m // tm,),
        in_specs=[row(d), const(w_out), row(d), const(gn), const(w_r), const(b_r)],
        out_specs=[row(d), row(d), row(LANES), row(LANES)],
        out_shape=[jax.ShapeDtypeStruct((m, d), F32), jax.ShapeDtypeStruct((m, d), F32),
                   jax.ShapeDtypeStruct((m, LANES), jnp.int32), jax.ShapeDtypeStruct((m, LANES), F32)],
        compiler_params=_params(("parallel",), 56),
        name="outproj_router",
    )(mixed, w_out, x, gn, w_r, b_r)


def _expert_kernel(te_ref, tv_ref, src_ref, dst_ref, hn_hbm, wg_ref, wu_ref, wd_ref, y_hbm,
                   xbuf, ybuf, xb_ref, wgb, wub, wdb, gsem, ssem, *, tm):
    t = pl.program_id(0)
    slot = lax.rem(t, 2)
    other = 1 - slot
    valid = tv_ref[t] > 0
    prev_valid = jnp.logical_and(t >= 1, tv_ref[jnp.maximum(t - 1, 0)] > 0)

    def gather_copy(tile, r, s):
        tok = src_ref[tile * tm + r]
        return pltpu.make_async_copy(hn_hbm.at[pl.ds(tok, 1), :], xbuf.at[s, pl.ds(r, 1), :], gsem.at[s])

    def scatter_copy(tile, r, s):
        dst = dst_ref[(tile + 1) * tm + r]
        return pltpu.make_async_copy(ybuf.at[s, pl.ds(r, 1), :], y_hbm.at[pl.ds(dst, 1), :], ssem.at[s])

    def start_rows(copy_fn, tile, s):
        def body(r, carry):
            copy_fn(tile, r, s).start()
            return carry

        lax.fori_loop(0, tm, body, 0, unroll=8)

    def gather_wait(s):
        pltpu.make_async_copy(hn_hbm.at[pl.ds(0, tm), :], xbuf.at[s], gsem.at[s]).wait()

    def scatter_wait(s):
        pltpu.make_async_copy(ybuf.at[s], y_hbm.at[pl.ds(0, tm), :], ssem.at[s]).wait()

    @pl.when(t == 0)
    def _():
        start_rows(gather_copy, 0, 0)
        ybuf[1] = jnp.zeros(ybuf.shape[1:], F32)

    @pl.when(prev_valid)
    def _():
        scatter_wait(slot)

    @pl.when(valid)
    def _():
        gather_wait(slot)

        @pl.when(jnp.logical_or(t == 0, te_ref[t] != te_ref[jnp.maximum(t - 1, 0)]))
        def _():
            wgb[...] = wg_ref[0].astype(BF16)
            wub[...] = wu_ref[0].astype(BF16)
            wdb[...] = wd_ref[0].astype(BF16)

        xb_ref[...] = xbuf[slot].astype(BF16)
        for r in range(tm):
            gather_copy(t + 1, r, other).start()
        for r in range(tm):
            scatter_copy(t - 1, r, other).start()
        xb = xb_ref[...]
        g = jnp.dot(xb, wgb[...], preferred_element_type=F32)
        u = jnp.dot(xb, wub[...], preferred_element_type=F32)
        hid = (g * _sigmoid(g) * u).astype(BF16)
        ybuf[slot] = jnp.dot(hid, wdb[...], preferred_element_type=F32)

    @pl.when(jnp.logical_and(jnp.logical_not(valid), prev_valid))
    def _():
        gather_wait(slot)
        start_rows(scatter_copy, t - 1, other)
        scatter_wait(other)


def _experts(hn, tile_expert, tile_valid, src, dst, w_g, w_u, w_d, tm):
    n_tok, d = hn.shape
    n_steps = tile_expert.shape[0]
    wspec = lambda a: pl.BlockSpec((1,) + a.shape[1:], lambda t, te, tv, sr, ds: (te[t], 0, 0))
    return pl.pallas_call(
        functools.partial(_expert_kernel, tm=tm),
        grid_spec=pltpu.PrefetchScalarGridSpec(
            num_scalar_prefetch=4,
            grid=(n_steps,),
            in_specs=[pl.BlockSpec(memory_space=pl.ANY), wspec(w_g), wspec(w_u), wspec(w_d)],
            out_specs=pl.BlockSpec(memory_space=pl.ANY),
            scratch_shapes=[pltpu.VMEM((2, tm, d), F32), pltpu.VMEM((2, tm, d), F32), pltpu.VMEM((tm, d), BF16),
                            pltpu.VMEM(w_g.shape[1:], BF16), pltpu.VMEM(w_u.shape[1:], BF16),
                            pltpu.VMEM(w_d.shape[1:], BF16),
                            pltpu.SemaphoreType.DMA((2,)), pltpu.SemaphoreType.DMA((2,))]),
        out_shape=jax.ShapeDtypeStruct((2 * n_tok + tm, d), F32),
        compiler_params=_params(("arbitrary",), 56),
        name="experts",
    )(tile_expert, tile_valid, src, dst, hn, w_g, w_u, w_d)


def _dispatch_plan(route_e, tm):
    n_tok = route_e.shape[0]
    n_pairs = 2 * n_tok
    n_tiles = n_pairs // tm + N_EXPERTS + 1
    eflat = route_e.T.reshape(-1)
    order = jnp.argsort(eflat, stable=True).astype(jnp.int32)
    counts = jnp.sum(eflat[:, None] == jnp.arange(N_EXPERTS, dtype=jnp.int32)[None, :], axis=0, dtype=jnp.int32)
    tiles_e = (counts + tm - 1) // tm
    tile_end = jnp.cumsum(tiles_e)
    tile_off = tile_end - tiles_e
    start = jnp.cumsum(counts) - counts
    tid = jnp.arange(n_tiles, dtype=jnp.int32)
    te = jnp.minimum(jnp.sum(tile_end[None, :] <= tid[:, None], axis=1), N_EXPERTS - 1).astype(jnp.int32)
    tv = jnp.clip(counts[te] - (tid - tile_off[te]) * tm, 0, tm)
    used = tid < tile_end[-1]
    last_e = te[jnp.maximum(tile_end[-1] - 1, 0)]
    te = jnp.where(used, te, last_e).astype(jnp.int32)
    tv = jnp.where(used, tv, 0).astype(jnp.int32)
    r = jnp.arange(tm, dtype=jnp.int32)[None, :]
    src = (start[te] + (tid - tile_off[te]) * tm)[:, None] + r
    dst = jnp.where(r < tv[:, None], order[jnp.clip(src, 0, n_pairs - 1)], n_pairs + r).reshape(-1).astype(jnp.int32)
    spare = n_pairs + jnp.arange(tm, dtype=jnp.int32)
    return te, tv, dst % n_tok, jnp.concatenate([spare, dst])


def _final_kernel(x1_ref, y0_ref, y1_ref, rw_ref, p_ref, wgate_ref, wproj_ref, gple_ref, gfin_ref, o_ref,
                  *, last_layer):
    rw = rw_ref[...]
    x2 = x1_ref[...] + rw[:, 0:1] * y0_ref[...] + rw[:, 1:2] * y1_ref[...]
    hn = _rms(x2, gple_ref[...]).astype(BF16)
    gate = _sigmoid(jnp.dot(hn, wgate_ref[...], preferred_element_type=F32))
    emb = jnp.dot(p_ref[...].astype(BF16), wproj_ref[...], preferred_element_type=F32)
    x3 = x2 + gate * emb
    o_ref[...] = _rms(x3, gfin_ref[...]) if last_layer else x3


def _final(x1, y, rw, p, w_gate, w_proj, g_ple, g_fin, last_layer, tm=256):
    m, d = x1.shape
    nb = m // tm
    row = lambda w: pl.BlockSpec((tm, w), lambda i: (i, 0))
    const = lambda a: pl.BlockSpec(a.shape, lambda i: (0, 0))
    g_ple, g_fin = g_ple.reshape(1, d), g_fin.reshape(1, d)
    return pl.pallas_call(
        functools.partial(_final_kernel, last_layer=last_layer),
        grid=(nb,),
        in_specs=[row(d), row(d), pl.BlockSpec((tm, d), lambda i: (nb + i, 0)), row(LANES), row(p.shape[1]),
                  const(w_gate), const(w_proj), const(g_ple), const(g_fin)],
        out_specs=row(d),
        out_shape=jax.ShapeDtypeStruct((m, d), F32),
        compiler_params=_params(("parallel",), 56),
        name="final",
    )(x1, y, y, rw, p, w_gate, w_proj, g_ple, g_fin)


def kernel(x, p, attn_norm, w_in, b_in, hg_lb_logits, hg_out_norm, w_hg_up, w_attn_up, w_out, ffn_norm,
           w_router_group, b_router_group, w_router_expert, b_router_expert, w_exp_gate, w_exp_up,
           w_exp_down, ple_norm, w_ple_gate, w_ple_proj, final_norm):
    batch, seq, d = x.shape
    n = batch * seq
    depth = w_in.shape[0]
    o_hf, o_at = 0, 4 * HG_WIDTH
    o_iq = o_at + 3 * AT_WIDTH
    o_ik = o_iq + IDX_HEADS * IDX_DIM
    o_iw = o_ik + IDX_DIM
    o_g = o_iw + IDX_HEADS
    lbs = jnp.cumsum(jax.nn.softmax(hg_lb_logits.astype(F32), axis=0), axis=0)
    xf = x.reshape(n, d)
    moe_tm = min(256, n // 2)
    for li in range(depth):
        w, b = w_in[li], b_in[li]
        h = _rmsnorm(xf, attn_norm[li], BF16)
        zhg = _matmul(h, w[:, o_hf:o_at].astype(BF16), b[o_hf:o_at], F32, name="proj_hgrn")
        qkv_scale = jnp.concatenate([jnp.full((AT_WIDTH,), AT_HEAD_DIM ** -0.5, F32), jnp.ones((2 * AT_WIDTH,), F32)])
        zat = _matmul(h, (w[:, o_at:o_iq] * qkv_scale).astype(BF16), b[o_at:o_iq] * qkv_scale, BF16,
                      name="proj_attn")
        dup = lambda a: jnp.repeat(a.reshape(a.shape[:-1] + (IDX_HEADS, 1, IDX_DIM)), 2, axis=-2).reshape(
            a.shape[:-1] + (2 * IDX_HEADS * IDX_DIM,))
        ziq = _matmul(h, dup(w[:, o_iq:o_ik]).astype(BF16), dup(b[o_iq:o_ik]), F32, name="proj_iq")
        pad = LANES - IDX_DIM - IDX_HEADS
        zikw = _matmul(h, jnp.pad(w[:, o_ik:o_g], ((0, 0), (0, pad))).astype(BF16),
                       jnp.pad(b[o_ik:o_g], (0, pad)), F32, name="proj_ikw")
        gates = _matmul(h, w[:, o_g:].astype(BF16), b[o_g:], BF16, act="sigmoid", name="proj_gates")

        o_hg = _hgrn(zhg, lbs[li], hg_out_norm[li], batch, seq)
        ikt = zikw[:, :IDX_DIM].reshape(batch, seq, IDX_DIM).transpose(0, 2, 1)
        o_att = _dsa(ziq, ikt, zikw, zat, batch, seq)
        mixed = _merge(o_hg, w_hg_up[li].astype(BF16), o_att, w_attn_up[li].astype(BF16), gates)

        w_r = jnp.pad(jnp.concatenate([w_router_group[li], w_router_expert[li]], axis=1),
                      ((0, 0), (0, LANES - N_GROUPS - N_EXPERTS)))
        b_r = jnp.pad(jnp.concatenate([b_router_group[li], b_router_expert[li]]), (0, LANES - N_GROUPS - N_EXPERTS))
        w_r_hi = w_r.astype(BF16)
        w_r_lo = (w_r - w_r_hi.astype(F32)).astype(BF16)
        w_r = jnp.concatenate([w_r_hi, w_r_lo, w_r_hi], axis=0)
        x1, hn, route_e, route_w = _outproj(mixed, w_out[li].astype(BF16), xf, ffn_norm[li], w_r, b_r)
        tile_e, tile_v, src, dst = _dispatch_plan(route_e[:, :2], moe_tm)
        y = _experts(hn, tile_e, tile_v, src, dst, w_exp_gate[li], w_exp_up[li], w_exp_down[li], moe_tm)
        xf = _final(x1, y, route_w, p[li].reshape(n, -1), w_ple_gate[li].astype(BF16),
                    w_ple_proj[li].astype(BF16), ple_norm[li], final_norm, li == depth - 1)
    return xf.reshape(batch, seq, d)
```

```python
import functools

import numpy as np
import jax
import jax.numpy as jnp
from jax import lax
from jax.experimental import pallas as pl
from jax.experimental.pallas import tpu as pltpu

F32 = jnp.float32
BF16 = jnp.bfloat16
HIGHEST = lax.Precision.HIGHEST

HG_HEADS = 8
HG_DK = 128
HG_DV = 128
HG_WIDTH = HG_HEADS * HG_DK
HG_CHUNK = 64
AT_HEADS = 8
AT_HEAD_DIM = 128
AT_WIDTH = AT_HEADS * AT_HEAD_DIM
IDX_HEADS = 16
IDX_DIM = 64
TOPK_MAX = 256
N_GROUPS = 4
EXPERTS_PER_GROUP = 8
N_EXPERTS = N_GROUPS * EXPERTS_PER_GROUP
D_EXPERT = 512
RMS_EPS = 1e-6
LANES = 128
INT_MIN = -(2 ** 31)
NEG_BIG = -1e30

_NT = (((1,), (1,)), ((), ()))
_TN = (((0,), (0,)), ((), ()))


def _params(sem, vmem_mb=None):
    kw = dict(dimension_semantics=sem)
    if vmem_mb is not None:
        kw["vmem_limit_bytes"] = vmem_mb * 1024 * 1024
    return pltpu.CompilerParams(**kw)


def _sigmoid(x):
    return 1.0 / (1.0 + jnp.exp(-x))


def _rms(x, g):
    ms = jnp.mean(x * x, axis=-1, keepdims=True)
    return x * lax.rsqrt(ms + RMS_EPS) * g


def _rmsnorm_kernel(x_ref, g_ref, o_ref):
    o_ref[...] = _rms(x_ref[...], g_ref[...]).astype(o_ref.dtype)


def _rmsnorm(x, g, out_dtype, tm=512):
    m, d = x.shape
    return pl.pallas_call(
        _rmsnorm_kernel,
        grid=(m // tm,),
        in_specs=[pl.BlockSpec((tm, d), lambda i: (i, 0)), pl.BlockSpec((1, d), lambda i: (0, 0))],
        out_specs=pl.BlockSpec((tm, d), lambda i: (i, 0)),
        out_shape=jax.ShapeDtypeStruct((m, d), out_dtype),
        compiler_params=_params(("parallel",)),
        name="rmsnorm",
    )(x, g.reshape(1, d))


def _mm_kernel(a_ref, wt_ref, b_ref, o_ref, *, act):
    acc = lax.dot_general(a_ref[...], wt_ref[...], _NT, preferred_element_type=F32) + b_ref[...]
    if act == "sigmoid":
        acc = _sigmoid(acc)
    o_ref[...] = acc.astype(o_ref.dtype)


def _matmul(a, wt, b, out_dtype, act=None, tm=1024, tn=512, name="proj"):
    m, k = a.shape
    n = wt.shape[0]
    tm, tn = min(tm, m), min(tn, n)
    return pl.pallas_call(
        functools.partial(_mm_kernel, act=act),
        grid=(m // tm, n // tn),
        in_specs=[pl.BlockSpec((tm, k), lambda i, j: (i, 0)),
                  pl.BlockSpec((tn, k), lambda i, j: (j, 0)),
                  pl.BlockSpec((1, tn), lambda i, j: (0, j))],
        out_specs=pl.BlockSpec((tm, tn), lambda i, j: (i, j)),
        out_shape=jax.ShapeDtypeStruct((m, n), out_dtype),
        compiler_params=_params(("parallel", "arbitrary"), 48),
        name=name,
    )(a, wt, b.reshape(1, n))


HG_GROUP = 8
HG_ROWS = 512


def _hgrn_tables():
    c = HG_CHUNK
    levels = [c >> i for i in range(int(np.log2(c)))]
    t = np.arange(c)
    u = np.arange(c)
    rall = [(u[None, :] <= t[:, None]), (u[None, :] > t[:, None])]
    low, mask = [], []
    for p in levels:
        half = p // 2
        m = (t // p) * p + half
        lower = (t % p) >= half
        r = np.where(lower[:, None], (u[None, :] > m[:, None]) & (u[None, :] <= t[:, None]),
                     (u[None, :] > t[:, None]) & (u[None, :] <= m[:, None]))
        rall.append(r)
        low.append(np.broadcast_to(lower[:, None], (c, LANES)))
        mask.append(((t[:, None] // p) == (t[None, :] // p)) & lower[:, None] & ~lower[None, :])
    mask.append(np.eye(c, dtype=bool))
    rall = np.concatenate(rall, 0).astype(np.float32)
    return (np.tile(rall, (1, 3)), np.stack(low).astype(np.float32), np.stack(mask).astype(np.float32),
            len(levels))


def _hgrn_kernel(q_ref, f_ref, i_ref, g_ref, lb_ref, gn_ref, rall_ref, low_ref, mask_ref, o_ref, st_ref,
                 *, n_chunks, n_levels, heads):
    c = HG_CHUNK

    @pl.when(pl.program_id(2) == 0)
    def _():
        st_ref[...] = jnp.zeros_like(st_ref)

    lb = lb_ref[...]
    one_m_lb = 1.0 - lb
    gn = gn_ref[...]

    def body(ci, carry):
        rows = pl.ds(pl.multiple_of(ci * c, c), c)
        z = f_ref[rows, :]
        q = q_ref[rows, :]
        e = jnp.exp(-jnp.abs(z))
        r = 1.0 / (1.0 + e)
        pos = z >= 0
        sig = jnp.where(pos, r, e * r)
        nsig = jnp.where(pos, e * r, r)
        logf = jnp.log(lb + one_m_lb * sig)
        k = one_m_lb * nsig
        qs = q * _sigmoid(q)
        l1 = logf.astype(BF16)
        r1 = logf - l1.astype(F32)
        l2 = r1.astype(BF16)
        l3 = (r1 - l2.astype(F32)).astype(BF16)
        ex = jnp.exp(jnp.dot(rall_ref[...], jnp.concatenate([l1, l2, l3], axis=0), preferred_element_type=F32))
        vb = i_ref[rows, :].astype(BF16)
        gate = g_ref[rows, :]
        gate = gate * _sigmoid(gate)
        for g in range(heads):
            sl = slice(g * HG_DK, (g + 1) * HG_DK)
            qg, kg, vg = qs[:, sl], k[:, sl], vb[:, sl]
            eb = ex[0:c, sl]
            a = mask_ref[n_levels] * lax.dot_general(qg.astype(BF16), kg.astype(BF16), _NT,
                                                     preferred_element_type=F32)
            for li in range(n_levels):
                w = (jnp.where(low_ref[li] != 0.0, qg, kg) * ex[(2 + li) * c:(3 + li) * c, sl]).astype(BF16)
                a = a + mask_ref[li] * lax.dot_general(w, w, _NT, preferred_element_type=F32)
            st = st_ref[g]
            o = (jnp.dot(a.astype(BF16), vg, preferred_element_type=F32)
                 + lax.dot_general((qg * eb).astype(BF16), st.astype(BF16), _NT, preferred_element_type=F32))
            st_ref[g] = st * eb[c - 1:c, :] + lax.dot_general(vg, (kg * ex[c:2 * c, sl]).astype(BF16), _TN,
                                                              preferred_element_type=F32)
            o_ref[rows, sl] = (_rms(o, gn) * gate[:, sl]).astype(o_ref.dtype)
        return carry

    lax.fori_loop(0, n_chunks, body, 0, unroll=2)


def _hgrn(zhg, lb, gn, batch, seq):
    rall, low, mask, n_levels = _hgrn_tables()
    hb = HG_HEADS // HG_GROUP
    gw = HG_GROUP * HG_DK
    tc = min(HG_ROWS, seq)
    nt = seq // tc
    col = lambda off: pl.BlockSpec((tc, gw), lambda b, hh, tt: (b * nt + tt, off + hh))
    full = lambda a: pl.BlockSpec(a.shape, lambda b, hh, tt: (0,) * a.ndim)
    return pl.pallas_call(
        functools.partial(_hgrn_kernel, n_chunks=tc // HG_CHUNK, n_levels=n_levels, heads=HG_GROUP),
        grid=(batch, hb, nt),
        in_specs=[col(0), col(hb), col(2 * hb), col(3 * hb),
                  pl.BlockSpec((1, gw), lambda b, hh, tt: (0, hh)),
                  pl.BlockSpec((1, HG_DV), lambda b, hh, tt: (0, 0)),
                  full(rall), full(low), full(mask)],
        out_specs=pl.BlockSpec((tc, gw), lambda b, hh, tt: (b * nt + tt, hh)),
        out_shape=jax.ShapeDtypeStruct((batch * seq, HG_HEADS * HG_DV), BF16),
        scratch_shapes=[pltpu.VMEM((HG_GROUP, HG_DV, HG_DK), F32)],
        compiler_params=_params(("parallel", "parallel", "arbitrary"), 48),
        name="hgrn2",
    )(zhg, zhg, zhg, zhg, lb.reshape(1, HG_WIDTH), gn.reshape(1, HG_DV),
      jnp.asarray(rall, BF16), jnp.asarray(low), jnp.asarray(mask))


DSA_TQ = 256
DSA_ROW_GROUPS = 4


def _alibi_tables(seq):
    assert 8 % AT_HEADS == 0 and seq <= 64 * 256
    pos = jnp.arange(seq, dtype=jnp.int32)
    hi, lo = (pos // 64).astype(F32), (pos % 64).astype(F32)
    one = jnp.ones((seq,), F32)
    kext = jnp.stack([one, one, hi, lo], axis=1)
    kext = jnp.pad(kext, ((0, 0), (0, AT_HEAD_DIM - 4))).astype(BF16)
    qext = []
    for h in range(AT_HEADS):
        slope = 2.0 ** (-8.0 * (h + 1) / AT_HEADS)
        cols = jnp.stack([-slope * 64.0 * hi, -slope * lo, slope * 64.0 * one, slope * one], axis=1)
        qext.append(jnp.pad(cols, ((0, 0), (0, AT_HEAD_DIM - 4))))
    return jnp.concatenate(qext, axis=1).astype(BF16), kext


def _dsa_kernel(iq_ref, ikt_ref, iw_ref, q_ref, qext_ref, k_ref, kext_ref, v_ref, o_ref, acc_ref, key_ref, bias_ref,
                *, tq, seq, ksel, col_step):
    first_row = pl.program_id(1) * tq
    for g in range(seq // col_step):
        @pl.when(first_row // col_step == g)
        def _(width=(g + 1) * col_step):
            _dsa_body(iq_ref, ikt_ref, iw_ref, q_ref, qext_ref, k_ref, kext_ref, v_ref, o_ref, acc_ref, key_ref,
                      bias_ref, tq=tq, ksel=ksel, width=width, first_row=first_row)


def _dsa_body(iq_ref, ikt_ref, iw_ref, q_ref, qext_ref, k_ref, kext_ref, v_ref, o_ref, acc_ref, key_ref, bias_ref,
              *, tq, ksel, width, first_row):
    seq = width
    t0 = first_row
    acc_ref, key_ref, bias_ref = acc_ref.at[:, :width], key_ref.at[:, :width], bias_ref.at[:, :width]
    ikt = ikt_ref[0, :, :width]
    ik_hi = ikt.astype(BF16)
    ik_lo = (ikt - ik_hi.astype(F32)).astype(BF16)
    rhs = jnp.concatenate([ik_hi, ik_hi, ik_lo, ik_lo], axis=0)
    lane = lax.broadcasted_iota(jnp.int32, (tq, LANES), 1)
    w_all = iw_ref[...] * (IDX_HEADS ** -0.5 * IDX_DIM ** -0.5)
    for h in range(IDX_HEADS):
        x = iq_ref[:, h * LANES:(h + 1) * LANES]
        x_hi = x.astype(BF16).astype(F32)
        u = jnp.where(lane < IDX_DIM, x_hi, x - x_hi).astype(BF16)
        s = jnp.dot(jnp.concatenate([u, u], axis=1), rhs, preferred_element_type=F32)
        term = jnp.maximum(s, 0.0) * w_all[:, IDX_DIM + h:IDX_DIM + h + 1]
        if h == 0:
            acc_ref[...] = term
        else:
            acc_ref[...] += term

    col = lax.broadcasted_iota(jnp.int32, (tq, seq), 1)
    row = t0 + lax.broadcasted_iota(jnp.int32, (tq, seq), 0)
    vis = col <= row
    bits = pltpu.bitcast(acc_ref[...], jnp.int32)
    key = bits ^ ((bits >> 31) & 0x7FFFFFFF)
    key_ref[...] = jnp.where(vis, key, INT_MIN)

    rg = tq // DSA_ROW_GROUPS

    def bisect(it, cs):
        bit = jnp.left_shift(jnp.int32(1), 31 - it)
        out = []
        for gi, c in enumerate(cs):
            cand = c | bit
            keys = key_ref[gi * rg:(gi + 1) * rg, :]
            cnt = jnp.sum((keys >= (cand ^ INT_MIN)).astype(jnp.int32), axis=1, keepdims=True)
            out.append(jnp.where(cnt >= ksel, cand, c))
        return tuple(out)

    cs = lax.fori_loop(0, 32, bisect, tuple(jnp.zeros((rg, 1), jnp.int32) for _ in range(DSA_ROW_GROUPS)))
    thr = jnp.concatenate(cs, axis=0) ^ INT_MIN
    bias_ref[...] = jnp.where(vis, jnp.where(key_ref[...] >= thr, 0.0, NEG_BIG), NEG_BIG)

    kext = kext_ref[:width, :]
    for h in range(AT_HEADS):
        hs = slice(h * AT_HEAD_DIM, (h + 1) * AT_HEAD_DIM)
        qa = jnp.concatenate([q_ref[:, hs], qext_ref[:, hs]], axis=1)
        ka = jnp.concatenate([k_ref[:width, hs], kext], axis=1)
        logit = lax.dot_general(qa, ka, _NT, preferred_element_type=F32) + bias_ref[...]
        m = jnp.max(logit, axis=1, keepdims=True)
        p = jnp.exp(logit - m)
        den = jnp.sum(p, axis=1, keepdims=True)
        oh = jnp.dot(p.astype(BF16), v_ref[:width, hs], preferred_element_type=F32) / den
        o_ref[:, hs] = oh.astype(o_ref.dtype)


def _dsa(iq_dup, ikt, ikw, zat, batch, seq, tq=DSA_TQ):
    nq = seq // tq
    ksel = min(TOPK_MAX, seq // 4)
    col_step = max(seq // 4, tq)
    qext, kext = _alibi_tables(seq)
    return pl.pallas_call(
        functools.partial(_dsa_kernel, tq=tq, seq=seq, ksel=ksel, col_step=col_step),
        grid=(batch, nq),
        in_specs=[pl.BlockSpec((tq, IDX_HEADS * LANES), lambda b, i: (b * nq + i, 0)),
                  pl.BlockSpec((1, IDX_DIM, seq), lambda b, i: (b, 0, 0)),
                  pl.BlockSpec((tq, LANES), lambda b, i: (b * nq + i, 0)),
                  pl.BlockSpec((tq, AT_WIDTH), lambda b, i: (b * nq + i, 0)),
                  pl.BlockSpec((tq, AT_WIDTH), lambda b, i: (i, 0)),
                  pl.BlockSpec((seq, AT_WIDTH), lambda b, i: (b, 1)),
                  pl.BlockSpec((seq, AT_HEAD_DIM), lambda b, i: (0, 0)),
                  pl.BlockSpec((seq, AT_WIDTH), lambda b, i: (b, 2))],
        out_specs=pl.BlockSpec((tq, AT_WIDTH), lambda b, i: (b * nq + i, 0)),
        out_shape=jax.ShapeDtypeStruct((batch * seq, AT_WIDTH), BF16),
        scratch_shapes=[pltpu.VMEM((tq, seq), F32), pltpu.VMEM((tq, seq), jnp.int32),
                        pltpu.VMEM((tq, seq), F32)],
        compiler_params=_params(("parallel", "arbitrary"), 48),
        name="dsa",
    )(iq_dup, ikt, ikw, zat, qext, zat, kext, zat)


def _merge_kernel(a1_ref, w1_ref, a2_ref, w2_ref, g1_ref, g2_ref, o_ref):
    y1 = jnp.dot(a1_ref[...], w1_ref[...], preferred_element_type=F32)
    y2 = jnp.dot(a2_ref[...], w2_ref[...], preferred_element_type=F32)
    o_ref[...] = (g1_ref[...].astype(F32) * y1 + g2_ref[...].astype(F32) * y2).astype(o_ref.dtype)


def _merge(o_hg, w_hg, o_at, w_at, gates, tm=1024, tn=512):
    m, k1 = o_hg.shape
    k2 = o_at.shape[1]
    n = w_hg.shape[1]
    tm = min(tm, m)
    nb = n // tn
    return pl.pallas_call(
        _merge_kernel,
        grid=(m // tm, nb),
        in_specs=[pl.BlockSpec((tm, k1), lambda i, j: (i, 0)), pl.BlockSpec((k1, tn), lambda i, j: (0, j)),
                  pl.BlockSpec((tm, k2), lambda i, j: (i, 0)), pl.BlockSpec((k2, tn), lambda i, j: (0, j)),
                  pl.BlockSpec((tm, tn), lambda i, j: (i, j)), pl.BlockSpec((tm, tn), lambda i, j: (i, nb + j))],
        out_specs=pl.BlockSpec((tm, tn), lambda i, j: (i, j)),
        out_shape=jax.ShapeDtypeStruct((m, n), BF16),
        compiler_params=_params(("parallel", "arbitrary"), 48),
        name="merge",
    )(o_hg, w_hg, o_at, w_at, gates, gates)


def _outproj_kernel(m_ref, w_ref, x_ref, gn_ref, wr_ref, br_ref, x1_ref, hn_ref, ri_ref, rw_ref):
    x1 = x_ref[...] + jnp.dot(m_ref[...], w_ref[...], preferred_element_type=F32)
    x1_ref[...] = x1
    hn = _rms(x1, gn_ref[...])
    hn_ref[...] = hn
    h_hi = hn.astype(BF16)
    h_lo = (hn - h_hi.astype(F32)).astype(BF16)
    logit = jnp.dot(jnp.concatenate([h_hi, h_hi, h_lo], axis=1), wr_ref[...],
                    preferred_element_type=F32) + br_ref[...]
    lane = lax.broadcasted_iota(jnp.int32, logit.shape, 1)
    lane_f = lane.astype(F32)
    is_g = lane < N_GROUPS
    lg = jnp.where(is_g, logit, -jnp.inf)
    mg = jnp.max(lg, axis=1, keepdims=True)
    pg_top = 1.0 / jnp.sum(jnp.exp(lg - mg), axis=1, keepdims=True)
    gid = jnp.min(jnp.where(lg == mg, lane_f, float(LANES)), axis=1, keepdims=True).astype(jnp.int32)
    in_grp = ((lane - N_GROUPS) >> 3) == gid
    le = jnp.where(in_grp, logit, -jnp.inf)
    m1 = jnp.max(le, axis=1, keepdims=True)
    i1 = jnp.min(jnp.where(le == m1, lane_f, float(LANES)), axis=1, keepdims=True)
    le2 = jnp.where(lane_f == i1, -jnp.inf, le)
    m2 = jnp.max(le2, axis=1, keepdims=True)
    i2 = jnp.min(jnp.where(le2 == m2, lane_f, float(LANES)), axis=1, keepdims=True)
    r = jnp.exp(m2 - m1)
    w1 = pg_top / (1.0 + r)
    w2 = pg_top * r / (1.0 + r)
    e1 = i1.astype(jnp.int32) - N_GROUPS
    e2 = i2.astype(jnp.int32) - N_GROUPS
    ri_ref[...] = jnp.where(lane == 0, e1, jnp.where(lane == 1, e2, 0))
    rw_ref[...] = jnp.where(lane == 0, w1, jnp.where(lane == 1, w2, 0.0))


def _outproj(mixed, w_out, x, gn, w_r, b_r, tm=256):
    m, d = x.shape
    row = lambda w: pl.BlockSpec((tm, w), lambda i: (i, 0))
    const = lambda a: pl.BlockSpec(a.shape, lambda i: (0, 0))
    gn = gn.reshape(1, d)
    b_r = b_r.reshape(1, LANES)
    return pl.pallas_call(
        _outproj_kernel,
        grid=(m // tm,),
        in_specs=[row(mixed.shape[1]), const(w_out), row(d), const(gn), const(w_r), const(b_r)],
        out_specs=[row(d), row(d), row(LANES), row(LANES)],
        out_shape=[jax.ShapeDtypeStruct((m, d), F32), jax.ShapeDtypeStruct((m, d), F32),
                   jax.ShapeDtypeStruct((m, LANES), jnp.int32), jax.ShapeDtypeStruct((m, LANES), F32)],
        compiler_params=_params(("parallel",), 56),
        name="outproj_router",
    )(mixed, w_out, x, gn, w_r, b_r)


def _expert_kernel(te_ref, tv_ref, src_ref, dst_ref, hn_hbm, wg_ref, wu_ref, wd_ref, y_hbm,
                   xbuf, ybuf, xb_ref, wgb, wub, wdb, gsem, ssem, *, tm):
    t = pl.program_id(0)
    slot = lax.rem(t, 2)
    other = 1 - slot
    valid = tv_ref[t] > 0
    prev_valid = jnp.logical_and(t >= 1, tv_ref[jnp.maximum(t - 1, 0)] > 0)

    def gather_copy(tile, r, s):
        tok = src_ref[tile * tm + r]
        return pltpu.make_async_copy(hn_hbm.at[pl.ds(tok, 1), :], xbuf.at[s, pl.ds(r, 1), :], gsem.at[s])

    def scatter_copy(tile, r, s):
        dst = dst_ref[(tile + 1) * tm + r]
        return pltpu.make_async_copy(ybuf.at[s, pl.ds(r, 1), :], y_hbm.at[pl.ds(dst, 1), :], ssem.at[s])

    def start_rows(copy_fn, tile, s):
        def body(r, carry):
            copy_fn(tile, r, s).start()
            return carry

        lax.fori_loop(0, tm, body, 0, unroll=8)

    def gather_wait(s):
        pltpu.make_async_copy(hn_hbm.at[pl.ds(0, tm), :], xbuf.at[s], gsem.at[s]).wait()

    def scatter_wait(s):
        pltpu.make_async_copy(ybuf.at[s], y_hbm.at[pl.ds(0, tm), :], ssem.at[s]).wait()

    @pl.when(t == 0)
    def _():
        start_rows(gather_copy, 0, 0)
        ybuf[1] = jnp.zeros(ybuf.shape[1:], F32)

    @pl.when(prev_valid)
    def _():
        scatter_wait(slot)

    @pl.when(valid)
    def _():
        gather_wait(slot)

        @pl.when(jnp.logical_or(t == 0, te_ref[t] != te_ref[jnp.maximum(t - 1, 0)]))
        def _():
            wgb[...] = wg_ref[0].astype(BF16)
            wub[...] = wu_ref[0].astype(BF16)
            wdb[...] = wd_ref[0].astype(BF16)

        xb_ref[...] = xbuf[slot].astype(BF16)
        for r in range(tm):
            gather_copy(t + 1, r, other).start()
        for r in range(tm):
            scatter_copy(t - 1, r, other).start()
        xb = xb_ref[...]
        g = jnp.dot(xb, wgb[...], preferred_element_type=F32)
        u = jnp.dot(xb, wub[...], preferred_element_type=F32)
        hid = (g * _sigmoid(g) * u).astype(BF16)
        ybuf[slot] = jnp.dot(hid, wdb[...], preferred_element_type=F32)

    @pl.when(jnp.logical_and(jnp.logical_not(valid), prev_valid))
    def _():
        gather_wait(slot)
        start_rows(scatter_copy, t - 1, other)
        scatter_wait(other)


def _experts(hn, tile_expert, tile_valid, src, dst, w_g, w_u, w_d, tm):
    n_tok, d = hn.shape
    n_steps = tile_expert.shape[0]
    wspec = lambda a: pl.BlockSpec((1,) + a.shape[1:], lambda t, te, tv, sr, ds: (te[t], 0, 0))
    return pl.pallas_call(
        functools.partial(_expert_kernel, tm=tm),
        grid_spec=pltpu.PrefetchScalarGridSpec(
            num_scalar_prefetch=4,
            grid=(n_steps,),
            in_specs=[pl.BlockSpec(memory_space=pl.ANY), wspec(w_g), wspec(w_u), wspec(w_d)],
            out_specs=pl.BlockSpec(memory_space=pl.ANY),
            scratch_shapes=[pltpu.VMEM((2, tm, d), F32), pltpu.VMEM((2, tm, d), F32), pltpu.VMEM((tm, d), BF16),
                            pltpu.VMEM(w_g.shape[1:], BF16), pltpu.VMEM(w_u.shape[1:], BF16),
                            pltpu.VMEM(w_d.shape[1:], BF16),
                            pltpu.SemaphoreType.DMA((2,)), pltpu.SemaphoreType.DMA((2,))]),
        out_shape=jax.ShapeDtypeStruct((2 * n_tok + tm, d), F32),
        compiler_params=_params(("arbitrary",), 56),
        name="experts",
    )(tile_expert, tile_valid, src, dst, hn, w_g, w_u, w_d)


def _dispatch_plan(route_e, tm):
    n_tok = route_e.shape[0]
    n_pairs = 2 * n_tok
    n_tiles = n_pairs // tm + N_EXPERTS + 1
    eflat = route_e.T.reshape(-1)
    order = jnp.argsort(eflat, stable=True).astype(jnp.int32)
    counts = jnp.sum(eflat[:, None] == jnp.arange(N_EXPERTS, dtype=jnp.int32)[None, :], axis=0, dtype=jnp.int32)
    tiles_e = (counts + tm - 1) // tm
    tile_end = jnp.cumsum(tiles_e)
    tile_off = tile_end - tiles_e
    start = jnp.cumsum(counts) - counts
    tid = jnp.arange(n_tiles, dtype=jnp.int32)
    te = jnp.minimum(jnp.sum(tile_end[None, :] <= tid[:, None], axis=1), N_EXPERTS - 1).astype(jnp.int32)
    tv = jnp.clip(counts[te] - (tid - tile_off[te]) * tm, 0, tm)
    used = tid < tile_end[-1]
    last_e = te[jnp.maximum(tile_end[-1] - 1, 0)]
    te = jnp.where(used, te, last_e).astype(jnp.int32)
    tv = jnp.where(used, tv, 0).astype(jnp.int32)
    r = jnp.arange(tm, dtype=jnp.int32)[None, :]
    src = (start[te] + (tid - tile_off[te]) * tm)[:, None] + r
    dst = jnp.where(r < tv[:, None], order[jnp.clip(src, 0, n_pairs - 1)], n_pairs + r).reshape(-1).astype(jnp.int32)
    spare = n_pairs + jnp.arange(tm, dtype=jnp.int32)
    return te, tv, dst % n_tok, jnp.concatenate([spare, dst])


def _final_kernel(x1_ref, y0_ref, y1_ref, rw_ref, p_ref, wgate_ref, wproj_ref, gple_ref, gfin_ref, o_ref,
                  *, last_layer):
    rw = rw_ref[...]
    x2 = x1_ref[...] + rw[:, 0:1] * y0_ref[...] + rw[:, 1:2] * y1_ref[...]
    hn = _rms(x2, gple_ref[...]).astype(BF16)
    gate = _sigmoid(jnp.dot(hn, wgate_ref[...], preferred_element_type=F32))
    emb = jnp.dot(p_ref[...].astype(BF16), wproj_ref[...], preferred_element_type=F32)
    x3 = x2 + gate * emb
    o_ref[...] = _rms(x3, gfin_ref[...]) if last_layer else x3


def _final(x1, y, rw, p, w_gate, w_proj, g_ple, g_fin, last_layer, tm=256):
    m, d = x1.shape
    nb = m // tm
    row = lambda w: pl.BlockSpec((tm, w), lambda i: (i, 0))
    const = lambda a: pl.BlockSpec(a.shape, lambda i: (0, 0))
    g_ple, g_fin = g_ple.reshape(1, d), g_fin.reshape(1, d)
    return pl.pallas_call(
        functools.partial(_final_kernel, last_layer=last_layer),
        grid=(nb,),
        in_specs=[row(d), row(d), pl.BlockSpec((tm, d), lambda i: (nb + i, 0)), row(LANES), row(p.shape[1]),
                  const(w_gate), const(w_proj), const(g_ple), const(g_fin)],
        out_specs=row(d),
        out_shape=jax.ShapeDtypeStruct((m, d), F32),
        compiler_params=_params(("parallel",), 56),
        name="final",
    )(x1, y, y, rw, p, w_gate, w_proj, g_ple, g_fin)


def kernel(x, p, attn_norm, w_in, b_in, hg_lb_logits, hg_out_norm, w_hg_up, w_attn_up, w_out, ffn_norm,
           w_router_group, b_router_group, w_router_expert, b_router_expert, w_exp_gate, w_exp_up,
           w_exp_down, ple_norm, w_ple_gate, w_ple_proj, final_norm):
    batch, seq, d = x.shape
    n = batch * seq
    depth = w_in.shape[0]
    o_hf, o_at = 0, 4 * HG_WIDTH
    o_iq = o_at + 3 * AT_WIDTH
    o_ik = o_iq + IDX_HEADS * IDX_DIM
    o_iw = o_ik + IDX_DIM
    o_g = o_iw + IDX_HEADS
    lbs = jnp.cumsum(jax.nn.softmax(hg_lb_logits.astype(F32), axis=0), axis=0)
    xf = x.reshape(n, d)
    moe_tm = min(256, n // 2)
    for li in range(depth):
        wt, b = w_in[li].T, b_in[li]
        h = _rmsnorm(xf, attn_norm[li], BF16)
        zhg = _matmul(h, wt[o_hf:o_at].astype(BF16), b[o_hf:o_at], F32, name="proj_hgrn")
        qkv_scale = jnp.concatenate([jnp.full((AT_WIDTH,), AT_HEAD_DIM ** -0.5, F32), jnp.ones((2 * AT_WIDTH,), F32)])
        zat = _matmul(h, (wt[o_at:o_iq] * qkv_scale[:, None]).astype(BF16), b[o_at:o_iq] * qkv_scale, BF16,
                      name="proj_attn")
        dup = lambda a: jnp.repeat(a.reshape((IDX_HEADS, 1, IDX_DIM) + a.shape[1:]), 2, axis=1).reshape(
            (2 * IDX_HEADS * IDX_DIM,) + a.shape[1:])
        ziq = _matmul(h, dup(wt[o_iq:o_ik]).astype(BF16), dup(b[o_iq:o_ik]), F32, name="proj_iq")
        pad = LANES - IDX_DIM - IDX_HEADS
        zikw = _matmul(h, jnp.pad(wt[o_ik:o_g], ((0, pad), (0, 0))).astype(BF16),
                       jnp.pad(b[o_ik:o_g], (0, pad)), F32, name="proj_ikw")
        gates = _matmul(h, wt[o_g:].astype(BF16), b[o_g:], BF16, act="sigmoid", name="proj_gates")

        o_hg = _hgrn(zhg, lbs[li], hg_out_norm[li], batch, seq)
        ikt = zikw[:, :IDX_DIM].reshape(batch, seq, IDX_DIM).transpose(0, 2, 1)
        o_att = _dsa(ziq, ikt, zikw, zat, batch, seq)
        mixed = _merge(o_hg, w_hg_up[li].astype(BF16), o_att, w_attn_up[li].astype(BF16), gates)

        w_r = jnp.pad(jnp.concatenate([w_router_group[li], w_router_expert[li]], axis=1),
                      ((0, 0), (0, LANES - N_GROUPS - N_EXPERTS)))
        b_r = jnp.pad(jnp.concatenate([b_router_group[li], b_router_expert[li]]), (0, LANES - N_GROUPS - N_EXPERTS))
        w_r_hi = w_r.astype(BF16)
        w_r_lo = (w_r - w_r_hi.astype(F32)).astype(BF16)
        w_r = jnp.concatenate([w_r_hi, w_r_lo, w_r_hi], axis=0)
        x1, hn, route_e, route_w = _outproj(mixed, w_out[li].astype(BF16), xf, ffn_norm[li], w_r, b_r)
        tile_e, tile_v, src, dst = _dispatch_plan(route_e[:, :2], moe_tm)
        y = _experts(hn, tile_e, tile_v, src, dst, w_exp_gate[li], w_exp_up[li], w_exp_down[li], moe_tm)
        xf = _final(x1, y, route_w, p[li].reshape(n, -1), w_ple_gate[li].astype(BF16),
                    w_ple_proj[li].astype(BF16), ple_norm[li], final_norm, li == depth - 1)
    return xf.reshape(batch, seq, d)
```

```python
import functools

import numpy as np
import jax
import jax.numpy as jnp
from jax import lax
from jax.experimental import pallas as pl
from jax.experimental.pallas import tpu as pltpu

F32 = jnp.float32
BF16 = jnp.bfloat16
HIGHEST = lax.Precision.HIGHEST

HG_HEADS = 8
HG_DK = 128
HG_DV = 128
HG_WIDTH = HG_HEADS * HG_DK
HG_CHUNK = 64
AT_HEADS = 8
AT_HEAD_DIM = 128
AT_WIDTH = AT_HEADS * AT_HEAD_DIM
IDX_HEADS = 16
IDX_DIM = 64
TOPK_MAX = 256
N_GROUPS = 4
EXPERTS_PER_GROUP = 8
N_EXPERTS = N_GROUPS * EXPERTS_PER_GROUP
D_EXPERT = 512
RMS_EPS = 1e-6
LANES = 128
INT_MIN = -(2 ** 31)
NEG_BIG = -1e30

_NT = (((1,), (1,)), ((), ()))
_TN = (((0,), (0,)), ((), ()))


def _params(sem, vmem_mb=None):
    kw = dict(dimension_semantics=sem)
    if vmem_mb is not None:
        kw["vmem_limit_bytes"] = vmem_mb * 1024 * 1024
    return pltpu.CompilerParams(**kw)


def _sigmoid(x):
    return 1.0 / (1.0 + jnp.exp(-x))


def _rms(x, g):
    ms = jnp.mean(x * x, axis=-1, keepdims=True)
    return x * lax.rsqrt(ms + RMS_EPS) * g


def _rmsnorm_kernel(x_ref, g_ref, o_ref):
    o_ref[...] = _rms(x_ref[...], g_ref[...]).astype(o_ref.dtype)


def _rmsnorm(x, g, out_dtype, tm=512):
    m, d = x.shape
    return pl.pallas_call(
        _rmsnorm_kernel,
        grid=(m // tm,),
        in_specs=[pl.BlockSpec((tm, d), lambda i: (i, 0)), pl.BlockSpec((1, d), lambda i: (0, 0))],
        out_specs=pl.BlockSpec((tm, d), lambda i: (i, 0)),
        out_shape=jax.ShapeDtypeStruct((m, d), out_dtype),
        compiler_params=_params(("parallel",)),
        name="rmsnorm",
    )(x, g.reshape(1, d))


def _mm_kernel(a_ref, wt_ref, b_ref, o_ref, *, act):
    acc = lax.dot_general(a_ref[...], wt_ref[...], _NT, preferred_element_type=F32) + b_ref[...]
    if act == "sigmoid":
        acc = _sigmoid(acc)
    o_ref[...] = acc.astype(o_ref.dtype)


def _matmul(a, wt, b, out_dtype, act=None, tm=1024, tn=512, name="proj"):
    m, k = a.shape
    n = wt.shape[0]
    tm, tn = min(tm, m), min(tn, n)
    return pl.pallas_call(
        functools.partial(_mm_kernel, act=act),
        grid=(m // tm, n // tn),
        in_specs=[pl.BlockSpec((tm, k), lambda i, j: (i, 0)),
                  pl.BlockSpec((tn, k), lambda i, j: (j, 0)),
                  pl.BlockSpec((1, tn), lambda i, j: (0, j))],
        out_specs=pl.BlockSpec((tm, tn), lambda i, j: (i, j)),
        out_shape=jax.ShapeDtypeStruct((m, n), out_dtype),
        compiler_params=_params(("parallel", "arbitrary"), 48),
        name=name,
    )(a, wt, b.reshape(1, n))


HG_GROUP = 8
HG_ROWS = 512


def _hgrn_tables():
    c = HG_CHUNK
    levels = [c >> i for i in range(int(np.log2(c)))]
    t = np.arange(c)
    u = np.arange(c)
    rall = [(u[None, :] <= t[:, None]), (u[None, :] > t[:, None])]
    low, mask = [], []
    for p in levels:
        half = p // 2
        m = (t // p) * p + half
        lower = (t % p) >= half
        r = np.where(lower[:, None], (u[None, :] > m[:, None]) & (u[None, :] <= t[:, None]),
                     (u[None, :] > t[:, None]) & (u[None, :] <= m[:, None]))
        rall.append(r)
        low.append(np.broadcast_to(lower[:, None], (c, LANES)))
        mask.append(((t[:, None] // p) == (t[None, :] // p)) & lower[:, None] & ~lower[None, :])
    mask.append(np.eye(c, dtype=bool))
    rall = np.concatenate(rall, 0).astype(np.float32)
    return (np.tile(rall, (1, 3)), np.stack(low).astype(np.float32), np.stack(mask).astype(np.float32),
            len(levels))


def _hgrn_kernel(q_ref, f_ref, i_ref, g_ref, lb_ref, gn_ref, rall_ref, low_ref, mask_ref, o_ref, st_ref,
                 *, n_chunks, n_levels, heads):
    c = HG_CHUNK

    @pl.when(pl.program_id(2) == 0)
    def _():
        st_ref[...] = jnp.zeros_like(st_ref)

    lb = lb_ref[...]
    one_m_lb = 1.0 - lb
    gn = gn_ref[...]

    def body(ci, carry):
        rows = pl.ds(pl.multiple_of(ci * c, c), c)
        z = f_ref[rows, :]
        q = q_ref[rows, :]
        e = jnp.exp(-jnp.abs(z))
        r = 1.0 / (1.0 + e)
        pos = z >= 0
        sig = jnp.where(pos, r, e * r)
        nsig = jnp.where(pos, e * r, r)
        logf = jnp.log(lb + one_m_lb * sig)
        k = one_m_lb * nsig
        qs = q * _sigmoid(q)
        l1 = logf.astype(BF16)
        r1 = logf - l1.astype(F32)
        l2 = r1.astype(BF16)
        l3 = (r1 - l2.astype(F32)).astype(BF16)
        ex = jnp.exp(jnp.dot(rall_ref[...], jnp.concatenate([l1, l2, l3], axis=0), preferred_element_type=F32))
        vb = i_ref[rows, :].astype(BF16)
        gate = g_ref[rows, :]
        gate = gate * _sigmoid(gate)
        for g in range(heads):
            sl = slice(g * HG_DK, (g + 1) * HG_DK)
            qg, kg, vg = qs[:, sl], k[:, sl], vb[:, sl]
            eb = ex[0:c, sl]
            a = mask_ref[n_levels] * lax.dot_general(qg.astype(BF16), kg.astype(BF16), _NT,
                                                     preferred_element_type=F32)
            for li in range(n_levels):
                w = (jnp.where(low_ref[li] != 0.0, qg, kg) * ex[(2 + li) * c:(3 + li) * c, sl]).astype(BF16)
                a = a + mask_ref[li] * lax.dot_general(w, w, _NT, preferred_element_type=F32)
            st = st_ref[g]
            o = (jnp.dot(a.astype(BF16), vg, preferred_element_type=F32)
                 + lax.dot_general((qg * eb).astype(BF16), st.astype(BF16), _NT, preferred_element_type=F32))
            st_ref[g] = st * eb[c - 1:c, :] + lax.dot_general(vg, (kg * ex[c:2 * c, sl]).astype(BF16), _TN,
                                                              preferred_element_type=F32)
            o_ref[rows, sl] = (_rms(o, gn) * gate[:, sl]).astype(o_ref.dtype)
        return carry

    lax.fori_loop(0, n_chunks, body, 0, unroll=2)


def _hgrn(zhg, lb, gn, batch, seq):
    rall, low, mask, n_levels = _hgrn_tables()
    hb = HG_HEADS // HG_GROUP
    gw = HG_GROUP * HG_DK
    tc = min(HG_ROWS, seq)
    nt = seq // tc
    col = lambda off: pl.BlockSpec((tc, gw), lambda b, hh, tt: (b * nt + tt, off + hh))
    full = lambda a: pl.BlockSpec(a.shape, lambda b, hh, tt: (0,) * a.ndim)
    return pl.pallas_call(
        functools.partial(_hgrn_kernel, n_chunks=tc // HG_CHUNK, n_levels=n_levels, heads=HG_GROUP),
        grid=(batch, hb, nt),
        in_specs=[col(0), col(hb), col(2 * hb), col(3 * hb),
                  pl.BlockSpec((1, gw), lambda b, hh, tt: (0, hh)),
                  pl.BlockSpec((1, HG_DV), lambda b, hh, tt: (0, 0)),
                  full(rall), full(low), full(mask)],
        out_specs=pl.BlockSpec((tc, gw), lambda b, hh, tt: (b * nt + tt, hh)),
        out_shape=jax.ShapeDtypeStruct((batch * seq, HG_HEADS * HG_DV), BF16),
        scratch_shapes=[pltpu.VMEM((HG_GROUP, HG_DV, HG_DK), F32)],
        compiler_params=_params(("parallel", "parallel", "arbitrary"), 48),
        name="hgrn2",
    )(zhg, zhg, zhg, zhg, lb.reshape(1, HG_WIDTH), gn.reshape(1, HG_DV),
      jnp.asarray(rall, BF16), jnp.asarray(low), jnp.asarray(mask))


DSA_TQ = 256
DSA_ROW_GROUPS = 4


def _alibi_tables(seq):
    assert 8 % AT_HEADS == 0 and seq <= 64 * 256
    pos = jnp.arange(seq, dtype=jnp.int32)
    hi, lo = (pos // 64).astype(F32), (pos % 64).astype(F32)
    one = jnp.ones((seq,), F32)
    kext = jnp.stack([one, one, hi, lo], axis=1)
    kext = jnp.pad(kext, ((0, 0), (0, AT_HEAD_DIM - 4))).astype(BF16)
    qext = []
    for h in range(AT_HEADS):
        slope = 2.0 ** (-8.0 * (h + 1) / AT_HEADS)
        cols = jnp.stack([-slope * 64.0 * hi, -slope * lo, slope * 64.0 * one, slope * one], axis=1)
        qext.append(jnp.pad(cols, ((0, 0), (0, AT_HEAD_DIM - 4))))
    return jnp.concatenate(qext, axis=1).astype(BF16), kext


def _flip_key(x):
    return x ^ ((x >> 31) & 0x7FFFFFFF)


def _kth_largest_key(key_ref, t0, *, tq, width, ksel):
    rg = tq // DSA_ROW_GROUPS
    k_f = float(ksel)
    int_max = 2 ** 31 - 1
    to_f32 = lambda key: pltpu.bitcast(_flip_key(key), F32)
    to_key = lambda val: _flip_key(pltpu.bitcast(val, jnp.int32))

    few, state = [], []
    for gi in range(DSA_ROW_GROUPS):
        keys = key_ref[gi * rg:(gi + 1) * rg, :]
        row = t0 + gi * rg + lax.broadcasted_iota(jnp.int32, (rg, 1), 0)
        col = lax.broadcasted_iota(jnp.int32, (rg, width), 1)
        kmax = jnp.max(keys, axis=1, keepdims=True)
        kmin = jnp.min(jnp.where(col <= row, keys, int_max), axis=1, keepdims=True)
        few.append(row + 1 <= ksel)
        hi = jnp.where(kmax == int_max, int_max, kmax + 1)
        state.append((to_f32(kmin), to_f32(hi), jnp.where(few[gi], 1.0, 0.0)))

    def active_of(st):
        return functools.reduce(jnp.maximum, [jnp.max(jnp.where(done == 0.0, 1, 0)) for _, _, done in st])

    def make_body(key_midpoint):
        def body(carry):
            it, _, st = carry
            out = []
            for gi, (lo, hi, done) in enumerate(st):
                if key_midpoint:
                    lo_k = to_key(lo)
                    mid = to_f32(lo_k + lax.shift_right_logical(to_key(hi) - lo_k, 1))
                else:
                    mid = 0.5 * lo + 0.5 * hi
                keys = key_ref[gi * rg:(gi + 1) * rg, :]
                cnt = jnp.sum((keys >= to_key(mid)).astype(jnp.int32), axis=1, keepdims=True).astype(F32)
                ge = cnt >= k_f
                live = done == 0.0
                exhausted = jnp.logical_or(mid <= lo, mid >= hi)
                out.append((jnp.where(live, jnp.where(ge, mid, lo), lo),
                            jnp.where(live, jnp.where(ge, hi, mid), hi),
                            jnp.where(jnp.logical_or(cnt == k_f, exhausted), 1.0, done)))
            out = tuple(out)
            return it + 1, active_of(out), out

        return body

    def run(limit, body, st):
        cond = lambda carry: jnp.logical_and(carry[1] > 0, carry[0] < limit)
        return lax.while_loop(cond, body, (jnp.int32(0), active_of(st), st))[2]

    state = run(40, make_body(False), tuple(state))
    state = run(34, make_body(True), state)
    return jnp.concatenate([jnp.where(f, INT_MIN, to_key(lo)) for f, (lo, _, _) in zip(few, state)], axis=0)


def _dsa_kernel(iq_ref, ikt_ref, iw_ref, q_ref, qext_ref, k_ref, kext_ref, v_ref, o_ref, acc_ref, key_ref, bias_ref,
                *, tq, seq, ksel, col_step):
    first_row = pl.program_id(1) * tq
    for g in range(seq // col_step):
        @pl.when(first_row // col_step == g)
        def _(width=(g + 1) * col_step):
            _dsa_body(iq_ref, ikt_ref, iw_ref, q_ref, qext_ref, k_ref, kext_ref, v_ref, o_ref, acc_ref, key_ref,
                      bias_ref, tq=tq, ksel=ksel, width=width, first_row=first_row)


def _dsa_body(iq_ref, ikt_ref, iw_ref, q_ref, qext_ref, k_ref, kext_ref, v_ref, o_ref, acc_ref, key_ref, bias_ref,
              *, tq, ksel, width, first_row):
    seq = width
    t0 = first_row
    acc_ref, key_ref, bias_ref = acc_ref.at[:, :width], key_ref.at[:, :width], bias_ref.at[:, :width]
    ikt = ikt_ref[0, :, :width]
    ik_hi = ikt.astype(BF16)
    ik_lo = (ikt - ik_hi.astype(F32)).astype(BF16)
    rhs = jnp.concatenate([ik_hi, ik_hi, ik_lo, ik_lo], axis=0)
    lane = lax.broadcasted_iota(jnp.int32, (tq, LANES), 1)
    w_all = iw_ref[...] * (IDX_HEADS ** -0.5 * IDX_DIM ** -0.5)
    for h in range(IDX_HEADS):
        x = iq_ref[:, h * LANES:(h + 1) * LANES]
        x_hi = x.astype(BF16).astype(F32)
        u = jnp.where(lane < IDX_DIM, x_hi, x - x_hi).astype(BF16)
        s = jnp.dot(jnp.concatenate([u, u], axis=1), rhs, preferred_element_type=F32)
        term = jnp.maximum(s, 0.0) * w_all[:, IDX_DIM + h:IDX_DIM + h + 1]
        if h == 0:
            acc_ref[...] = term
        else:
            acc_ref[...] += term

    col = lax.broadcasted_iota(jnp.int32, (tq, seq), 1)
    row = t0 + lax.broadcasted_iota(jnp.int32, (tq, seq), 0)
    vis = col <= row
    bits = pltpu.bitcast(acc_ref[...], jnp.int32)
    key = bits ^ ((bits >> 31) & 0x7FFFFFFF)
    key_ref[...] = jnp.where(vis, key, INT_MIN)

    thr = _kth_largest_key(key_ref, t0, tq=tq, width=width, ksel=ksel)
    bias_ref[...] = jnp.where(vis, jnp.where(key_ref[...] >= thr, 0.0, NEG_BIG), NEG_BIG)

    kext = kext_ref[:width, :]
    for h in range(AT_HEADS):
        hs = slice(h * AT_HEAD_DIM, (h + 1) * AT_HEAD_DIM)
        qa = jnp.concatenate([q_ref[:, hs], qext_ref[:, hs]], axis=1)
        ka = jnp.concatenate([k_ref[:width, hs], kext], axis=1)
        logit = lax.dot_general(qa, ka, _NT, preferred_element_type=F32) + bias_ref[...]
        m = jnp.max(logit, axis=1, keepdims=True)
        p = jnp.exp(logit - m)
        den = jnp.sum(p, axis=1, keepdims=True)
        oh = jnp.dot(p.astype(BF16), v_ref[:width, hs], preferred_element_type=F32) / den
        o_ref[:, hs] = oh.astype(o_ref.dtype)


def _dsa(iq_dup, ikt, ikw, zat, batch, seq, tq=DSA_TQ):
    nq = seq // tq
    ksel = min(TOPK_MAX, seq // 4)
    col_step = max(seq // 4, tq)
    qext, kext = _alibi_tables(seq)
    return pl.pallas_call(
        functools.partial(_dsa_kernel, tq=tq, seq=seq, ksel=ksel, col_step=col_step),
        grid=(batch, nq),
        in_specs=[pl.BlockSpec((tq, IDX_HEADS * LANES), lambda b, i: (b * nq + i, 0)),
                  pl.BlockSpec((1, IDX_DIM, seq), lambda b, i: (b, 0, 0)),
                  pl.BlockSpec((tq, LANES), lambda b, i: (b * nq + i, 0)),
                  pl.BlockSpec((tq, AT_WIDTH), lambda b, i: (b * nq + i, 0)),
                  pl.BlockSpec((tq, AT_WIDTH), lambda b, i: (i, 0)),
                  pl.BlockSpec((seq, AT_WIDTH), lambda b, i: (b, 1)),
                  pl.BlockSpec((seq, AT_HEAD_DIM), lambda b, i: (0, 0)),
                  pl.BlockSpec((seq, AT_WIDTH), lambda b, i: (b, 2))],
        out_specs=pl.BlockSpec((tq, AT_WIDTH), lambda b, i: (b * nq + i, 0)),
        out_shape=jax.ShapeDtypeStruct((batch * seq, AT_WIDTH), BF16),
        scratch_shapes=[pltpu.VMEM((tq, seq), F32), pltpu.VMEM((tq, seq), jnp.int32),
                        pltpu.VMEM((tq, seq), F32)],
        compiler_params=_params(("parallel", "arbitrary"), 48),
        name="dsa",
    )(iq_dup, ikt, ikw, zat, qext, zat, kext, zat)


def _merge_kernel(a1_ref, w1_ref, a2_ref, w2_ref, g1_ref, g2_ref, o_ref):
    y1 = jnp.dot(a1_ref[...], w1_ref[...], preferred_element_type=F32)
    y2 = jnp.dot(a2_ref[...], w2_ref[...], preferred_element_type=F32)
    o_ref[...] = (g1_ref[...].astype(F32) * y1 + g2_ref[...].astype(F32) * y2).astype(o_ref.dtype)


def _merge(o_hg, w_hg, o_at, w_at, gates, tm=1024, tn=512):
    m, k1 = o_hg.shape
    k2 = o_at.shape[1]
    n = w_hg.shape[1]
    tm = min(tm, m)
    nb = n // tn
    return pl.pallas_call(
        _merge_kernel,
        grid=(m // tm, nb),
        in_specs=[pl.BlockSpec((tm, k1), lambda i, j: (i, 0)), pl.BlockSpec((k1, tn), lambda i, j: (0, j)),
                  pl.BlockSpec((tm, k2), lambda i, j: (i, 0)), pl.BlockSpec((k2, tn), lambda i, j: (0, j)),
                  pl.BlockSpec((tm, tn), lambda i, j: (i, j)), pl.BlockSpec((tm, tn), lambda i, j: (i, nb + j))],
        out_specs=pl.BlockSpec((tm, tn), lambda i, j: (i, j)),
        out_shape=jax.ShapeDtypeStruct((m, n), BF16),
        compiler_params=_params(("parallel", "arbitrary"), 48),
        name="merge",
    )(o_hg, w_hg, o_at, w_at, gates, gates)


def _outproj_kernel(m_ref, w_ref, x_ref, gn_ref, wr_ref, br_ref, x1_ref, hn_ref, ri_ref, rw_ref):
    x1 = x_ref[...] + jnp.dot(m_ref[...], w_ref[...], preferred_element_type=F32)
    x1_ref[...] = x1
    hn = _rms(x1, gn_ref[...])
    hn_ref[...] = hn
    h_hi = hn.astype(BF16)
    h_lo = (hn - h_hi.astype(F32)).astype(BF16)
    logit = jnp.dot(jnp.concatenate([h_hi, h_hi, h_lo], axis=1), wr_ref[...],
                    preferred_element_type=F32) + br_ref[...]
    lane = lax.broadcasted_iota(jnp.int32, logit.shape, 1)
    lane_f = lane.astype(F32)
    is_g = lane < N_GROUPS
    lg = jnp.where(is_g, logit, -jnp.inf)
    mg = jnp.max(lg, axis=1, keepdims=True)
    pg_top = 1.0 / jnp.sum(jnp.exp(lg - mg), axis=1, keepdims=True)
    gid = jnp.min(jnp.where(lg == mg, lane_f, float(LANES)), axis=1, keepdims=True).astype(jnp.int32)
    in_grp = ((lane - N_GROUPS) >> 3) == gid
    le = jnp.where(in_grp, logit, -jnp.inf)
    m1 = jnp.max(le, axis=1, keepdims=True)
    i1 = jnp.min(jnp.where(le == m1, lane_f, float(LANES)), axis=1, keepdims=True)
    le2 = jnp.where(lane_f == i1, -jnp.inf, le)
    m2 = jnp.max(le2, axis=1, keepdims=True)
    i2 = jnp.min(jnp.where(le2 == m2, lane_f, float(LANES)), axis=1, keepdims=True)
    r = jnp.exp(m2 - m1)
    w1 = pg_top / (1.0 + r)
    w2 = pg_top * r / (1.0 + r)
    e1 = i1.astype(jnp.int32) - N_GROUPS
    e2 = i2.astype(jnp.int32) - N_GROUPS
    ri_ref[...] = jnp.where(lane == 0, e1, jnp.where(lane == 1, e2, 0))
    rw_ref[...] = jnp.where(lane == 0, w1, jnp.where(lane == 1, w2, 0.0))


def _outproj(mixed, w_out, x, gn, w_r, b_r, tm=256):
    m, d = x.shape
    row = lambda w: pl.BlockSpec((tm, w), lambda i: (i, 0))
    const = lambda a: pl.BlockSpec(a.shape, lambda i: (0, 0))
    gn = gn.reshape(1, d)
    b_r = b_r.reshape(1, LANES)
    return pl.pallas_call(
        _outproj_kernel,
        grid=(m // tm,),
        in_specs=[row(mixed.shape[1]), const(w_out), row(d), const(gn), const(w_r), const(b_r)],
        out_specs=[row(d), row(d), row(LANES), row(LANES)],
        out_shape=[jax.ShapeDtypeStruct((m, d), F32), jax.ShapeDtypeStruct((m, d), F32),
                   jax.ShapeDtypeStruct((m, LANES), jnp.int32), jax.ShapeDtypeStruct((m, LANES), F32)],
        compiler_params=_params(("parallel",), 56),
        name="outproj_router",
    )(mixed, w_out, x, gn, w_r, b_r)


def _expert_kernel(te_ref, tv_ref, src_ref, dst_ref, hn_hbm, wg_ref, wu_ref, wd_ref, y_hbm,
                   xbuf, ybuf, xb_ref, wgb, wub, wdb, gsem, ssem, *, tm):
    t = pl.program_id(0)
    slot = lax.rem(t, 2)
    other = 1 - slot
    valid = tv_ref[t] > 0
    prev_valid = jnp.logical_and(t >= 1, tv_ref[jnp.maximum(t - 1, 0)] > 0)

    def gather_copy(tile, r, s):
        tok = src_ref[tile * tm + r]
        return pltpu.make_async_copy(hn_hbm.at[pl.ds(tok, 1), :], xbuf.at[s, pl.ds(r, 1), :], gsem.at[s])

    def scatter_copy(tile, r, s):
        dst = dst_ref[(tile + 1) * tm + r]
        return pltpu.make_async_copy(ybuf.at[s, pl.ds(r, 1), :], y_hbm.at[pl.ds(dst, 1), :], ssem.at[s])

    def start_rows(copy_fn, tile, s):
        def body(r, carry):
            copy_fn(tile, r, s).start()
            return carry

        lax.fori_loop(0, tm, body, 0, unroll=8)

    def gather_wait(s):
        pltpu.make_async_copy(hn_hbm.at[pl.ds(0, tm), :], xbuf.at[s], gsem.at[s]).wait()

    def scatter_wait(s):
        pltpu.make_async_copy(ybuf.at[s], y_hbm.at[pl.ds(0, tm), :], ssem.at[s]).wait()

    @pl.when(t == 0)
    def _():
        start_rows(gather_copy, 0, 0)
        ybuf[1] = jnp.zeros(ybuf.shape[1:], F32)

    @pl.when(prev_valid)
    def _():
        scatter_wait(slot)

    @pl.when(valid)
    def _():
        gather_wait(slot)

        @pl.when(jnp.logical_or(t == 0, te_ref[t] != te_ref[jnp.maximum(t - 1, 0)]))
        def _():
            wgb[...] = wg_ref[0].astype(BF16)
            wub[...] = wu_ref[0].astype(BF16)
            wdb[...] = wd_ref[0].astype(BF16)

        xb_ref[...] = xbuf[slot].astype(BF16)
        for r in range(tm):
            gather_copy(t + 1, r, other).start()
        for r in range(tm):
            scatter_copy(t - 1, r, other).start()
        xb = xb_ref[...]
        g = jnp.dot(xb, wgb[...], preferred_element_type=F32)
        u = jnp.dot(xb, wub[...], preferred_element_type=F32)
        hid = (g * _sigmoid(g) * u).astype(BF16)
        ybuf[slot] = jnp.dot(hid, wdb[...], preferred_element_type=F32)

    @pl.when(jnp.logical_and(jnp.logical_not(valid), prev_valid))
    def _():
        gather_wait(slot)
        start_rows(scatter_copy, t - 1, other)
        scatter_wait(other)


def _experts(hn, tile_expert, tile_valid, src, dst, w_g, w_u, w_d, tm):
    n_tok, d = hn.shape
    n_steps = tile_expert.shape[0]
    wspec = lambda a: pl.BlockSpec((1,) + a.shape[1:], lambda t, te, tv, sr, ds: (te[t], 0, 0))
    return pl.pallas_call(
        functools.partial(_expert_kernel, tm=tm),
        grid_spec=pltpu.PrefetchScalarGridSpec(
            num_scalar_prefetch=4,
            grid=(n_steps,),
            in_specs=[pl.BlockSpec(memory_space=pl.ANY), wspec(w_g), wspec(w_u), wspec(w_d)],
            out_specs=pl.BlockSpec(memory_space=pl.ANY),
            scratch_shapes=[pltpu.VMEM((2, tm, d), F32), pltpu.VMEM((2, tm, d), F32), pltpu.VMEM((tm, d), BF16),
                            pltpu.VMEM(w_g.shape[1:], BF16), pltpu.VMEM(w_u.shape[1:], BF16),
                            pltpu.VMEM(w_d.shape[1:], BF16),
                            pltpu.SemaphoreType.DMA((2,)), pltpu.SemaphoreType.DMA((2,))]),
        out_shape=jax.ShapeDtypeStruct((2 * n_tok + tm, d), F32),
        compiler_params=_params(("arbitrary",), 56),
        name="experts",
    )(tile_expert, tile_valid, src, dst, hn, w_g, w_u, w_d)


def _dispatch_plan(route_e, tm):
    n_tok = route_e.shape[0]
    n_pairs = 2 * n_tok
    n_tiles = n_pairs // tm + N_EXPERTS + 1
    eflat = route_e.T.reshape(-1)
    order = jnp.argsort(eflat, stable=True).astype(jnp.int32)
    counts = jnp.sum(eflat[:, None] == jnp.arange(N_EXPERTS, dtype=jnp.int32)[None, :], axis=0, dtype=jnp.int32)
    tiles_e = (counts + tm - 1) // tm
    tile_end = jnp.cumsum(tiles_e)
    tile_off = tile_end - tiles_e
    start = jnp.cumsum(counts) - counts
    tid = jnp.arange(n_tiles, dtype=jnp.int32)
    te = jnp.minimum(jnp.sum(tile_end[None, :] <= tid[:, None], axis=1), N_EXPERTS - 1).astype(jnp.int32)
    tv = jnp.clip(counts[te] - (tid - tile_off[te]) * tm, 0, tm)
    used = tid < tile_end[-1]
    last_e = te[jnp.maximum(tile_end[-1] - 1, 0)]
    te = jnp.where(used, te, last_e).astype(jnp.int32)
    tv = jnp.where(used, tv, 0).astype(jnp.int32)
    r = jnp.arange(tm, dtype=jnp.int32)[None, :]
    src = (start[te] + (tid - tile_off[te]) * tm)[:, None] + r
    dst = jnp.where(r < tv[:, None], order[jnp.clip(src, 0, n_pairs - 1)], n_pairs + r).reshape(-1).astype(jnp.int32)
    spare = n_pairs + jnp.arange(tm, dtype=jnp.int32)
    return te, tv, dst % n_tok, jnp.concatenate([spare, dst])


def _final_kernel(x1_ref, y0_ref, y1_ref, rw_ref, p_ref, wgate_ref, wproj_ref, gple_ref, gfin_ref, o_ref,
                  *, last_layer):
    rw = rw_ref[...]
    x2 = x1_ref[...] + rw[:, 0:1] * y0_ref[...] + rw[:, 1:2] * y1_ref[...]
    hn = _rms(x2, gple_ref[...]).astype(BF16)
    gate = _sigmoid(jnp.dot(hn, wgate_ref[...], preferred_element_type=F32))
    emb = jnp.dot(p_ref[...].astype(BF16), wproj_ref[...], preferred_element_type=F32)
    x3 = x2 + gate * emb
    o_ref[...] = _rms(x3, gfin_ref[...]) if last_layer else x3


def _final(x1, y, rw, p, w_gate, w_proj, g_ple, g_fin, last_layer, tm=256):
    m, d = x1.shape
    nb = m // tm
    row = lambda w: pl.BlockSpec((tm, w), lambda i: (i, 0))
    const = lambda a: pl.BlockSpec(a.shape, lambda i: (0, 0))
    g_ple, g_fin = g_ple.reshape(1, d), g_fin.reshape(1, d)
    return pl.pallas_call(
        functools.partial(_final_kernel, last_layer=last_layer),
        grid=(nb,),
        in_specs=[row(d), row(d), pl.BlockSpec((tm, d), lambda i: (nb + i, 0)), row(LANES), row(p.shape[1]),
                  const(w_gate), const(w_proj), const(g_ple), const(g_fin)],
        out_specs=row(d),
        out_shape=jax.ShapeDtypeStruct((m, d), F32),
        compiler_params=_params(("parallel",), 56),
        name="final",
    )(x1, y, y, rw, p, w_gate, w_proj, g_ple, g_fin)


def kernel(x, p, attn_norm, w_in, b_in, hg_lb_logits, hg_out_norm, w_hg_up, w_attn_up, w_out, ffn_norm,
           w_router_group, b_router_group, w_router_expert, b_router_expert, w_exp_gate, w_exp_up,
           w_exp_down, ple_norm, w_ple_gate, w_ple_proj, final_norm):
    batch, seq, d = x.shape
    n = batch * seq
    depth = w_in.shape[0]
    o_hf, o_at = 0, 4 * HG_WIDTH
    o_iq = o_at + 3 * AT_WIDTH
    o_ik = o_iq + IDX_HEADS * IDX_DIM
    o_iw = o_ik + IDX_DIM
    o_g = o_iw + IDX_HEADS
    lbs = jnp.cumsum(jax.nn.softmax(hg_lb_logits.astype(F32), axis=0), axis=0)
    xf = x.reshape(n, d)
    moe_tm = min(256, n // 2)
    for li in range(depth):
        wt, b = w_in[li].T, b_in[li]
        h = _rmsnorm(xf, attn_norm[li], BF16)
        zhg = _matmul(h, wt[o_hf:o_at].astype(BF16), b[o_hf:o_at], F32, name="proj_hgrn")
        qkv_scale = jnp.concatenate([jnp.full((AT_WIDTH,), AT_HEAD_DIM ** -0.5, F32), jnp.ones((2 * AT_WIDTH,), F32)])
        zat = _matmul(h, (wt[o_at:o_iq] * qkv_scale[:, None]).astype(BF16), b[o_at:o_iq] * qkv_scale, BF16,
                      name="proj_attn")
        dup = lambda a: jnp.repeat(a.reshape((IDX_HEADS, 1, IDX_DIM) + a.shape[1:]), 2, axis=1).reshape(
            (2 * IDX_HEADS * IDX_DIM,) + a.shape[1:])
        ziq = _matmul(h, dup(wt[o_iq:o_ik]).astype(BF16), dup(b[o_iq:o_ik]), F32, name="proj_iq")
        pad = LANES - IDX_DIM - IDX_HEADS
        zikw = _matmul(h, jnp.pad(wt[o_ik:o_g], ((0, pad), (0, 0))).astype(BF16),
                       jnp.pad(b[o_ik:o_g], (0, pad)), F32, name="proj_ikw")
        gates = _matmul(h, wt[o_g:].astype(BF16), b[o_g:], BF16, act="sigmoid", name="proj_gates")

        o_hg = _hgrn(zhg, lbs[li], hg_out_norm[li], batch, seq)
        ikt = zikw[:, :IDX_DIM].reshape(batch, seq, IDX_DIM).transpose(0, 2, 1)
        o_att = _dsa(ziq, ikt, zikw, zat, batch, seq)
        mixed = _merge(o_hg, w_hg_up[li].astype(BF16), o_att, w_attn_up[li].astype(BF16), gates)

        w_r = jnp.pad(jnp.concatenate([w_router_group[li], w_router_expert[li]], axis=1),
                      ((0, 0), (0, LANES - N_GROUPS - N_EXPERTS)))
        b_r = jnp.pad(jnp.concatenate([b_router_group[li], b_router_expert[li]]), (0, LANES - N_GROUPS - N_EXPERTS))
        w_r_hi = w_r.astype(BF16)
        w_r_lo = (w_r - w_r_hi.astype(F32)).astype(BF16)
        w_r = jnp.concatenate([w_r_hi, w_r_lo, w_r_hi], axis=0)
        x1, hn, route_e, route_w = _outproj(mixed, w_out[li].astype(BF16), xf, ffn_norm[li], w_r, b_r)
        tile_e, tile_v, src, dst = _dispatch_plan(route_e[:, :2], moe_tm)
        y = _experts(hn, tile_e, tile_v, src, dst, w_exp_gate[li], w_exp_up[li], w_exp_down[li], moe_tm)
        xf = _final(x1, y, route_w, p[li].reshape(n, -1), w_ple_gate[li].astype(BF16),
                    w_ple_proj[li].astype(BF16), ple_norm[li], final_norm, li == depth - 1)
    return xf.reshape(batch, seq, d)
```

```python
import functools

import numpy as np
import jax
import jax.numpy as jnp
from jax import lax
from jax.experimental import pallas as pl
from jax.experimental.pallas import tpu as pltpu

F32 = jnp.float32
BF16 = jnp.bfloat16
HIGHEST = lax.Precision.HIGHEST

HG_HEADS = 8
HG_DK = 128
HG_DV = 128
HG_WIDTH = HG_HEADS * HG_DK
HG_CHUNK = 64
AT_HEADS = 8
AT_HEAD_DIM = 128
AT_WIDTH = AT_HEADS * AT_HEAD_DIM
IDX_HEADS = 16
IDX_DIM = 64
TOPK_MAX = 256
N_GROUPS = 4
EXPERTS_PER_GROUP = 8
N_EXPERTS = N_GROUPS * EXPERTS_PER_GROUP
D_EXPERT = 512
RMS_EPS = 1e-6
LANES = 128
INT_MIN = -(2 ** 31)
NEG_BIG = -1e30

_NT = (((1,), (1,)), ((), ()))
_TN = (((0,), (0,)), ((), ()))


def _params(sem, vmem_mb=None):
    kw = dict(dimension_semantics=sem)
    if vmem_mb is not None:
        kw["vmem_limit_bytes"] = vmem_mb * 1024 * 1024
    return pltpu.CompilerParams(**kw)


def _sigmoid(x):
    return 1.0 / (1.0 + jnp.exp(-x))


def _pack_bf16_halves(x):
    c = x.shape[1] // 2
    xr = x.astype(BF16).astype(F32)
    hi = pltpu.bitcast(xr[:, :c], jnp.int32)
    lo = lax.shift_right_logical(pltpu.bitcast(xr[:, c:], jnp.int32), 16)
    return hi | lo


def _unpack_bf16_halves(w):
    hi = pltpu.bitcast(w & -65536, F32)
    lo = pltpu.bitcast(lax.shift_left(w, 16), F32)
    return jnp.concatenate([hi, lo], axis=1)


def _rms(x, g):
    ms = jnp.mean(x * x, axis=-1, keepdims=True)
    return x * lax.rsqrt(ms + RMS_EPS) * g


def _rmsnorm_kernel(x_ref, g_ref, o_ref):
    o_ref[...] = _rms(x_ref[...], g_ref[...]).astype(o_ref.dtype)


def _rmsnorm(x, g, out_dtype, tm=512):
    m, d = x.shape
    return pl.pallas_call(
        _rmsnorm_kernel,
        grid=(m // tm,),
        in_specs=[pl.BlockSpec((tm, d), lambda i: (i, 0)), pl.BlockSpec((1, d), lambda i: (0, 0))],
        out_specs=pl.BlockSpec((tm, d), lambda i: (i, 0)),
        out_shape=jax.ShapeDtypeStruct((m, d), out_dtype),
        compiler_params=_params(("parallel",)),
        name="rmsnorm",
    )(x, g.reshape(1, d))


def _mm_kernel(a_ref, wt_ref, b_ref, o_ref, *, act):
    acc = lax.dot_general(a_ref[...], wt_ref[...], _NT, preferred_element_type=F32) + b_ref[...]
    if act == "sigmoid":
        acc = _sigmoid(acc)
    o_ref[...] = acc.astype(o_ref.dtype)


def _matmul(a, wt, b, out_dtype, act=None, tm=1024, tn=512, name="proj"):
    m, k = a.shape
    n = wt.shape[0]
    tm, tn = min(tm, m), min(tn, n)
    return pl.pallas_call(
        functools.partial(_mm_kernel, act=act),
        grid=(m // tm, n // tn),
        in_specs=[pl.BlockSpec((tm, k), lambda i, j: (i, 0)),
                  pl.BlockSpec((tn, k), lambda i, j: (j, 0)),
                  pl.BlockSpec((1, tn), lambda i, j: (0, j))],
        out_specs=pl.BlockSpec((tm, tn), lambda i, j: (i, j)),
        out_shape=jax.ShapeDtypeStruct((m, n), out_dtype),
        compiler_params=_params(("parallel", "arbitrary"), 48),
        name=name,
    )(a, wt, b.reshape(1, n))


HG_GROUP = 8
HG_ROWS = 512


def _hgrn_tables():
    c = HG_CHUNK
    levels = [c >> i for i in range(int(np.log2(c)))]
    t = np.arange(c)
    u = np.arange(c)
    rall = [(u[None, :] <= t[:, None]), (u[None, :] > t[:, None])]
    low, mask = [], []
    for p in levels:
        half = p // 2
        m = (t // p) * p + half
        lower = (t % p) >= half
        r = np.where(lower[:, None], (u[None, :] > m[:, None]) & (u[None, :] <= t[:, None]),
                     (u[None, :] > t[:, None]) & (u[None, :] <= m[:, None]))
        rall.append(r)
        low.append(np.broadcast_to(lower[:, None], (c, LANES)))
        mask.append(((t[:, None] // p) == (t[None, :] // p)) & lower[:, None] & ~lower[None, :])
    mask.append(np.eye(c, dtype=bool))
    rall = np.concatenate(rall, 0).astype(np.float32)
    return (np.tile(rall, (1, 3)), np.stack(low).astype(np.float32), np.stack(mask).astype(np.float32),
            len(levels))


def _hgrn_kernel(q_ref, f_ref, i_ref, g_ref, lb_ref, gn_ref, rall_ref, low_ref, mask_ref, o_ref, st_ref,
                 *, n_chunks, n_levels, heads):
    c = HG_CHUNK

    @pl.when(pl.program_id(2) == 0)
    def _():
        st_ref[...] = jnp.zeros_like(st_ref)

    lb = lb_ref[...]
    one_m_lb = 1.0 - lb
    gn = gn_ref[...]

    def body(ci, carry):
        rows = pl.ds(pl.multiple_of(ci * c, c), c)
        z = f_ref[rows, :]
        q = q_ref[rows, :]
        e = jnp.exp(-jnp.abs(z))
        r = 1.0 / (1.0 + e)
        pos = z >= 0
        sig = jnp.where(pos, r, e * r)
        nsig = jnp.where(pos, e * r, r)
        logf = jnp.log(lb + one_m_lb * sig)
        k = one_m_lb * nsig
        qs = q * _sigmoid(q)
        l1 = logf.astype(BF16)
        r1 = logf - l1.astype(F32)
        l2 = r1.astype(BF16)
        l3 = (r1 - l2.astype(F32)).astype(BF16)
        ex = jnp.exp(jnp.dot(rall_ref[...], jnp.concatenate([l1, l2, l3], axis=0), preferred_element_type=F32))
        vb = i_ref[rows, :].astype(BF16)
        gate = g_ref[rows, :]
        gate = gate * _sigmoid(gate)
        for g in range(heads):
            sl = slice(g * HG_DK, (g + 1) * HG_DK)
            qg, kg, vg = qs[:, sl], k[:, sl], vb[:, sl]
            eb = ex[0:c, sl]
            a = mask_ref[n_levels] * lax.dot_general(qg.astype(BF16), kg.astype(BF16), _NT,
                                                     preferred_element_type=F32)
            for li in range(n_levels):
                w = (jnp.where(low_ref[li] != 0.0, qg, kg) * ex[(2 + li) * c:(3 + li) * c, sl]).astype(BF16)
                a = a + mask_ref[li] * lax.dot_general(w, w, _NT, preferred_element_type=F32)
            st = st_ref[g]
            o = (jnp.dot(a.astype(BF16), vg, preferred_element_type=F32)
                 + lax.dot_general((qg * eb).astype(BF16), st.astype(BF16), _NT, preferred_element_type=F32))
            st_ref[g] = st * eb[c - 1:c, :] + lax.dot_general(vg, (kg * ex[c:2 * c, sl]).astype(BF16), _TN,
                                                              preferred_element_type=F32)
            o_ref[rows, sl] = (_rms(o, gn) * gate[:, sl]).astype(o_ref.dtype)
        return carry

    lax.fori_loop(0, n_chunks, body, 0, unroll=2)


def _hgrn(zhg, lb, gn, batch, seq):
    rall, low, mask, n_levels = _hgrn_tables()
    hb = HG_HEADS // HG_GROUP
    gw = HG_GROUP * HG_DK
    tc = min(HG_ROWS, seq)
    nt = seq // tc
    col = lambda off: pl.BlockSpec((tc, gw), lambda b, hh, tt: (b * nt + tt, off + hh))
    full = lambda a: pl.BlockSpec(a.shape, lambda b, hh, tt: (0,) * a.ndim)
    return pl.pallas_call(
        functools.partial(_hgrn_kernel, n_chunks=tc // HG_CHUNK, n_levels=n_levels, heads=HG_GROUP),
        grid=(batch, hb, nt),
        in_specs=[col(0), col(hb), col(2 * hb), col(3 * hb),
                  pl.BlockSpec((1, gw), lambda b, hh, tt: (0, hh)),
                  pl.BlockSpec((1, HG_DV), lambda b, hh, tt: (0, 0)),
                  full(rall), full(low), full(mask)],
        out_specs=pl.BlockSpec((tc, gw), lambda b, hh, tt: (b * nt + tt, hh)),
        out_shape=jax.ShapeDtypeStruct((batch * seq, HG_HEADS * HG_DV), BF16),
        scratch_shapes=[pltpu.VMEM((HG_GROUP, HG_DV, HG_DK), F32)],
        compiler_params=_params(("parallel", "parallel", "arbitrary"), 48),
        name="hgrn2",
    )(zhg, zhg, zhg, zhg, lb.reshape(1, HG_WIDTH), gn.reshape(1, HG_DV),
      jnp.asarray(rall, BF16), jnp.asarray(low), jnp.asarray(mask))


DSA_TQ = 256
DSA_ROW_GROUPS = 4
DSA_WIDTH_VARIANTS = 8


def _alibi_tables(seq):
    assert 8 % AT_HEADS == 0 and seq <= 64 * 256
    pos = jnp.arange(seq, dtype=jnp.int32)
    hi, lo = (pos // 64).astype(F32), (pos % 64).astype(F32)
    one = jnp.ones((seq,), F32)
    kext = jnp.stack([one, one, hi, lo], axis=1)
    kext = jnp.pad(kext, ((0, 0), (0, AT_HEAD_DIM - 4))).astype(BF16)
    qext = []
    for h in range(AT_HEADS):
        slope = 2.0 ** (-8.0 * (h + 1) / AT_HEADS)
        cols = jnp.stack([-slope * 64.0 * hi, -slope * lo, slope * 64.0 * one, slope * one], axis=1)
        qext.append(jnp.pad(cols, ((0, 0), (0, AT_HEAD_DIM - 4))))
    return jnp.concatenate(qext, axis=1).astype(BF16), kext


def _dsa_kernel(iq_ref, ikt_ref, iw_ref, q_ref, qext_ref, k_ref, kext_ref, v_ref, o_ref, acc_ref, key_ref, bias_ref,
                *, tq, seq, ksel, col_step):
    first_row = pl.program_id(1) * tq
    for g in range(seq // col_step):
        @pl.when(first_row // col_step == g)
        def _(width=(g + 1) * col_step):
            _dsa_body(iq_ref, ikt_ref, iw_ref, q_ref, qext_ref, k_ref, kext_ref, v_ref, o_ref, acc_ref, key_ref,
                      bias_ref, tq=tq, ksel=ksel, width=width, first_row=first_row)


def _dsa_body(iq_ref, ikt_ref, iw_ref, q_ref, qext_ref, k_ref, kext_ref, v_ref, o_ref, acc_ref, key_ref, bias_ref,
              *, tq, ksel, width, first_row):
    seq = width
    t0 = first_row
    acc_ref, key_ref, bias_ref = acc_ref.at[:, :width], key_ref.at[:, :width], bias_ref.at[:, :width]
    ikt = ikt_ref[0, :, :width]
    ik_hi = ikt.astype(BF16)
    ik_lo = (ikt - ik_hi.astype(F32)).astype(BF16)
    rhs = jnp.concatenate([ik_hi, ik_hi, ik_lo, ik_lo], axis=0)
    lane = lax.broadcasted_iota(jnp.int32, (tq, LANES), 1)
    w_all = iw_ref[...] * (IDX_HEADS ** -0.5 * IDX_DIM ** -0.5)
    for h in range(IDX_HEADS):
        x = iq_ref[:, h * LANES:(h + 1) * LANES]
        x_hi = x.astype(BF16).astype(F32)
        u = jnp.where(lane < IDX_DIM, x_hi, x - x_hi).astype(BF16)
        s = jnp.dot(jnp.concatenate([u, u], axis=1), rhs, preferred_element_type=F32)
        term = jnp.maximum(s, 0.0) * w_all[:, IDX_DIM + h:IDX_DIM + h + 1]
        if h == 0:
            acc_ref[...] = term
        else:
            acc_ref[...] += term

    col = lax.broadcasted_iota(jnp.int32, (tq, seq), 1)
    row = t0 + lax.broadcasted_iota(jnp.int32, (tq, seq), 0)
    vis = col <= row
    bits = pltpu.bitcast(acc_ref[...], jnp.int32)
    key = bits ^ ((bits >> 31) & 0x7FFFFFFF)
    key_ref[...] = jnp.where(vis, key, INT_MIN)

    rg = tq // DSA_ROW_GROUPS

    def bisect(it, cs):
        bit = jnp.left_shift(jnp.int32(1), 31 - it)
        out = []
        for gi, c in enumerate(cs):
            cand = c | bit
            keys = key_ref[gi * rg:(gi + 1) * rg, :]
            cnt = jnp.sum((keys >= (cand ^ INT_MIN)).astype(jnp.int32), axis=1, keepdims=True)
            out.append(jnp.where(cnt >= ksel, cand, c))
        return tuple(out)

    cs = lax.fori_loop(0, 32, bisect, tuple(jnp.zeros((rg, 1), jnp.int32) for _ in range(DSA_ROW_GROUPS)))
    thr = jnp.concatenate(cs, axis=0) ^ INT_MIN
    bias_ref[...] = jnp.where(vis, jnp.where(key_ref[...] >= thr, 0.0, NEG_BIG), NEG_BIG)

    kext = kext_ref[:width, :]
    for h in range(AT_HEADS):
        hs = slice(h * AT_HEAD_DIM, (h + 1) * AT_HEAD_DIM)
        qa = jnp.concatenate([q_ref[:, hs], qext_ref[:, hs]], axis=1)
        ka = jnp.concatenate([k_ref[:width, hs], kext], axis=1)
        logit = lax.dot_general(qa, ka, _NT, preferred_element_type=F32) + bias_ref[...]
        m = jnp.max(logit, axis=1, keepdims=True)
        p = jnp.exp(logit - m)
        den = jnp.sum(p, axis=1, keepdims=True)
        oh = jnp.dot(p.astype(BF16), v_ref[:width, hs], preferred_element_type=F32) / den
        o_ref[:, hs] = oh.astype(o_ref.dtype)


def _dsa(iq_dup, ikt, ikw, zat, batch, seq, tq=DSA_TQ):
    nq = seq // tq
    ksel = min(TOPK_MAX, seq // 4)
    col_step = max(seq // DSA_WIDTH_VARIANTS, tq)
    qext, kext = _alibi_tables(seq)
    return pl.pallas_call(
        functools.partial(_dsa_kernel, tq=tq, seq=seq, ksel=ksel, col_step=col_step),
        grid=(batch, nq),
        in_specs=[pl.BlockSpec((tq, IDX_HEADS * LANES), lambda b, i: (b * nq + i, 0)),
                  pl.BlockSpec((1, IDX_DIM, seq), lambda b, i: (b, 0, 0)),
                  pl.BlockSpec((tq, LANES), lambda b, i: (b * nq + i, 0)),
                  pl.BlockSpec((tq, AT_WIDTH), lambda b, i: (b * nq + i, 0)),
                  pl.BlockSpec((tq, AT_WIDTH), lambda b, i: (i, 0)),
                  pl.BlockSpec((seq, AT_WIDTH), lambda b, i: (b, 1)),
                  pl.BlockSpec((seq, AT_HEAD_DIM), lambda b, i: (0, 0)),
                  pl.BlockSpec((seq, AT_WIDTH), lambda b, i: (b, 2))],
        out_specs=pl.BlockSpec((tq, AT_WIDTH), lambda b, i: (b * nq + i, 0)),
        out_shape=jax.ShapeDtypeStruct((batch * seq, AT_WIDTH), BF16),
        scratch_shapes=[pltpu.VMEM((tq, seq), F32), pltpu.VMEM((tq, seq), jnp.int32),
                        pltpu.VMEM((tq, seq), F32)],
        compiler_params=_params(("parallel", "arbitrary"), 48),
        name="dsa",
    )(iq_dup, ikt, ikw, zat, qext, zat, kext, zat)


def _merge_kernel(a1_ref, w1_ref, a2_ref, w2_ref, g1_ref, g2_ref, o_ref):
    y1 = jnp.dot(a1_ref[...], w1_ref[...], preferred_element_type=F32)
    y2 = jnp.dot(a2_ref[...], w2_ref[...], preferred_element_type=F32)
    o_ref[...] = (g1_ref[...].astype(F32) * y1 + g2_ref[...].astype(F32) * y2).astype(o_ref.dtype)


def _merge(o_hg, w_hg, o_at, w_at, gates, tm=1024, tn=512):
    m, k1 = o_hg.shape
    k2 = o_at.shape[1]
    n = w_hg.shape[1]
    tm = min(tm, m)
    nb = n // tn
    return pl.pallas_call(
        _merge_kernel,
        grid=(m // tm, nb),
        in_specs=[pl.BlockSpec((tm, k1), lambda i, j: (i, 0)), pl.BlockSpec((k1, tn), lambda i, j: (0, j)),
                  pl.BlockSpec((tm, k2), lambda i, j: (i, 0)), pl.BlockSpec((k2, tn), lambda i, j: (0, j)),
                  pl.BlockSpec((tm, tn), lambda i, j: (i, j)), pl.BlockSpec((tm, tn), lambda i, j: (i, nb + j))],
        out_specs=pl.BlockSpec((tm, tn), lambda i, j: (i, j)),
        out_shape=jax.ShapeDtypeStruct((m, n), BF16),
        compiler_params=_params(("parallel", "arbitrary"), 48),
        name="merge",
    )(o_hg, w_hg, o_at, w_at, gates, gates)


def _outproj_kernel(m_ref, w_ref, x_ref, gn_ref, wr_ref, br_ref, x1_ref, hp_ref, ri_ref, rw_ref):
    x1 = x_ref[...] + jnp.dot(m_ref[...], w_ref[...], preferred_element_type=F32)
    x1_ref[...] = x1
    hn = _rms(x1, gn_ref[...])
    hp_ref[...] = _pack_bf16_halves(hn)
    h_hi = hn.astype(BF16)
    h_lo = (hn - h_hi.astype(F32)).astype(BF16)
    logit = jnp.dot(jnp.concatenate([h_hi, h_hi, h_lo], axis=1), wr_ref[...],
                    preferred_element_type=F32) + br_ref[...]
    lane = lax.broadcasted_iota(jnp.int32, logit.shape, 1)
    lane_f = lane.astype(F32)
    is_g = lane < N_GROUPS
    lg = jnp.where(is_g, logit, -jnp.inf)
    mg = jnp.max(lg, axis=1, keepdims=True)
    pg_top = 1.0 / jnp.sum(jnp.exp(lg - mg), axis=1, keepdims=True)
    gid = jnp.min(jnp.where(lg == mg, lane_f, float(LANES)), axis=1, keepdims=True).astype(jnp.int32)
    in_grp = ((lane - N_GROUPS) >> 3) == gid
    le = jnp.where(in_grp, logit, -jnp.inf)
    m1 = jnp.max(le, axis=1, keepdims=True)
    i1 = jnp.min(jnp.where(le == m1, lane_f, float(LANES)), axis=1, keepdims=True)
    le2 = jnp.where(lane_f == i1, -jnp.inf, le)
    m2 = jnp.max(le2, axis=1, keepdims=True)
    i2 = jnp.min(jnp.where(le2 == m2, lane_f, float(LANES)), axis=1, keepdims=True)
    r = jnp.exp(m2 - m1)
    w1 = pg_top / (1.0 + r)
    w2 = pg_top * r / (1.0 + r)
    e1 = i1.astype(jnp.int32) - N_GROUPS
    e2 = i2.astype(jnp.int32) - N_GROUPS
    ri_ref[...] = jnp.where(lane == 0, e1, jnp.where(lane == 1, e2, 0))
    rw_ref[...] = jnp.where(lane == 0, w1, jnp.where(lane == 1, w2, 0.0))


def _outproj(mixed, w_out, x, gn, w_r, b_r, tm=256):
    m, d = x.shape
    row = lambda w: pl.BlockSpec((tm, w), lambda i: (i, 0))
    const = lambda a: pl.BlockSpec(a.shape, lambda i: (0, 0), pipeline_mode=pl.Buffered(1))
    gn = gn.reshape(1, d)
    b_r = b_r.reshape(1, LANES)
    return pl.pallas_call(
        _outproj_kernel,
        grid=(m // tm,),
        in_specs=[row(mixed.shape[1]), const(w_out), row(d), const(gn), const(w_r), const(b_r)],
        out_specs=[row(d), row(d // 2), row(LANES), row(LANES)],
        out_shape=[jax.ShapeDtypeStruct((m, d), F32), jax.ShapeDtypeStruct((m, d // 2), jnp.int32),
                   jax.ShapeDtypeStruct((m, LANES), jnp.int32), jax.ShapeDtypeStruct((m, LANES), F32)],
        compiler_params=_params(("parallel",), 56),
        name="outproj_router",
    )(mixed, w_out, x, gn, w_r, b_r)


def _expert_kernel(te_ref, tv_ref, src_ref, dst_ref, hn_hbm, wg_ref, wu_ref, wd_ref, y_hbm,
                   xbuf, ybuf, xb_ref, wgb, wub, wdb, gsem, ssem, *, tm):
    t = pl.program_id(0)
    slot = lax.rem(t, 2)
    other = 1 - slot
    valid = tv_ref[t] > 0
    prev_valid = jnp.logical_and(t >= 1, tv_ref[jnp.maximum(t - 1, 0)] > 0)

    def gather_copy(tile, r, s):
        tok = src_ref[tile * tm + r]
        return pltpu.make_async_copy(hn_hbm.at[pl.ds(tok, 1), :], xbuf.at[s, pl.ds(r, 1), :], gsem.at[s])

    def scatter_copy(tile, r, s):
        dst = dst_ref[(tile + 1) * tm + r]
        return pltpu.make_async_copy(ybuf.at[s, pl.ds(r, 1), :], y_hbm.at[pl.ds(dst, 1), :], ssem.at[s])

    def start_rows(copy_fn, tile, s):
        def body(r, carry):
            copy_fn(tile, r, s).start()
            return carry

        lax.fori_loop(0, tm, body, 0, unroll=8)

    def gather_wait(s):
        pltpu.make_async_copy(hn_hbm.at[pl.ds(0, tm), :], xbuf.at[s], gsem.at[s]).wait()

    def scatter_wait(s):
        pltpu.make_async_copy(ybuf.at[s], y_hbm.at[pl.ds(0, tm), :], ssem.at[s]).wait()

    @pl.when(t == 0)
    def _():
        start_rows(gather_copy, 0, 0)
        ybuf[1] = jnp.zeros(ybuf.shape[1:], jnp.int32)

    @pl.when(prev_valid)
    def _():
        scatter_wait(slot)

    @pl.when(valid)
    def _():
        gather_wait(slot)

        @pl.when(jnp.logical_or(t == 0, te_ref[t] != te_ref[jnp.maximum(t - 1, 0)]))
        def _():
            wgb[...] = wg_ref[0].astype(BF16)
            wub[...] = wu_ref[0].astype(BF16)
            wdb[...] = wd_ref[0].astype(BF16)

        xb_ref[...] = _unpack_bf16_halves(xbuf[slot]).astype(BF16)
        for r in range(tm):
            gather_copy(t + 1, r, other).start()
        for r in range(tm):
            scatter_copy(t - 1, r, other).start()
        xb = xb_ref[...]
        g = jnp.dot(xb, wgb[...], preferred_element_type=F32)
        u = jnp.dot(xb, wub[...], preferred_element_type=F32)
        hid = (g * _sigmoid(g) * u).astype(BF16)
        ybuf[slot] = _pack_bf16_halves(jnp.dot(hid, wdb[...], preferred_element_type=F32))

    @pl.when(jnp.logical_and(jnp.logical_not(valid), prev_valid))
    def _():
        gather_wait(slot)
        start_rows(scatter_copy, t - 1, other)
        scatter_wait(other)


def _experts(hn, tile_expert, tile_valid, src, dst, w_g, w_u, w_d, tm):
    n_tok, dp = hn.shape
    d = 2 * dp
    n_steps = tile_expert.shape[0]
    wspec = lambda a: pl.BlockSpec((1,) + a.shape[1:], lambda t, te, tv, sr, ds: (te[t], 0, 0))
    return pl.pallas_call(
        functools.partial(_expert_kernel, tm=tm),
        grid_spec=pltpu.PrefetchScalarGridSpec(
            num_scalar_prefetch=4,
            grid=(n_steps,),
            in_specs=[pl.BlockSpec(memory_space=pl.ANY), wspec(w_g), wspec(w_u), wspec(w_d)],
            out_specs=pl.BlockSpec(memory_space=pl.ANY),
            scratch_shapes=[pltpu.VMEM((2, tm, dp), jnp.int32), pltpu.VMEM((2, tm, dp), jnp.int32),
                            pltpu.VMEM((tm, d), BF16),
                            pltpu.VMEM(w_g.shape[1:], BF16), pltpu.VMEM(w_u.shape[1:], BF16),
                            pltpu.VMEM(w_d.shape[1:], BF16),
                            pltpu.SemaphoreType.DMA((2,)), pltpu.SemaphoreType.DMA((2,))]),
        out_shape=jax.ShapeDtypeStruct((2 * n_tok + tm, dp), jnp.int32),
        compiler_params=_params(("arbitrary",), 56),
        name="experts",
    )(tile_expert, tile_valid, src, dst, hn, w_g, w_u, w_d)


def _dispatch_plan(route_e, tm):
    n_tok = route_e.shape[0]
    n_pairs = 2 * n_tok
    n_tiles = n_pairs // tm + N_EXPERTS + 1
    eflat = route_e.T.reshape(-1)
    order = jnp.argsort(eflat, stable=True).astype(jnp.int32)
    counts = jnp.sum(eflat[:, None] == jnp.arange(N_EXPERTS, dtype=jnp.int32)[None, :], axis=0, dtype=jnp.int32)
    tiles_e = (counts + tm - 1) // tm
    tile_end = jnp.cumsum(tiles_e)
    tile_off = tile_end - tiles_e
    start = jnp.cumsum(counts) - counts
    tid = jnp.arange(n_tiles, dtype=jnp.int32)
    te = jnp.minimum(jnp.sum(tile_end[None, :] <= tid[:, None], axis=1), N_EXPERTS - 1).astype(jnp.int32)
    tv = jnp.clip(counts[te] - (tid - tile_off[te]) * tm, 0, tm)
    used = tid < tile_end[-1]
    last_e = te[jnp.maximum(tile_end[-1] - 1, 0)]
    te = jnp.where(used, te, last_e).astype(jnp.int32)
    tv = jnp.where(used, tv, 0).astype(jnp.int32)
    r = jnp.arange(tm, dtype=jnp.int32)[None, :]
    src = (start[te] + (tid - tile_off[te]) * tm)[:, None] + r
    dst = jnp.where(r < tv[:, None], order[jnp.clip(src, 0, n_pairs - 1)], n_pairs + r).reshape(-1).astype(jnp.int32)
    spare = n_pairs + jnp.arange(tm, dtype=jnp.int32)
    return te, tv, dst % n_tok, jnp.concatenate([spare, dst])


def _final_kernel(x1_ref, y0_ref, y1_ref, rw_ref, p_ref, wgate_ref, wproj_ref, gple_ref, gfin_ref, o_ref,
                  *, last_layer):
    rw = rw_ref[...]
    x2 = (x1_ref[...] + rw[:, 0:1] * _unpack_bf16_halves(y0_ref[...])
          + rw[:, 1:2] * _unpack_bf16_halves(y1_ref[...]))
    hn = _rms(x2, gple_ref[...]).astype(BF16)
    gate = _sigmoid(jnp.dot(hn, wgate_ref[...], preferred_element_type=F32))
    emb = jnp.dot(p_ref[...].astype(BF16), wproj_ref[...], preferred_element_type=F32)
    x3 = x2 + gate * emb
    o_ref[...] = _rms(x3, gfin_ref[...]) if last_layer else x3


def _final(x1, y, rw, p, w_gate, w_proj, g_ple, g_fin, last_layer, tm=512):
    m, d = x1.shape
    nb = m // tm
    row = lambda w: pl.BlockSpec((tm, w), lambda i: (i, 0))
    const = lambda a: pl.BlockSpec(a.shape, lambda i: (0, 0), pipeline_mode=pl.Buffered(1))
    g_ple, g_fin = g_ple.reshape(1, d), g_fin.reshape(1, d)
    return pl.pallas_call(
        functools.partial(_final_kernel, last_layer=last_layer),
        grid=(nb,),
        in_specs=[row(d), row(d // 2), pl.BlockSpec((tm, d // 2), lambda i: (nb + i, 0)), row(LANES), row(p.shape[1]),
                  const(w_gate), const(w_proj), const(g_ple), const(g_fin)],
        out_specs=row(d),
        out_shape=jax.ShapeDtypeStruct((m, d), F32),
        compiler_params=_params(("parallel",), 56),
        name="final",
    )(x1, y, y, rw, p, w_gate, w_proj, g_ple, g_fin)


def kernel(x, p, attn_norm, w_in, b_in, hg_lb_logits, hg_out_norm, w_hg_up, w_attn_up, w_out, ffn_norm,
           w_router_group, b_router_group, w_router_expert, b_router_expert, w_exp_gate, w_exp_up,
           w_exp_down, ple_norm, w_ple_gate, w_ple_proj, final_norm):
    batch, seq, d = x.shape
    n = batch * seq
    depth = w_in.shape[0]
    o_hf, o_at = 0, 4 * HG_WIDTH
    o_iq = o_at + 3 * AT_WIDTH
    o_ik = o_iq + IDX_HEADS * IDX_DIM
    o_iw = o_ik + IDX_DIM
    o_g = o_iw + IDX_HEADS
    lbs = jnp.cumsum(jax.nn.softmax(hg_lb_logits.astype(F32), axis=0), axis=0)
    xf = x.reshape(n, d)
    moe_tm = min(256, n // 2)
    for li in range(depth):
        wt, b = w_in[li].T, b_in[li]
        h = _rmsnorm(xf, attn_norm[li], BF16)
        zhg = _matmul(h, wt[o_hf:o_at].astype(BF16), b[o_hf:o_at], F32, name="proj_hgrn")
        qkv_scale = jnp.concatenate([jnp.full((AT_WIDTH,), AT_HEAD_DIM ** -0.5, F32), jnp.ones((2 * AT_WIDTH,), F32)])
        zat = _matmul(h, (wt[o_at:o_iq] * qkv_scale[:, None]).astype(BF16), b[o_at:o_iq] * qkv_scale, BF16,
                      name="proj_attn")
        dup = lambda a: jnp.repeat(a.reshape((IDX_HEADS, 1, IDX_DIM) + a.shape[1:]), 2, axis=1).reshape(
            (2 * IDX_HEADS * IDX_DIM,) + a.shape[1:])
        ziq = _matmul(h, dup(wt[o_iq:o_ik]).astype(BF16), dup(b[o_iq:o_ik]), F32, name="proj_iq")
        pad = LANES - IDX_DIM - IDX_HEADS
        zikw = _matmul(h, jnp.pad(wt[o_ik:o_g], ((0, pad), (0, 0))).astype(BF16),
                       jnp.pad(b[o_ik:o_g], (0, pad)), F32, name="proj_ikw")
        gates = _matmul(h, wt[o_g:].astype(BF16), b[o_g:], BF16, act="sigmoid", name="proj_gates")

        o_hg = _hgrn(zhg, lbs[li], hg_out_norm[li], batch, seq)
        ikt = zikw[:, :IDX_DIM].reshape(batch, seq, IDX_DIM).transpose(0, 2, 1)
        o_att = _dsa(ziq, ikt, zikw, zat, batch, seq)
        mixed = _merge(o_hg, w_hg_up[li].astype(BF16), o_att, w_attn_up[li].astype(BF16), gates)

        w_r = jnp.pad(jnp.concatenate([w_router_group[li], w_router_expert[li]], axis=1),
                      ((0, 0), (0, LANES - N_GROUPS - N_EXPERTS)))
        b_r = jnp.pad(jnp.concatenate([b_router_group[li], b_router_expert[li]]), (0, LANES - N_GROUPS - N_EXPERTS))
        w_r_hi = w_r.astype(BF16)
        w_r_lo = (w_r - w_r_hi.astype(F32)).astype(BF16)
        w_r = jnp.concatenate([w_r_hi, w_r_lo, w_r_hi], axis=0)
        x1, hn, route_e, route_w = _outproj(mixed, w_out[li].astype(BF16), xf, ffn_norm[li], w_r, b_r)
        tile_e, tile_v, src, dst = _dispatch_plan(route_e[:, :2], moe_tm)
        y = _experts(hn, tile_e, tile_v, src, dst, w_exp_gate[li], w_exp_up[li], w_exp_down[li], moe_tm)
        xf = _final(x1, y, route_w, p[li].reshape(n, -1), w_ple_gate[li].astype(BF16),
                    w_ple_proj[li].astype(BF16), ple_norm[li], final_norm, li == depth - 1)
    return xf.reshape(batch, seq, d)
```

```python
import functools

import numpy as np
import jax
import jax.numpy as jnp
from jax import lax
from jax.experimental import pallas as pl
from jax.experimental.pallas import tpu as pltpu

F32 = jnp.float32
BF16 = jnp.bfloat16
HIGHEST = lax.Precision.HIGHEST

HG_HEADS = 8
HG_DK = 128
HG_DV = 128
HG_WIDTH = HG_HEADS * HG_DK
HG_CHUNK = 64
AT_HEADS = 8
AT_HEAD_DIM = 128
AT_WIDTH = AT_HEADS * AT_HEAD_DIM
IDX_HEADS = 16
IDX_DIM = 64
TOPK_MAX = 256
N_GROUPS = 4
EXPERTS_PER_GROUP = 8
N_EXPERTS = N_GROUPS * EXPERTS_PER_GROUP
D_EXPERT = 512
RMS_EPS = 1e-6
LANES = 128
INT_MIN = -(2 ** 31)
NEG_BIG = -1e30

_NT = (((1,), (1,)), ((), ()))
_TN = (((0,), (0,)), ((), ()))


def _params(sem, vmem_mb=None):
    kw = dict(dimension_semantics=sem)
    if vmem_mb is not None:
        kw["vmem_limit_bytes"] = vmem_mb * 1024 * 1024
    return pltpu.CompilerParams(**kw)


def _sigmoid(x):
    return 1.0 / (1.0 + jnp.exp(-x))


def _pack_bf16_halves(x):
    c = x.shape[1] // 2
    xr = x.astype(BF16).astype(F32)
    hi = pltpu.bitcast(xr[:, :c], jnp.int32)
    lo = lax.shift_right_logical(pltpu.bitcast(xr[:, c:], jnp.int32), 16)
    return hi | lo


def _unpack_bf16_halves(w):
    hi = pltpu.bitcast(w & -65536, F32)
    lo = pltpu.bitcast(lax.shift_left(w, 16), F32)
    return jnp.concatenate([hi, lo], axis=1)


def _rms(x, g):
    ms = jnp.mean(x * x, axis=-1, keepdims=True)
    return x * lax.rsqrt(ms + RMS_EPS) * g


def _rmsnorm_kernel(x_ref, g_ref, o_ref):
    o_ref[...] = _rms(x_ref[...], g_ref[...]).astype(o_ref.dtype)


def _rmsnorm(x, g, out_dtype, tm=512):
    m, d = x.shape
    return pl.pallas_call(
        _rmsnorm_kernel,
        grid=(m // tm,),
        in_specs=[pl.BlockSpec((tm, d), lambda i: (i, 0)), pl.BlockSpec((1, d), lambda i: (0, 0))],
        out_specs=pl.BlockSpec((tm, d), lambda i: (i, 0)),
        out_shape=jax.ShapeDtypeStruct((m, d), out_dtype),
        compiler_params=_params(("parallel",)),
        name="rmsnorm",
    )(x, g.reshape(1, d))


def _mm_kernel(a_ref, wt_ref, b_ref, o_ref, *, act):
    acc = lax.dot_general(a_ref[...], wt_ref[...], _NT, preferred_element_type=F32) + b_ref[...]
    if act == "sigmoid":
        acc = _sigmoid(acc)
    o_ref[...] = acc.astype(o_ref.dtype)


def _matmul(a, wt, b, out_dtype, act=None, tm=1024, tn=512, name="proj"):
    m, k = a.shape
    n = wt.shape[0]
    tm, tn = min(tm, m), min(tn, n)
    return pl.pallas_call(
        functools.partial(_mm_kernel, act=act),
        grid=(m // tm, n // tn),
        in_specs=[pl.BlockSpec((tm, k), lambda i, j: (i, 0)),
                  pl.BlockSpec((tn, k), lambda i, j: (j, 0)),
                  pl.BlockSpec((1, tn), lambda i, j: (0, j))],
        out_specs=pl.BlockSpec((tm, tn), lambda i, j: (i, j)),
        out_shape=jax.ShapeDtypeStruct((m, n), out_dtype),
        compiler_params=_params(("parallel", "arbitrary"), 48),
        name=name,
    )(a, wt, b.reshape(1, n))


HG_GROUP = 8
HG_ROWS = 512


def _hgrn_tables():
    c = HG_CHUNK
    levels = [c >> i for i in range(int(np.log2(c)))]
    t = np.arange(c)
    u = np.arange(c)
    rall = [(u[None, :] <= t[:, None]), (u[None, :] > t[:, None])]
    low, mask = [], []
    for p in levels:
        half = p // 2
        m = (t // p) * p + half
        lower = (t % p) >= half
        r = np.where(lower[:, None], (u[None, :] > m[:, None]) & (u[None, :] <= t[:, None]),
                     (u[None, :] > t[:, None]) & (u[None, :] <= m[:, None]))
        rall.append(r)
        low.append(np.broadcast_to(lower[:, None], (c, LANES)))
        mask.append(((t[:, None] // p) == (t[None, :] // p)) & lower[:, None] & ~lower[None, :])
    mask.append(np.eye(c, dtype=bool))
    rall = np.concatenate(rall, 0).astype(np.float32)
    return (np.tile(rall, (1, 3)), np.stack(low).astype(np.float32), np.stack(mask).astype(np.float32),
            len(levels))


def _hgrn_kernel(q_ref, f_ref, i_ref, g_ref, lb_ref, gn_ref, rall_ref, low_ref, mask_ref, o_ref, st_ref,
                 *, n_chunks, n_levels, heads):
    c = HG_CHUNK

    @pl.when(pl.program_id(2) == 0)
    def _():
        st_ref[...] = jnp.zeros_like(st_ref)

    lb = lb_ref[...]
    one_m_lb = 1.0 - lb
    gn = gn_ref[...]

    def body(ci, carry):
        rows = pl.ds(pl.multiple_of(ci * c, c), c)
        z = f_ref[rows, :]
        q = q_ref[rows, :]
        e = jnp.exp(-jnp.abs(z))
        r = 1.0 / (1.0 + e)
        pos = z >= 0
        sig = jnp.where(pos, r, e * r)
        nsig = jnp.where(pos, e * r, r)
        logf = jnp.log(lb + one_m_lb * sig)
        k = one_m_lb * nsig
        qs = q * _sigmoid(q)
        l1 = logf.astype(BF16)
        r1 = logf - l1.astype(F32)
        l2 = r1.astype(BF16)
        l3 = (r1 - l2.astype(F32)).astype(BF16)
        ex = jnp.exp(jnp.dot(rall_ref[...], jnp.concatenate([l1, l2, l3], axis=0), preferred_element_type=F32))
        vb = i_ref[rows, :].astype(BF16)
        gate = g_ref[rows, :]
        gate = gate * _sigmoid(gate)
        for g in range(heads):
            sl = slice(g * HG_DK, (g + 1) * HG_DK)
            qg, kg, vg = qs[:, sl], k[:, sl], vb[:, sl]
            eb = ex[0:c, sl]
            a = mask_ref[n_levels] * lax.dot_general(qg.astype(BF16), kg.astype(BF16), _NT,
                                                     preferred_element_type=F32)
            for li in range(n_levels):
                w = (jnp.where(low_ref[li] != 0.0, qg, kg) * ex[(2 + li) * c:(3 + li) * c, sl]).astype(BF16)
                a = a + mask_ref[li] * lax.dot_general(w, w, _NT, preferred_element_type=F32)
            st = st_ref[g]
            o = (jnp.dot(a.astype(BF16), vg, preferred_element_type=F32)
                 + lax.dot_general((qg * eb).astype(BF16), st.astype(BF16), _NT, preferred_element_type=F32))
            st_ref[g] = st * eb[c - 1:c, :] + lax.dot_general(vg, (kg * ex[c:2 * c, sl]).astype(BF16), _TN,
                                                              preferred_element_type=F32)
            o_ref[rows, sl] = (_rms(o, gn) * gate[:, sl]).astype(o_ref.dtype)
        return carry

    lax.fori_loop(0, n_chunks, body, 0, unroll=2)


def _hgrn(zhg, lb, gn, batch, seq):
    rall, low, mask, n_levels = _hgrn_tables()
    hb = HG_HEADS // HG_GROUP
    gw = HG_GROUP * HG_DK
    tc = min(HG_ROWS, seq)
    nt = seq // tc
    col = lambda off: pl.BlockSpec((tc, gw), lambda b, hh, tt: (b * nt + tt, off + hh))
    full = lambda a: pl.BlockSpec(a.shape, lambda b, hh, tt: (0,) * a.ndim)
    return pl.pallas_call(
        functools.partial(_hgrn_kernel, n_chunks=tc // HG_CHUNK, n_levels=n_levels, heads=HG_GROUP),
        grid=(batch, hb, nt),
        in_specs=[col(0), col(hb), col(2 * hb), col(3 * hb),
                  pl.BlockSpec((1, gw), lambda b, hh, tt: (0, hh)),
                  pl.BlockSpec((1, HG_DV), lambda b, hh, tt: (0, 0)),
                  full(rall), full(low), full(mask)],
        out_specs=pl.BlockSpec((tc, gw), lambda b, hh, tt: (b * nt + tt, hh)),
        out_shape=jax.ShapeDtypeStruct((batch * seq, HG_HEADS * HG_DV), BF16),
        scratch_shapes=[pltpu.VMEM((HG_GROUP, HG_DV, HG_DK), F32)],
        compiler_params=_params(("parallel", "parallel", "arbitrary"), 48),
        name="hgrn2",
    )(zhg, zhg, zhg, zhg, lb.reshape(1, HG_WIDTH), gn.reshape(1, HG_DV),
      jnp.asarray(rall, BF16), jnp.asarray(low), jnp.asarray(mask))


DSA_TQ = 256
DSA_ROW_GROUPS = 4
DSA_WIDTH_VARIANTS = 4


def _alibi_tables(seq):
    assert 8 % AT_HEADS == 0 and seq <= 64 * 256
    pos = jnp.arange(seq, dtype=jnp.int32)
    hi, lo = (pos // 64).astype(F32), (pos % 64).astype(F32)
    one = jnp.ones((seq,), F32)
    kext = jnp.stack([one, one, hi, lo], axis=1)
    kext = jnp.pad(kext, ((0, 0), (0, AT_HEAD_DIM - 4))).astype(BF16)
    qext = []
    for h in range(AT_HEADS):
        slope = 2.0 ** (-8.0 * (h + 1) / AT_HEADS)
        cols = jnp.stack([-slope * 64.0 * hi, -slope * lo, slope * 64.0 * one, slope * one], axis=1)
        qext.append(jnp.pad(cols, ((0, 0), (0, AT_HEAD_DIM - 4))))
    return jnp.concatenate(qext, axis=1).astype(BF16), kext


def _dsa_kernel(iq_ref, ikt_ref, iw_ref, q_ref, qext_ref, k_ref, kext_ref, v_ref, o_ref, acc_ref, key_ref, bias_ref,
                *, tq, seq, ksel, col_step):
    first_row = pl.program_id(1) * tq
    for g in range(seq // col_step):
        @pl.when(first_row // col_step == g)
        def _(width=(g + 1) * col_step):
            _dsa_body(iq_ref, ikt_ref, iw_ref, q_ref, qext_ref, k_ref, kext_ref, v_ref, o_ref, acc_ref, key_ref,
                      bias_ref, tq=tq, ksel=ksel, width=width, first_row=first_row)


def _dsa_body(iq_ref, ikt_ref, iw_ref, q_ref, qext_ref, k_ref, kext_ref, v_ref, o_ref, acc_ref, key_ref, bias_ref,
              *, tq, ksel, width, first_row):
    seq = width
    t0 = first_row
    acc_ref, key_ref, bias_ref = acc_ref.at[:, :width], key_ref.at[:, :width], bias_ref.at[:, :width]
    ikt = ikt_ref[0, :, :width]
    ik_hi = ikt.astype(BF16)
    ik_lo = (ikt - ik_hi.astype(F32)).astype(BF16)
    rhs = jnp.concatenate([ik_hi, ik_hi, ik_lo, ik_lo], axis=0)
    lane = lax.broadcasted_iota(jnp.int32, (tq, LANES), 1)
    w_all = iw_ref[...] * (IDX_HEADS ** -0.5 * IDX_DIM ** -0.5)
    for h in range(IDX_HEADS):
        if h % 2 == 0:
            x = iq_ref[:, (h // 2) * LANES:(h // 2 + 1) * LANES]
            x_hi = x.astype(BF16).astype(F32)
            x_lo = x - x_hi
            u = jnp.where(lane < IDX_DIM, x_hi, pltpu.roll(x_lo, IDX_DIM, 1)).astype(BF16)
        else:
            u = jnp.where(lane < IDX_DIM, pltpu.roll(x_hi, IDX_DIM, 1), x_lo).astype(BF16)
        s = jnp.dot(jnp.concatenate([u, u], axis=1), rhs, preferred_element_type=F32)
        term = jnp.maximum(s, 0.0) * w_all[:, IDX_DIM + h:IDX_DIM + h + 1]
        if h == 0:
            acc_ref[...] = term
        else:
            acc_ref[...] += term

    col = lax.broadcasted_iota(jnp.int32, (tq, seq), 1)
    row = t0 + lax.broadcasted_iota(jnp.int32, (tq, seq), 0)
    vis = col <= row
    bits = pltpu.bitcast(acc_ref[...], jnp.int32)
    key = bits ^ ((bits >> 31) & 0x7FFFFFFF)
    key_ref[...] = jnp.where(vis, key, INT_MIN)

    rg = tq // DSA_ROW_GROUPS

    def bisect(it, cs):
        bit = jnp.left_shift(jnp.int32(1), 31 - it)
        out = []
        for gi, c in enumerate(cs):
            cand = c | bit
            keys = key_ref[gi * rg:(gi + 1) * rg, :]
            cnt = jnp.sum((keys >= (cand ^ INT_MIN)).astype(jnp.int32), axis=1, keepdims=True)
            out.append(jnp.where(cnt >= ksel, cand, c))
        return tuple(out)

    cs = lax.fori_loop(0, 32, bisect, tuple(jnp.zeros((rg, 1), jnp.int32) for _ in range(DSA_ROW_GROUPS)))
    thr = jnp.concatenate(cs, axis=0) ^ INT_MIN
    bias_ref[...] = jnp.where(vis, jnp.where(key_ref[...] >= thr, 0.0, NEG_BIG), NEG_BIG)

    kext = kext_ref[:width, :]
    for h in range(AT_HEADS):
        hs = slice(h * AT_HEAD_DIM, (h + 1) * AT_HEAD_DIM)
        qa = jnp.concatenate([q_ref[:, hs], qext_ref[:, hs]], axis=1)
        ka = jnp.concatenate([k_ref[:width, hs], kext], axis=1)
        logit = lax.dot_general(qa, ka, _NT, preferred_element_type=F32) + bias_ref[...]
        m = jnp.max(logit, axis=1, keepdims=True)
        p = jnp.exp(logit - m)
        den = jnp.sum(p, axis=1, keepdims=True)
        oh = jnp.dot(p.astype(BF16), v_ref[:width, hs], preferred_element_type=F32) / den
        o_ref[:, hs] = oh.astype(o_ref.dtype)


def _dsa(ziq, ikt, ikw, zat, batch, seq, tq=DSA_TQ):
    nq = seq // tq
    ksel = min(TOPK_MAX, seq // 4)
    col_step = max(seq // DSA_WIDTH_VARIANTS, tq)
    qext, kext = _alibi_tables(seq)
    return pl.pallas_call(
        functools.partial(_dsa_kernel, tq=tq, seq=seq, ksel=ksel, col_step=col_step),
        grid=(batch, nq),
        in_specs=[pl.BlockSpec((tq, IDX_HEADS * IDX_DIM), lambda b, i: (b * nq + i, 0)),
                  pl.BlockSpec((1, IDX_DIM, seq), lambda b, i: (b, 0, 0)),
                  pl.BlockSpec((tq, LANES), lambda b, i: (b * nq + i, 0)),
                  pl.BlockSpec((tq, AT_WIDTH), lambda b, i: (b * nq + i, 0)),
                  pl.BlockSpec((tq, AT_WIDTH), lambda b, i: (i, 0)),
                  pl.BlockSpec((seq, AT_WIDTH), lambda b, i: (b, 1)),
                  pl.BlockSpec((seq, AT_HEAD_DIM), lambda b, i: (0, 0)),
                  pl.BlockSpec((seq, AT_WIDTH), lambda b, i: (b, 2))],
        out_specs=pl.BlockSpec((tq, AT_WIDTH), lambda b, i: (b * nq + i, 0)),
        out_shape=jax.ShapeDtypeStruct((batch * seq, AT_WIDTH), BF16),
        scratch_shapes=[pltpu.VMEM((tq, seq), F32), pltpu.VMEM((tq, seq), jnp.int32),
                        pltpu.VMEM((tq, seq), F32)],
        compiler_params=_params(("parallel", "arbitrary"), 48),
        name="dsa",
    )(ziq, ikt, ikw, zat, qext, zat, kext, zat)


def _merge_kernel(a1_ref, w1_ref, a2_ref, w2_ref, g1_ref, g2_ref, o_ref):
    y1 = jnp.dot(a1_ref[...], w1_ref[...], preferred_element_type=F32)
    y2 = jnp.dot(a2_ref[...], w2_ref[...], preferred_element_type=F32)
    o_ref[...] = (g1_ref[...].astype(F32) * y1 + g2_ref[...].astype(F32) * y2).astype(o_ref.dtype)


def _merge(o_hg, w_hg, o_at, w_at, gates, tm=1024, tn=512):
    m, k1 = o_hg.shape
    k2 = o_at.shape[1]
    n = w_hg.shape[1]
    tm = min(tm, m)
    nb = n // tn
    return pl.pallas_call(
        _merge_kernel,
        grid=(m // tm, nb),
        in_specs=[pl.BlockSpec((tm, k1), lambda i, j: (i, 0)), pl.BlockSpec((k1, tn), lambda i, j: (0, j)),
                  pl.BlockSpec((tm, k2), lambda i, j: (i, 0)), pl.BlockSpec((k2, tn), lambda i, j: (0, j)),
                  pl.BlockSpec((tm, tn), lambda i, j: (i, j)), pl.BlockSpec((tm, tn), lambda i, j: (i, nb + j))],
        out_specs=pl.BlockSpec((tm, tn), lambda i, j: (i, j)),
        out_shape=jax.ShapeDtypeStruct((m, n), BF16),
        compiler_params=_params(("parallel", "arbitrary"), 48),
        name="merge",
    )(o_hg, w_hg, o_at, w_at, gates, gates)


def _outproj_kernel(m_ref, w_ref, x_ref, gn_ref, wr_ref, br_ref, x1_ref, hp_ref, ri_ref, rw_ref):
    x1 = x_ref[...] + jnp.dot(m_ref[...], w_ref[...], preferred_element_type=F32)
    x1_ref[...] = x1
    hn = _rms(x1, gn_ref[...])
    hp_ref[...] = _pack_bf16_halves(hn)
    h_hi = hn.astype(BF16)
    h_lo = (hn - h_hi.astype(F32)).astype(BF16)
    logit = jnp.dot(jnp.concatenate([h_hi, h_hi, h_lo], axis=1), wr_ref[...],
                    preferred_element_type=F32) + br_ref[...]
    lane = lax.broadcasted_iota(jnp.int32, logit.shape, 1)
    lane_f = lane.astype(F32)
    is_g = lane < N_GROUPS
    lg = jnp.where(is_g, logit, -jnp.inf)
    mg = jnp.max(lg, axis=1, keepdims=True)
    pg_top = 1.0 / jnp.sum(jnp.exp(lg - mg), axis=1, keepdims=True)
    gid = jnp.min(jnp.where(lg == mg, lane_f, float(LANES)), axis=1, keepdims=True).astype(jnp.int32)
    in_grp = ((lane - N_GROUPS) >> 3) == gid
    le = jnp.where(in_grp, logit, -jnp.inf)
    m1 = jnp.max(le, axis=1, keepdims=True)
    i1 = jnp.min(jnp.where(le == m1, lane_f, float(LANES)), axis=1, keepdims=True)
    le2 = jnp.where(lane_f == i1, -jnp.inf, le)
    m2 = jnp.max(le2, axis=1, keepdims=True)
    i2 = jnp.min(jnp.where(le2 == m2, lane_f, float(LANES)), axis=1, keepdims=True)
    r = jnp.exp(m2 - m1)
    w1 = pg_top / (1.0 + r)
    w2 = pg_top * r / (1.0 + r)
    e1 = i1.astype(jnp.int32) - N_GROUPS
    e2 = i2.astype(jnp.int32) - N_GROUPS
    ri_ref[...] = jnp.where(lane == 0, e1, jnp.where(lane == 1, e2, 0))
    rw_ref[...] = jnp.where(lane == 0, w1, jnp.where(lane == 1, w2, 0.0))


def _outproj(mixed, w_out, x, gn, w_r, b_r, tm=256):
    m, d = x.shape
    row = lambda w: pl.BlockSpec((tm, w), lambda i: (i, 0))
    const = lambda a: pl.BlockSpec(a.shape, lambda i: (0, 0), pipeline_mode=pl.Buffered(1))
    gn = gn.reshape(1, d)
    b_r = b_r.reshape(1, LANES)
    return pl.pallas_call(
        _outproj_kernel,
        grid=(m // tm,),
        in_specs=[row(mixed.shape[1]), const(w_out), row(d), const(gn), const(w_r), const(b_r)],
        out_specs=[row(d), row(d // 2), row(LANES), row(LANES)],
        out_shape=[jax.ShapeDtypeStruct((m, d), F32), jax.ShapeDtypeStruct((m, d // 2), jnp.int32),
                   jax.ShapeDtypeStruct((m, LANES), jnp.int32), jax.ShapeDtypeStruct((m, LANES), F32)],
        compiler_params=_params(("parallel",), 56),
        name="outproj_router",
    )(mixed, w_out, x, gn, w_r, b_r)


def _expert_kernel(te_ref, tv_ref, src_ref, dst_ref, hn_hbm, wg_ref, wu_ref, wd_ref, y_hbm,
                   xbuf, ybuf, xb_ref, wgb, wub, wdb, gsem, ssem, *, tm):
    t = pl.program_id(0)
    slot = lax.rem(t, 2)
    other = 1 - slot
    valid = tv_ref[t] > 0
    prev_valid = jnp.logical_and(t >= 1, tv_ref[jnp.maximum(t - 1, 0)] > 0)

    def gather_copy(tile, r, s):
        tok = src_ref[tile * tm + r]
        return pltpu.make_async_copy(hn_hbm.at[pl.ds(tok, 1), :], xbuf.at[s, pl.ds(r, 1), :], gsem.at[s])

    def scatter_copy(tile, r, s):
        dst = dst_ref[(tile + 1) * tm + r]
        return pltpu.make_async_copy(ybuf.at[s, pl.ds(r, 1), :], y_hbm.at[pl.ds(dst, 1), :], ssem.at[s])

    def start_rows(copy_fn, tile, s):
        def body(r, carry):
            copy_fn(tile, r, s).start()
            return carry

        lax.fori_loop(0, tm, body, 0, unroll=8)

    def gather_wait(s):
        pltpu.make_async_copy(hn_hbm.at[pl.ds(0, tm), :], xbuf.at[s], gsem.at[s]).wait()

    def scatter_wait(s):
        pltpu.make_async_copy(ybuf.at[s], y_hbm.at[pl.ds(0, tm), :], ssem.at[s]).wait()

    @pl.when(t == 0)
    def _():
        start_rows(gather_copy, 0, 0)
        ybuf[1] = jnp.zeros(ybuf.shape[1:], jnp.int32)

    @pl.when(prev_valid)
    def _():
        scatter_wait(slot)

    @pl.when(valid)
    def _():
        gather_wait(slot)

        @pl.when(jnp.logical_or(t == 0, te_ref[t] != te_ref[jnp.maximum(t - 1, 0)]))
        def _():
            wgb[...] = wg_ref[0].astype(BF16)
            wub[...] = wu_ref[0].astype(BF16)
            wdb[...] = wd_ref[0].astype(BF16)

        xb_ref[...] = _unpack_bf16_halves(xbuf[slot]).astype(BF16)
        for r in range(tm):
            gather_copy(t + 1, r, other).start()
        for r in range(tm):
            scatter_copy(t - 1, r, other).start()
        xb = xb_ref[...]
        g = jnp.dot(xb, wgb[...], preferred_element_type=F32)
        u = jnp.dot(xb, wub[...], preferred_element_type=F32)
        hid = (g * _sigmoid(g) * u).astype(BF16)
        ybuf[slot] = _pack_bf16_halves(jnp.dot(hid, wdb[...], preferred_element_type=F32))

    @pl.when(jnp.logical_and(jnp.logical_not(valid), prev_valid))
    def _():
        gather_wait(slot)
        start_rows(scatter_copy, t - 1, other)
        scatter_wait(other)


def _experts(hn, tile_expert, tile_valid, src, dst, w_g, w_u, w_d, tm):
    n_tok, dp = hn.shape
    d = 2 * dp
    n_steps = tile_expert.shape[0]
    wspec = lambda a: pl.BlockSpec((1,) + a.shape[1:], lambda t, te, tv, sr, ds: (te[t], 0, 0))
    return pl.pallas_call(
        functools.partial(_expert_kernel, tm=tm),
        grid_spec=pltpu.PrefetchScalarGridSpec(
            num_scalar_prefetch=4,
            grid=(n_steps,),
            in_specs=[pl.BlockSpec(memory_space=pl.ANY), wspec(w_g), wspec(w_u), wspec(w_d)],
            out_specs=pl.BlockSpec(memory_space=pl.ANY),
            scratch_shapes=[pltpu.VMEM((2, tm, dp), jnp.int32), pltpu.VMEM((2, tm, dp), jnp.int32),
                            pltpu.VMEM((tm, d), BF16),
                            pltpu.VMEM(w_g.shape[1:], BF16), pltpu.VMEM(w_u.shape[1:], BF16),
                            pltpu.VMEM(w_d.shape[1:], BF16),
                            pltpu.SemaphoreType.DMA((2,)), pltpu.SemaphoreType.DMA((2,))]),
        out_shape=jax.ShapeDtypeStruct((2 * n_tok + tm, dp), jnp.int32),
        compiler_params=_params(("arbitrary",), 56),
        name="experts",
    )(tile_expert, tile_valid, src, dst, hn, w_g, w_u, w_d)


def _dispatch_plan(route_e, tm):
    n_tok = route_e.shape[0]
    n_pairs = 2 * n_tok
    n_tiles = n_pairs // tm + N_EXPERTS + 1
    eflat = route_e.T.reshape(-1)
    order = jnp.argsort(eflat, stable=True).astype(jnp.int32)
    counts = jnp.sum(eflat[:, None] == jnp.arange(N_EXPERTS, dtype=jnp.int32)[None, :], axis=0, dtype=jnp.int32)
    tiles_e = (counts + tm - 1) // tm
    tile_end = jnp.cumsum(tiles_e)
    tile_off = tile_end - tiles_e
    start = jnp.cumsum(counts) - counts
    tid = jnp.arange(n_tiles, dtype=jnp.int32)
    te = jnp.minimum(jnp.sum(tile_end[None, :] <= tid[:, None], axis=1), N_EXPERTS - 1).astype(jnp.int32)
    tv = jnp.clip(counts[te] - (tid - tile_off[te]) * tm, 0, tm)
    used = tid < tile_end[-1]
    last_e = te[jnp.maximum(tile_end[-1] - 1, 0)]
    te = jnp.where(used, te, last_e).astype(jnp.int32)
    tv = jnp.where(used, tv, 0).astype(jnp.int32)
    r = jnp.arange(tm, dtype=jnp.int32)[None, :]
    src = (start[te] + (tid - tile_off[te]) * tm)[:, None] + r
    dst = jnp.where(r < tv[:, None], order[jnp.clip(src, 0, n_pairs - 1)], n_pairs + r).reshape(-1).astype(jnp.int32)
    spare = n_pairs + jnp.arange(tm, dtype=jnp.int32)
    return te, tv, dst % n_tok, jnp.concatenate([spare, dst])


def _final_kernel(x1_ref, y0_ref, y1_ref, rw_ref, p_ref, wgate_ref, wproj_ref, gple_ref, gfin_ref, o_ref,
                  *, last_layer):
    rw = rw_ref[...]
    x2 = (x1_ref[...] + rw[:, 0:1] * _unpack_bf16_halves(y0_ref[...])
          + rw[:, 1:2] * _unpack_bf16_halves(y1_ref[...]))
    hn = _rms(x2, gple_ref[...]).astype(BF16)
    gate = _sigmoid(jnp.dot(hn, wgate_ref[...], preferred_element_type=F32))
    emb = jnp.dot(p_ref[...].astype(BF16), wproj_ref[...], preferred_element_type=F32)
    x3 = x2 + gate * emb
    o_ref[...] = _rms(x3, gfin_ref[...]) if last_layer else x3


def _final(x1, y, rw, p, w_gate, w_proj, g_ple, g_fin, last_layer, tm=512):
    m, d = x1.shape
    nb = m // tm
    row = lambda w: pl.BlockSpec((tm, w), lambda i: (i, 0))
    const = lambda a: pl.BlockSpec(a.shape, lambda i: (0, 0), pipeline_mode=pl.Buffered(1))
    g_ple, g_fin = g_ple.reshape(1, d), g_fin.reshape(1, d)
    return pl.pallas_call(
        functools.partial(_final_kernel, last_layer=last_layer),
        grid=(nb,),
        in_specs=[row(d), row(d // 2), pl.BlockSpec((tm, d // 2), lambda i: (nb + i, 0)), row(LANES), row(p.shape[1]),
                  const(w_gate), const(w_proj), const(g_ple), const(g_fin)],
        out_specs=row(d),
        out_shape=jax.ShapeDtypeStruct((m, d), F32),
        compiler_params=_params(("parallel",), 56),
        name="final",
    )(x1, y, y, rw, p, w_gate, w_proj, g_ple, g_fin)


def kernel(x, p, attn_norm, w_in, b_in, hg_lb_logits, hg_out_norm, w_hg_up, w_attn_up, w_out, ffn_norm,
           w_router_group, b_router_group, w_router_expert, b_router_expert, w_exp_gate, w_exp_up,
           w_exp_down, ple_norm, w_ple_gate, w_ple_proj, final_norm):
    batch, seq, d = x.shape
    n = batch * seq
    depth = w_in.shape[0]
    o_hf, o_at = 0, 4 * HG_WIDTH
    o_iq = o_at + 3 * AT_WIDTH
    o_ik = o_iq + IDX_HEADS * IDX_DIM
    o_iw = o_ik + IDX_DIM
    o_g = o_iw + IDX_HEADS
    lbs = jnp.cumsum(jax.nn.softmax(hg_lb_logits.astype(F32), axis=0), axis=0)
    xf = x.reshape(n, d)
    moe_tm = min(256, n // 2)
    for li in range(depth):
        wt, b = w_in[li].T, b_in[li]
        h = _rmsnorm(xf, attn_norm[li], BF16)
        zhg = _matmul(h, wt[o_hf:o_at].astype(BF16), b[o_hf:o_at], F32, name="proj_hgrn")
        qkv_scale = jnp.concatenate([jnp.full((AT_WIDTH,), AT_HEAD_DIM ** -0.5, F32), jnp.ones((2 * AT_WIDTH,), F32)])
        zat = _matmul(h, (wt[o_at:o_iq] * qkv_scale[:, None]).astype(BF16), b[o_at:o_iq] * qkv_scale, BF16,
                      name="proj_attn")
        ziq = _matmul(h, wt[o_iq:o_ik].astype(BF16), b[o_iq:o_ik], F32, name="proj_iq")
        pad = LANES - IDX_DIM - IDX_HEADS
        zikw = _matmul(h, jnp.pad(wt[o_ik:o_g], ((0, pad), (0, 0))).astype(BF16),
                       jnp.pad(b[o_ik:o_g], (0, pad)), F32, name="proj_ikw")
        gates = _matmul(h, wt[o_g:].astype(BF16), b[o_g:], BF16, act="sigmoid", name="proj_gates")

        o_hg = _hgrn(zhg, lbs[li], hg_out_norm[li], batch, seq)
        ikt = zikw[:, :IDX_DIM].reshape(batch, seq, IDX_DIM).transpose(0, 2, 1)
        o_att = _dsa(ziq, ikt, zikw, zat, batch, seq)
        mixed = _merge(o_hg, w_hg_up[li].astype(BF16), o_att, w_attn_up[li].astype(BF16), gates)

        w_r = jnp.pad(jnp.concatenate([w_router_group[li], w_router_expert[li]], axis=1),
                      ((0, 0), (0, LANES - N_GROUPS - N_EXPERTS)))
        b_r = jnp.pad(jnp.concatenate([b_router_group[li], b_router_expert[li]]), (0, LANES - N_GROUPS - N_EXPERTS))
        w_r_hi = w_r.astype(BF16)
        w_r_lo = (w_r - w_r_hi.astype(F32)).astype(BF16)
        w_r = jnp.concatenate([w_r_hi, w_r_lo, w_r_hi], axis=0)
        x1, hn, route_e, route_w = _outproj(mixed, w_out[li].astype(BF16), xf, ffn_norm[li], w_r, b_r)
        tile_e, tile_v, src, dst = _dispatch_plan(route_e[:, :2], moe_tm)
        y = _experts(hn, tile_e, tile_v, src, dst, w_exp_gate[li], w_exp_up[li], w_exp_down[li], moe_tm)
        xf = _final(x1, y, route_w, p[li].reshape(n, -1), w_ple_gate[li].astype(BF16),
                    w_ple_proj[li].astype(BF16), ple_norm[li], final_norm, li == depth - 1)
    return xf.reshape(batch, seq, d)
```

```python
import functools

import numpy as np
import jax
import jax.numpy as jnp
from jax import lax
from jax.experimental import pallas as pl
from jax.experimental.pallas import tpu as pltpu

F32 = jnp.float32
BF16 = jnp.bfloat16
HIGHEST = lax.Precision.HIGHEST

HG_HEADS = 8
HG_DK = 128
HG_DV = 128
HG_WIDTH = HG_HEADS * HG_DK
HG_CHUNK = 64
AT_HEADS = 8
AT_HEAD_DIM = 128
AT_WIDTH = AT_HEADS * AT_HEAD_DIM
IDX_HEADS = 16
IDX_DIM = 64
TOPK_MAX = 256
N_GROUPS = 4
EXPERTS_PER_GROUP = 8
N_EXPERTS = N_GROUPS * EXPERTS_PER_GROUP
D_EXPERT = 512
RMS_EPS = 1e-6
LANES = 128
INT_MIN = -(2 ** 31)
NEG_BIG = -1e30

_NT = (((1,), (1,)), ((), ()))
_TN = (((0,), (0,)), ((), ()))


def _params(sem, vmem_mb=None):
    kw = dict(dimension_semantics=sem)
    if vmem_mb is not None:
        kw["vmem_limit_bytes"] = vmem_mb * 1024 * 1024
    return pltpu.CompilerParams(**kw)


def _sigmoid(x):
    return 1.0 / (1.0 + jnp.exp(-x))


def _pack_bf16_halves(x):
    c = x.shape[1] // 2
    xr = x.astype(BF16).astype(F32)
    hi = pltpu.bitcast(xr[:, :c], jnp.int32)
    lo = lax.shift_right_logical(pltpu.bitcast(xr[:, c:], jnp.int32), 16)
    return hi | lo


def _unpack_bf16_halves(w):
    hi = pltpu.bitcast(w & -65536, F32)
    lo = pltpu.bitcast(lax.shift_left(w, 16), F32)
    return jnp.concatenate([hi, lo], axis=1)


def _rms(x, g):
    ms = jnp.mean(x * x, axis=-1, keepdims=True)
    return x * lax.rsqrt(ms + RMS_EPS) * g


def _rmsnorm_kernel(x_ref, g_ref, o_ref):
    o_ref[...] = _rms(x_ref[...], g_ref[...]).astype(o_ref.dtype)


def _rmsnorm(x, g, out_dtype, tm=512):
    m, d = x.shape
    return pl.pallas_call(
        _rmsnorm_kernel,
        grid=(m // tm,),
        in_specs=[pl.BlockSpec((tm, d), lambda i: (i, 0)), pl.BlockSpec((1, d), lambda i: (0, 0))],
        out_specs=pl.BlockSpec((tm, d), lambda i: (i, 0)),
        out_shape=jax.ShapeDtypeStruct((m, d), out_dtype),
        compiler_params=_params(("parallel",)),
        name="rmsnorm",
    )(x, g.reshape(1, d))


def _mm_kernel(a_ref, wt_ref, b_ref, o_ref, *, act):
    acc = lax.dot_general(a_ref[...], wt_ref[...], _NT, preferred_element_type=F32) + b_ref[...]
    if act == "sigmoid":
        acc = _sigmoid(acc)
    o_ref[...] = acc.astype(o_ref.dtype)


def _mm_scaled_kernel(a_ref, wt_ref, s_ref, b_ref, o_ref):
    acc = lax.dot_general(a_ref[...], wt_ref[...], _NT, preferred_element_type=F32)
    o_ref[...] = (acc * s_ref[...] + b_ref[...]).astype(o_ref.dtype)


def _matmul(a, wt, row0, n, b, out_dtype, act=None, scale=None, tm=1024, tn=512, name="proj"):
    m, k = a.shape
    tm, tn = min(tm, m), min(tn, n)
    row = lambda: pl.BlockSpec((1, tn), lambda i, j: (0, j))
    scaled = scale is not None
    return pl.pallas_call(
        _mm_scaled_kernel if scaled else functools.partial(_mm_kernel, act=act),
        grid=(m // tm, n // tn),
        in_specs=[pl.BlockSpec((tm, k), lambda i, j: (i, 0)),
                  pl.BlockSpec((pl.Element(tn), pl.Element(k)),
                               lambda i, j: (pl.multiple_of(row0 + j * tn, 16), 0))]
                 + ([row()] if scaled else []) + [row()],
        out_specs=pl.BlockSpec((tm, tn), lambda i, j: (i, j)),
        out_shape=jax.ShapeDtypeStruct((m, n), out_dtype),
        compiler_params=_params(("parallel", "arbitrary"), 48),
        name=name,
    )(a, wt, *([scale.reshape(1, n)] if scaled else []), b.reshape(1, n))


HG_GROUP = 8
HG_ROWS = 512


def _hgrn_tables():
    c = HG_CHUNK
    levels = [c >> i for i in range(int(np.log2(c)))]
    t = np.arange(c)
    u = np.arange(c)
    rall = [(u[None, :] <= t[:, None]), (u[None, :] > t[:, None])]
    low, mask = [], []
    for p in levels:
        half = p // 2
        m = (t // p) * p + half
        lower = (t % p) >= half
        r = np.where(lower[:, None], (u[None, :] > m[:, None]) & (u[None, :] <= t[:, None]),
                     (u[None, :] > t[:, None]) & (u[None, :] <= m[:, None]))
        rall.append(r)
        low.append(np.broadcast_to(lower[:, None], (c, LANES)))
        mask.append(((t[:, None] // p) == (t[None, :] // p)) & lower[:, None] & ~lower[None, :])
    mask.append(np.eye(c, dtype=bool))
    rall = np.concatenate(rall, 0).astype(np.float32)
    return (np.tile(rall, (1, 3)), np.stack(low).astype(np.float32), np.stack(mask).astype(np.float32),
            len(levels))


def _hgrn_kernel(q_ref, f_ref, i_ref, g_ref, lb_ref, gn_ref, rall_ref, low_ref, mask_ref, o_ref, st_ref,
                 *, n_chunks, n_levels, heads):
    c = HG_CHUNK

    @pl.when(pl.program_id(2) == 0)
    def _():
        st_ref[...] = jnp.zeros_like(st_ref)

    lb = lb_ref[...]
    one_m_lb = 1.0 - lb
    gn = gn_ref[...]

    def body(ci, carry):
        rows = pl.ds(pl.multiple_of(ci * c, c), c)
        z = f_ref[rows, :]
        q = q_ref[rows, :]
        e = jnp.exp(-jnp.abs(z))
        r = 1.0 / (1.0 + e)
        pos = z >= 0
        sig = jnp.where(pos, r, e * r)
        nsig = jnp.where(pos, e * r, r)
        logf = jnp.log(lb + one_m_lb * sig)
        k = one_m_lb * nsig
        qs = q * _sigmoid(q)
        l1 = logf.astype(BF16)
        r1 = logf - l1.astype(F32)
        l2 = r1.astype(BF16)
        l3 = (r1 - l2.astype(F32)).astype(BF16)
        ex = jnp.exp(jnp.dot(rall_ref[...], jnp.concatenate([l1, l2, l3], axis=0), preferred_element_type=F32))
        vb = i_ref[rows, :].astype(BF16)
        gate = g_ref[rows, :]
        gate = gate * _sigmoid(gate)
        for g in range(heads):
            sl = slice(g * HG_DK, (g + 1) * HG_DK)
            qg, kg, vg = qs[:, sl], k[:, sl], vb[:, sl]
            eb = ex[0:c, sl]
            a = mask_ref[n_levels] * lax.dot_general(qg.astype(BF16), kg.astype(BF16), _NT,
                                                     preferred_element_type=F32)
            for li in range(n_levels):
                w = (jnp.where(low_ref[li] != 0.0, qg, kg) * ex[(2 + li) * c:(3 + li) * c, sl]).astype(BF16)
                a = a + mask_ref[li] * lax.dot_general(w, w, _NT, preferred_element_type=F32)
            st = st_ref[g]
            o = (jnp.dot(a.astype(BF16), vg, preferred_element_type=F32)
                 + lax.dot_general((qg * eb).astype(BF16), st.astype(BF16), _NT, preferred_element_type=F32))
            st_ref[g] = st * eb[c - 1:c, :] + lax.dot_general(vg, (kg * ex[c:2 * c, sl]).astype(BF16), _TN,
                                                              preferred_element_type=F32)
            o_ref[rows, sl] = (_rms(o, gn) * gate[:, sl]).astype(o_ref.dtype)
        return carry

    lax.fori_loop(0, n_chunks, body, 0, unroll=2)


def _hgrn(zhg, lb, gn, batch, seq):
    rall, low, mask, n_levels = _hgrn_tables()
    hb = HG_HEADS // HG_GROUP
    gw = HG_GROUP * HG_DK
    tc = min(HG_ROWS, seq)
    nt = seq // tc
    col = lambda off: pl.BlockSpec((tc, gw), lambda b, hh, tt: (b * nt + tt, off + hh))
    full = lambda a: pl.BlockSpec(a.shape, lambda b, hh, tt: (0,) * a.ndim)
    return pl.pallas_call(
        functools.partial(_hgrn_kernel, n_chunks=tc // HG_CHUNK, n_levels=n_levels, heads=HG_GROUP),
        grid=(batch, hb, nt),
        in_specs=[col(0), col(hb), col(2 * hb), col(3 * hb),
                  pl.BlockSpec((1, gw), lambda b, hh, tt: (0, hh)),
                  pl.BlockSpec((1, HG_DV), lambda b, hh, tt: (0, 0)),
                  full(rall), full(low), full(mask)],
        out_specs=pl.BlockSpec((tc, gw), lambda b, hh, tt: (b * nt + tt, hh)),
        out_shape=jax.ShapeDtypeStruct((batch * seq, HG_HEADS * HG_DV), BF16),
        scratch_shapes=[pltpu.VMEM((HG_GROUP, HG_DV, HG_DK), F32)],
        compiler_params=_params(("parallel", "parallel", "arbitrary"), 48),
        name="hgrn2",
    )(zhg, zhg, zhg, zhg, lb.reshape(1, HG_WIDTH), gn.reshape(1, HG_DV),
      jnp.asarray(rall, BF16), jnp.asarray(low), jnp.asarray(mask))


DSA_TQ = 256
DSA_ROW_GROUPS = 4
DSA_WIDTH_VARIANTS = 4


def _alibi_tables(seq):
    assert 8 % AT_HEADS == 0 and seq <= 64 * 256
    pos = jnp.arange(seq, dtype=jnp.int32)
    hi, lo = (pos // 64).astype(F32), (pos % 64).astype(F32)
    one = jnp.ones((seq,), F32)
    kext = jnp.stack([one, one, hi, lo], axis=1)
    kext = jnp.pad(kext, ((0, 0), (0, AT_HEAD_DIM - 4))).astype(BF16)
    qext = []
    for h in range(AT_HEADS):
        slope = 2.0 ** (-8.0 * (h + 1) / AT_HEADS)
        cols = jnp.stack([-slope * 64.0 * hi, -slope * lo, slope * 64.0 * one, slope * one], axis=1)
        qext.append(jnp.pad(cols, ((0, 0), (0, AT_HEAD_DIM - 4))))
    return jnp.concatenate(qext, axis=1).astype(BF16), kext


def _dsa_kernel(iq_ref, ikt_ref, iw_ref, q_ref, qext_ref, k_ref, kext_ref, v_ref, o_ref, acc_ref, key_ref, bias_ref,
                *, tq, seq, ksel, col_step):
    first_row = pl.program_id(1) * tq
    for g in range(seq // col_step):
        @pl.when(first_row // col_step == g)
        def _(width=(g + 1) * col_step):
            _dsa_body(iq_ref, ikt_ref, iw_ref, q_ref, qext_ref, k_ref, kext_ref, v_ref, o_ref, acc_ref, key_ref,
                      bias_ref, tq=tq, ksel=ksel, width=width, first_row=first_row)


def _dsa_body(iq_ref, ikt_ref, iw_ref, q_ref, qext_ref, k_ref, kext_ref, v_ref, o_ref, acc_ref, key_ref, bias_ref,
              *, tq, ksel, width, first_row):
    seq = width
    t0 = first_row
    acc_ref, key_ref, bias_ref = acc_ref.at[:, :width], key_ref.at[:, :width], bias_ref.at[:, :width]
    ikt = ikt_ref[0, :, :width]
    ik_hi = ikt.astype(BF16)
    ik_lo = (ikt - ik_hi.astype(F32)).astype(BF16)
    rhs = jnp.concatenate([ik_hi, ik_hi, ik_lo, ik_lo], axis=0)
    lane = lax.broadcasted_iota(jnp.int32, (tq, LANES), 1)
    w_all = iw_ref[...] * (IDX_HEADS ** -0.5 * IDX_DIM ** -0.5)
    for h in range(IDX_HEADS):
        if h % 2 == 0:
            x = iq_ref[:, (h // 2) * LANES:(h // 2 + 1) * LANES]
            x_hi = x.astype(BF16).astype(F32)
            x_lo = x - x_hi
            u = jnp.where(lane < IDX_DIM, x_hi, pltpu.roll(x_lo, IDX_DIM, 1)).astype(BF16)
        else:
            u = jnp.where(lane < IDX_DIM, pltpu.roll(x_hi, IDX_DIM, 1), x_lo).astype(BF16)
        s = jnp.dot(jnp.concatenate([u, u], axis=1), rhs, preferred_element_type=F32)
        term = jnp.maximum(s, 0.0) * w_all[:, IDX_DIM + h:IDX_DIM + h + 1]
        if h == 0:
            acc_ref[...] = term
        else:
            acc_ref[...] += term

    col = lax.broadcasted_iota(jnp.int32, (tq, seq), 1)
    row = t0 + lax.broadcasted_iota(jnp.int32, (tq, seq), 0)
    vis = col <= row
    bits = pltpu.bitcast(acc_ref[...], jnp.int32)
    key = bits ^ ((bits >> 31) & 0x7FFFFFFF)
    key_ref[...] = jnp.where(vis, key, INT_MIN)

    rg = tq // DSA_ROW_GROUPS

    def bisect(it, cs):
        bit = jnp.left_shift(jnp.int32(1), 31 - it)
        out = []
        for gi, c in enumerate(cs):
            cand = c | bit
            keys = key_ref[gi * rg:(gi + 1) * rg, :]
            cnt = jnp.sum((keys >= (cand ^ INT_MIN)).astype(jnp.int32), axis=1, keepdims=True)
            out.append(jnp.where(cnt >= ksel, cand, c))
        return tuple(out)

    cs = lax.fori_loop(0, 32, bisect, tuple(jnp.zeros((rg, 1), jnp.int32) for _ in range(DSA_ROW_GROUPS)))
    thr = jnp.concatenate(cs, axis=0) ^ INT_MIN
    bias_ref[...] = jnp.where(vis, jnp.where(key_ref[...] >= thr, 0.0, NEG_BIG), NEG_BIG)

    kext = kext_ref[:width, :]
    for h in range(AT_HEADS):
        hs = slice(h * AT_HEAD_DIM, (h + 1) * AT_HEAD_DIM)
        qa = jnp.concatenate([q_ref[:, hs], qext_ref[:, hs]], axis=1)
        ka = jnp.concatenate([k_ref[:width, hs], kext], axis=1)
        logit = lax.dot_general(qa, ka, _NT, preferred_element_type=F32) + bias_ref[...]
        m = jnp.max(logit, axis=1, keepdims=True)
        p = jnp.exp(logit - m)
        den = jnp.sum(p, axis=1, keepdims=True)
        oh = jnp.dot(p.astype(BF16), v_ref[:width, hs], preferred_element_type=F32) / den
        o_ref[:, hs] = oh.astype(o_ref.dtype)


def _dsa(ziq, ikt, ikw, zat, batch, seq, tq=DSA_TQ):
    nq = seq // tq
    ksel = min(TOPK_MAX, seq // 4)
    col_step = max(seq // DSA_WIDTH_VARIANTS, tq)
    qext, kext = _alibi_tables(seq)
    return pl.pallas_call(
        functools.partial(_dsa_kernel, tq=tq, seq=seq, ksel=ksel, col_step=col_step),
        grid=(batch, nq),
        in_specs=[pl.BlockSpec((tq, IDX_HEADS * IDX_DIM), lambda b, i: (b * nq + i, 0)),
                  pl.BlockSpec((1, IDX_DIM, seq), lambda b, i: (b, 0, 0)),
                  pl.BlockSpec((tq, LANES), lambda b, i: (b * nq + i, 0)),
                  pl.BlockSpec((tq, AT_WIDTH), lambda b, i: (b * nq + i, 0)),
                  pl.BlockSpec((tq, AT_WIDTH), lambda b, i: (i, 0)),
                  pl.BlockSpec((seq, AT_WIDTH), lambda b, i: (b, 1)),
                  pl.BlockSpec((seq, AT_HEAD_DIM), lambda b, i: (0, 0)),
                  pl.BlockSpec((seq, AT_WIDTH), lambda b, i: (b, 2))],
        out_specs=pl.BlockSpec((tq, AT_WIDTH), lambda b, i: (b * nq + i, 0)),
        out_shape=jax.ShapeDtypeStruct((batch * seq, AT_WIDTH), BF16),
        scratch_shapes=[pltpu.VMEM((tq, seq), F32), pltpu.VMEM((tq, seq), jnp.int32),
                        pltpu.VMEM((tq, seq), F32)],
        compiler_params=_params(("parallel", "arbitrary"), 48),
        name="dsa",
    )(ziq, ikt, ikw, zat, qext, zat, kext, zat)


def _merge_kernel(a1_ref, w1_ref, a2_ref, w2_ref, g1_ref, g2_ref, o_ref):
    y1 = jnp.dot(a1_ref[...], w1_ref[...], preferred_element_type=F32)
    y2 = jnp.dot(a2_ref[...], w2_ref[...], preferred_element_type=F32)
    o_ref[...] = (g1_ref[...].astype(F32) * y1 + g2_ref[...].astype(F32) * y2).astype(o_ref.dtype)


def _merge(o_hg, w_hg, o_at, w_at, gates, tm=1024, tn=512):
    m, k1 = o_hg.shape
    k2 = o_at.shape[1]
    n = w_hg.shape[1]
    tm = min(tm, m)
    nb = n // tn
    return pl.pallas_call(
        _merge_kernel,
        grid=(m // tm, nb),
        in_specs=[pl.BlockSpec((tm, k1), lambda i, j: (i, 0)), pl.BlockSpec((k1, tn), lambda i, j: (0, j)),
                  pl.BlockSpec((tm, k2), lambda i, j: (i, 0)), pl.BlockSpec((k2, tn), lambda i, j: (0, j)),
                  pl.BlockSpec((tm, tn), lambda i, j: (i, j)), pl.BlockSpec((tm, tn), lambda i, j: (i, nb + j))],
        out_specs=pl.BlockSpec((tm, tn), lambda i, j: (i, j)),
        out_shape=jax.ShapeDtypeStruct((m, n), BF16),
        compiler_params=_params(("parallel", "arbitrary"), 48),
        name="merge",
    )(o_hg, w_hg, o_at, w_at, gates, gates)


def _outproj_kernel(m_ref, w_ref, x_ref, gn_ref, wr_ref, br_ref, x1_ref, hp_ref, ri_ref, rw_ref):
    x1 = x_ref[...] + jnp.dot(m_ref[...], w_ref[...], preferred_element_type=F32)
    x1_ref[...] = x1
    hn = _rms(x1, gn_ref[...])
    hp_ref[...] = _pack_bf16_halves(hn)
    h_hi = hn.astype(BF16)
    h_lo = (hn - h_hi.astype(F32)).astype(BF16)
    logit = jnp.dot(jnp.concatenate([h_hi, h_hi, h_lo], axis=1), wr_ref[...],
                    preferred_element_type=F32) + br_ref[...]
    lane = lax.broadcasted_iota(jnp.int32, logit.shape, 1)
    lane_f = lane.astype(F32)
    is_g = lane < N_GROUPS
    lg = jnp.where(is_g, logit, -jnp.inf)
    mg = jnp.max(lg, axis=1, keepdims=True)
    pg_top = 1.0 / jnp.sum(jnp.exp(lg - mg), axis=1, keepdims=True)
    gid = jnp.min(jnp.where(lg == mg, lane_f, float(LANES)), axis=1, keepdims=True).astype(jnp.int32)
    in_grp = ((lane - N_GROUPS) >> 3) == gid
    le = jnp.where(in_grp, logit, -jnp.inf)
    m1 = jnp.max(le, axis=1, keepdims=True)
    i1 = jnp.min(jnp.where(le == m1, lane_f, float(LANES)), axis=1, keepdims=True)
    le2 = jnp.where(lane_f == i1, -jnp.inf, le)
    m2 = jnp.max(le2, axis=1, keepdims=True)
    i2 = jnp.min(jnp.where(le2 == m2, lane_f, float(LANES)), axis=1, keepdims=True)
    r = jnp.exp(m2 - m1)
    w1 = pg_top / (1.0 + r)
    w2 = pg_top * r / (1.0 + r)
    e1 = i1.astype(jnp.int32) - N_GROUPS
    e2 = i2.astype(jnp.int32) - N_GROUPS
    ri_ref[...] = jnp.where(lane == 0, e1, jnp.where(lane == 1, e2, 0))
    rw_ref[...] = jnp.where(lane == 0, w1, jnp.where(lane == 1, w2, 0.0))


def _outproj(mixed, w_out, x, gn, w_r, b_r, tm=256):
    m, d = x.shape
    row = lambda w: pl.BlockSpec((tm, w), lambda i: (i, 0))
    const = lambda a: pl.BlockSpec(a.shape, lambda i: (0, 0), pipeline_mode=pl.Buffered(1))
    gn = gn.reshape(1, d)
    b_r = b_r.reshape(1, LANES)
    return pl.pallas_call(
        _outproj_kernel,
        grid=(m // tm,),
        in_specs=[row(mixed.shape[1]), const(w_out), row(d), const(gn), const(w_r), const(b_r)],
        out_specs=[row(d), row(d // 2), row(LANES), row(LANES)],
        out_shape=[jax.ShapeDtypeStruct((m, d), F32), jax.ShapeDtypeStruct((m, d // 2), jnp.int32),
                   jax.ShapeDtypeStruct((m, LANES), jnp.int32), jax.ShapeDtypeStruct((m, LANES), F32)],
        compiler_params=_params(("parallel",), 56),
        name="outproj_router",
    )(mixed, w_out, x, gn, w_r, b_r)


def _expert_kernel(te_ref, tv_ref, src_ref, dst_ref, hn_hbm, wg_ref, wu_ref, wd_ref, y_hbm,
                   xbuf, ybuf, xb_ref, wgb, wub, wdb, gsem, ssem, *, tm):
    t = pl.program_id(0)
    slot = lax.rem(t, 2)
    other = 1 - slot
    valid = tv_ref[t] > 0
    prev_valid = jnp.logical_and(t >= 1, tv_ref[jnp.maximum(t - 1, 0)] > 0)

    def gather_copy(tile, r, s):
        tok = src_ref[tile * tm + r]
        return pltpu.make_async_copy(hn_hbm.at[pl.ds(tok, 1), :], xbuf.at[s, pl.ds(r, 1), :], gsem.at[s])

    def scatter_copy(tile, r, s):
        dst = dst_ref[(tile + 1) * tm + r]
        return pltpu.make_async_copy(ybuf.at[s, pl.ds(r, 1), :], y_hbm.at[pl.ds(dst, 1), :], ssem.at[s])

    def start_rows(copy_fn, tile, s):
        def body(r, carry):
            copy_fn(tile, r, s).start()
            return carry

        lax.fori_loop(0, tm, body, 0, unroll=8)

    def gather_wait(s):
        pltpu.make_async_copy(hn_hbm.at[pl.ds(0, tm), :], xbuf.at[s], gsem.at[s]).wait()

    def scatter_wait(s):
        pltpu.make_async_copy(ybuf.at[s], y_hbm.at[pl.ds(0, tm), :], ssem.at[s]).wait()

    @pl.when(t == 0)
    def _():
        start_rows(gather_copy, 0, 0)
        ybuf[1] = jnp.zeros(ybuf.shape[1:], jnp.int32)

    @pl.when(prev_valid)
    def _():
        scatter_wait(slot)

    @pl.when(valid)
    def _():
        gather_wait(slot)

        @pl.when(jnp.logical_or(t == 0, te_ref[t] != te_ref[jnp.maximum(t - 1, 0)]))
        def _():
            wgb[...] = wg_ref[0].astype(BF16)
            wub[...] = wu_ref[0].astype(BF16)
            wdb[...] = wd_ref[0].astype(BF16)

        xb_ref[...] = _unpack_bf16_halves(xbuf[slot]).astype(BF16)
        for r in range(tm):
            gather_copy(t + 1, r, other).start()
        for r in range(tm):
            scatter_copy(t - 1, r, other).start()
        xb = xb_ref[...]
        g = jnp.dot(xb, wgb[...], preferred_element_type=F32)
        u = jnp.dot(xb, wub[...], preferred_element_type=F32)
        hid = (g * _sigmoid(g) * u).astype(BF16)
        ybuf[slot] = _pack_bf16_halves(jnp.dot(hid, wdb[...], preferred_element_type=F32))

    @pl.when(jnp.logical_and(jnp.logical_not(valid), prev_valid))
    def _():
        gather_wait(slot)
        start_rows(scatter_copy, t - 1, other)
        scatter_wait(other)


def _experts(hn, tile_expert, tile_valid, src, dst, w_g, w_u, w_d, tm):
    n_tok, dp = hn.shape
    d = 2 * dp
    n_steps = tile_expert.shape[0]
    wspec = lambda a: pl.BlockSpec((1,) + a.shape[1:], lambda t, te, tv, sr, ds: (te[t], 0, 0))
    return pl.pallas_call(
        functools.partial(_expert_kernel, tm=tm),
        grid_spec=pltpu.PrefetchScalarGridSpec(
            num_scalar_prefetch=4,
            grid=(n_steps,),
            in_specs=[pl.BlockSpec(memory_space=pl.ANY), wspec(w_g), wspec(w_u), wspec(w_d)],
            out_specs=pl.BlockSpec(memory_space=pl.ANY),
            scratch_shapes=[pltpu.VMEM((2, tm, dp), jnp.int32), pltpu.VMEM((2, tm, dp), jnp.int32),
                            pltpu.VMEM((tm, d), BF16),
                            pltpu.VMEM(w_g.shape[1:], BF16), pltpu.VMEM(w_u.shape[1:], BF16),
                            pltpu.VMEM(w_d.shape[1:], BF16),
                            pltpu.SemaphoreType.DMA((2,)), pltpu.SemaphoreType.DMA((2,))]),
        out_shape=jax.ShapeDtypeStruct((2 * n_tok + tm, dp), jnp.int32),
        compiler_params=_params(("arbitrary",), 56),
        name="experts",
    )(tile_expert, tile_valid, src, dst, hn, w_g, w_u, w_d)


def _dispatch_plan(route_e, tm):
    n_tok = route_e.shape[0]
    n_pairs = 2 * n_tok
    n_tiles = n_pairs // tm + N_EXPERTS + 1
    eflat = route_e.T.reshape(-1)
    order = jnp.argsort(eflat, stable=True).astype(jnp.int32)
    counts = jnp.sum(eflat[:, None] == jnp.arange(N_EXPERTS, dtype=jnp.int32)[None, :], axis=0, dtype=jnp.int32)
    tiles_e = (counts + tm - 1) // tm
    tile_end = jnp.cumsum(tiles_e)
    tile_off = tile_end - tiles_e
    start = jnp.cumsum(counts) - counts
    tid = jnp.arange(n_tiles, dtype=jnp.int32)
    te = jnp.minimum(jnp.sum(tile_end[None, :] <= tid[:, None], axis=1), N_EXPERTS - 1).astype(jnp.int32)
    tv = jnp.clip(counts[te] - (tid - tile_off[te]) * tm, 0, tm)
    used = tid < tile_end[-1]
    last_e = te[jnp.maximum(tile_end[-1] - 1, 0)]
    te = jnp.where(used, te, last_e).astype(jnp.int32)
    tv = jnp.where(used, tv, 0).astype(jnp.int32)
    r = jnp.arange(tm, dtype=jnp.int32)[None, :]
    src = (start[te] + (tid - tile_off[te]) * tm)[:, None] + r
    dst = jnp.where(r < tv[:, None], order[jnp.clip(src, 0, n_pairs - 1)], n_pairs + r).reshape(-1).astype(jnp.int32)
    spare = n_pairs + jnp.arange(tm, dtype=jnp.int32)
    return te, tv, dst % n_tok, jnp.concatenate([spare, dst])


def _final_kernel(x1_ref, y0_ref, y1_ref, rw_ref, p_ref, wgate_ref, wproj_ref, gple_ref, gfin_ref, o_ref,
                  *, last_layer):
    rw = rw_ref[...]
    x2 = (x1_ref[...] + rw[:, 0:1] * _unpack_bf16_halves(y0_ref[...])
          + rw[:, 1:2] * _unpack_bf16_halves(y1_ref[...]))
    hn = _rms(x2, gple_ref[...]).astype(BF16)
    gate = _sigmoid(jnp.dot(hn, wgate_ref[...], preferred_element_type=F32))
    emb = jnp.dot(p_ref[...].astype(BF16), wproj_ref[...], preferred_element_type=F32)
    x3 = x2 + gate * emb
    o_ref[...] = _rms(x3, gfin_ref[...]) if last_layer else x3


def _final(x1, y, rw, p, w_gate, w_proj, g_ple, g_fin, last_layer, tm=512):
    m, d = x1.shape
    nb = m // tm
    row = lambda w: pl.BlockSpec((tm, w), lambda i: (i, 0))
    const = lambda a: pl.BlockSpec(a.shape, lambda i: (0, 0), pipeline_mode=pl.Buffered(1))
    g_ple, g_fin = g_ple.reshape(1, d), g_fin.reshape(1, d)
    return pl.pallas_call(
        functools.partial(_final_kernel, last_layer=last_layer),
        grid=(nb,),
        in_specs=[row(d), row(d // 2), pl.BlockSpec((tm, d // 2), lambda i: (nb + i, 0)), row(LANES), row(p.shape[1]),
                  const(w_gate), const(w_proj), const(g_ple), const(g_fin)],
        out_specs=row(d),
        out_shape=jax.ShapeDtypeStruct((m, d), F32),
        compiler_params=_params(("parallel",), 56),
        name="final",
    )(x1, y, y, rw, p, w_gate, w_proj, g_ple, g_fin)


def kernel(x, p, attn_norm, w_in, b_in, hg_lb_logits, hg_out_norm, w_hg_up, w_attn_up, w_out, ffn_norm,
           w_router_group, b_router_group, w_router_expert, b_router_expert, w_exp_gate, w_exp_up,
           w_exp_down, ple_norm, w_ple_gate, w_ple_proj, final_norm):
    batch, seq, d = x.shape
    n = batch * seq
    depth = w_in.shape[0]
    o_hf, o_at = 0, 4 * HG_WIDTH
    o_iq = o_at + 3 * AT_WIDTH
    o_ik = o_iq + IDX_HEADS * IDX_DIM
    o_iw = o_ik + IDX_DIM
    o_g = o_iw + IDX_HEADS
    lbs = jnp.cumsum(jax.nn.softmax(hg_lb_logits.astype(F32), axis=0), axis=0)
    xf = x.reshape(n, d)
    moe_tm = min(256, n // 2)
    for li in range(depth):
        wt, b = w_in[li].T.astype(BF16), b_in[li]
        h = _rmsnorm(xf, attn_norm[li], BF16)
        zhg = _matmul(h, wt, o_hf, o_at - o_hf, b[o_hf:o_at], F32, name="proj_hgrn")
        qkv_scale = jnp.concatenate([jnp.full((AT_WIDTH,), AT_HEAD_DIM ** -0.5, F32), jnp.ones((2 * AT_WIDTH,), F32)])
        zat = _matmul(h, wt, o_at, o_iq - o_at, b[o_at:o_iq] * qkv_scale, BF16, scale=qkv_scale, name="proj_attn")
        ziq = _matmul(h, wt, o_iq, o_ik - o_iq, b[o_iq:o_ik], F32, name="proj_iq")
        zikw = _matmul(h, wt, o_ik, LANES, b[o_ik:o_ik + LANES], F32, name="proj_ikw")
        gates = _matmul(h, wt, o_g, 2 * d, b[o_g:], BF16, act="sigmoid", name="proj_gates")

        o_hg = _hgrn(zhg, lbs[li], hg_out_norm[li], batch, seq)
        ikt = zikw[:, :IDX_DIM].reshape(batch, seq, IDX_DIM).transpose(0, 2, 1)
        o_att = _dsa(ziq, ikt, zikw, zat, batch, seq)
        mixed = _merge(o_hg, w_hg_up[li].astype(BF16), o_att, w_attn_up[li].astype(BF16), gates)

        w_r = jnp.pad(jnp.concatenate([w_router_group[li], w_router_expert[li]], axis=1),
                      ((0, 0), (0, LANES - N_GROUPS - N_EXPERTS)))
        b_r = jnp.pad(jnp.concatenate([b_router_group[li], b_router_expert[li]]), (0, LANES - N_GROUPS - N_EXPERTS))
        w_r_hi = w_r.astype(BF16)
        w_r_lo = (w_r - w_r_hi.astype(F32)).astype(BF16)
        w_r = jnp.concatenate([w_r_hi, w_r_lo, w_r_hi], axis=0)
        x1, hn, route_e, route_w = _outproj(mixed, w_out[li].astype(BF16), xf, ffn_norm[li], w_r, b_r)
        tile_e, tile_v, src, dst = _dispatch_plan(route_e[:, :2], moe_tm)
        y = _experts(hn, tile_e, tile_v, src, dst, w_exp_gate[li], w_exp_up[li], w_exp_down[li], moe_tm)
        xf = _final(x1, y, route_w, p[li].reshape(n, -1), w_ple_gate[li].astype(BF16),
                    w_ple_proj[li].astype(BF16), ple_norm[li], final_norm, li == depth - 1)
    return xf.reshape(batch, seq, d)
```

```python
import functools

import numpy as np
import jax
import jax.numpy as jnp
from jax import lax
from jax.experimental import pallas as pl
from jax.experimental.pallas import tpu as pltpu

F32 = jnp.float32
BF16 = jnp.bfloat16
HIGHEST = lax.Precision.HIGHEST

HG_HEADS = 8
HG_DK = 128
HG_DV = 128
HG_WIDTH = HG_HEADS * HG_DK
HG_CHUNK = 64
AT_HEADS = 8
AT_HEAD_DIM = 128
AT_WIDTH = AT_HEADS * AT_HEAD_DIM
IDX_HEADS = 16
IDX_DIM = 64
TOPK_MAX = 256
N_GROUPS = 4
EXPERTS_PER_GROUP = 8
N_EXPERTS = N_GROUPS * EXPERTS_PER_GROUP
D_EXPERT = 512
RMS_EPS = 1e-6
LANES = 128
INT_MIN = -(2 ** 31)
NEG_BIG = -1e30

_NT = (((1,), (1,)), ((), ()))
_TN = (((0,), (0,)), ((), ()))


def _params(sem, vmem_mb=None):
    kw = dict(dimension_semantics=sem)
    if vmem_mb is not None:
        kw["vmem_limit_bytes"] = vmem_mb * 1024 * 1024
    return pltpu.CompilerParams(**kw)


def _sigmoid(x):
    return 1.0 / (1.0 + jnp.exp(-x))


def _pack_bf16_halves(x):
    c = x.shape[1] // 2
    xr = x.astype(BF16).astype(F32)
    hi = pltpu.bitcast(xr[:, :c], jnp.int32)
    lo = lax.shift_right_logical(pltpu.bitcast(xr[:, c:], jnp.int32), 16)
    return hi | lo


def _unpack_bf16_halves(w):
    hi = pltpu.bitcast(w & -65536, F32)
    lo = pltpu.bitcast(lax.shift_left(w, 16), F32)
    return jnp.concatenate([hi, lo], axis=1)


def _rms(x, g):
    ms = jnp.mean(x * x, axis=-1, keepdims=True)
    return x * lax.rsqrt(ms + RMS_EPS) * g


def _rmsnorm_kernel(x_ref, g_ref, o_ref):
    o_ref[...] = _rms(x_ref[...], g_ref[...]).astype(o_ref.dtype)


def _rmsnorm(x, g, out_dtype, tm=512):
    m, d = x.shape
    return pl.pallas_call(
        _rmsnorm_kernel,
        grid=(m // tm,),
        in_specs=[pl.BlockSpec((tm, d), lambda i: (i, 0)), pl.BlockSpec((1, d), lambda i: (0, 0))],
        out_specs=pl.BlockSpec((tm, d), lambda i: (i, 0)),
        out_shape=jax.ShapeDtypeStruct((m, d), out_dtype),
        compiler_params=_params(("parallel",)),
        name="rmsnorm",
    )(x, g.reshape(1, d))


def _mm_kernel(a_ref, wt_ref, b_ref, o_ref, *, act):
    acc = lax.dot_general(a_ref[...], wt_ref[...], _NT, preferred_element_type=F32) + b_ref[...]
    if act == "sigmoid":
        acc = _sigmoid(acc)
    o_ref[...] = acc.astype(o_ref.dtype)


def _mm_scaled_kernel(a_ref, wt_ref, s_ref, b_ref, o_ref):
    acc = lax.dot_general(a_ref[...], wt_ref[...], _NT, preferred_element_type=F32)
    o_ref[...] = (acc * s_ref[...] + b_ref[...]).astype(o_ref.dtype)


def _matmul(a, wt, row0, n, b, out_dtype, act=None, scale=None, tm=1024, tn=1024, name="proj"):
    m, k = a.shape
    tm, tn = min(tm, m), min(tn, n)
    row = lambda: pl.BlockSpec((1, tn), lambda i, j: (0, j))
    scaled = scale is not None
    return pl.pallas_call(
        _mm_scaled_kernel if scaled else functools.partial(_mm_kernel, act=act),
        grid=(m // tm, n // tn),
        in_specs=[pl.BlockSpec((tm, k), lambda i, j: (i, 0)),
                  pl.BlockSpec((pl.Element(tn), pl.Element(k)),
                               lambda i, j: (pl.multiple_of(row0 + j * tn, 16), 0))]
                 + ([row()] if scaled else []) + [row()],
        out_specs=pl.BlockSpec((tm, tn), lambda i, j: (i, j)),
        out_shape=jax.ShapeDtypeStruct((m, n), out_dtype),
        compiler_params=_params(("parallel", "arbitrary"), 48),
        name=name,
    )(a, wt, *([scale.reshape(1, n)] if scaled else []), b.reshape(1, n))


HG_BLOCK = HG_CHUNK
HG_GROUP = 8
HG_ROWS = 512


def _hgrn_tables():
    c = HG_BLOCK
    levels = [c >> i for i in range(int(np.log2(c)))]
    t = np.arange(c)
    u = np.arange(c)
    rall = [(u[None, :] <= t[:, None]), (u[None, :] > t[:, None])]
    low, mask = [], []
    for p in levels:
        half = p // 2
        m = (t // p) * p + half
        lower = (t % p) >= half
        r = np.where(lower[:, None], (u[None, :] > m[:, None]) & (u[None, :] <= t[:, None]),
                     (u[None, :] > t[:, None]) & (u[None, :] <= m[:, None]))
        rall.append(r)
        low.append(np.broadcast_to(lower[:, None], (c, LANES)))
        mask.append(((t[:, None] // p) == (t[None, :] // p)) & lower[:, None] & ~lower[None, :])
    mask.append(np.eye(c, dtype=bool))
    rall = np.concatenate(rall, 0).astype(np.float32)
    return (np.tile(rall, (1, 3)), np.stack(low).astype(np.float32), np.stack(mask).astype(np.float32),
            len(levels))


def _hgrn_kernel(q_ref, f_ref, i_ref, g_ref, lb_ref, gn_ref, rall_ref, low_ref, mask_ref, o_ref, st_ref,
                 *, n_chunks, n_levels, heads):
    c = HG_BLOCK

    @pl.when(pl.program_id(2) == 0)
    def _():
        st_ref[...] = jnp.zeros_like(st_ref)

    lb = lb_ref[...]
    one_m_lb = 1.0 - lb
    gn = gn_ref[...]

    def body(ci, carry):
        rows = pl.ds(pl.multiple_of(ci * c, c), c)
        z = f_ref[rows, :]
        q = q_ref[rows, :]
        e = jnp.exp(-jnp.abs(z))
        r = 1.0 / (1.0 + e)
        pos = z >= 0
        sig = jnp.where(pos, r, e * r)
        nsig = jnp.where(pos, e * r, r)
        logf = jnp.log(lb + one_m_lb * sig)
        k = one_m_lb * nsig
        qs = q * _sigmoid(q)
        l1 = logf.astype(BF16)
        r1 = logf - l1.astype(F32)
        l2 = r1.astype(BF16)
        l3 = (r1 - l2.astype(F32)).astype(BF16)
        ex = jnp.exp(jnp.dot(rall_ref[...], jnp.concatenate([l1, l2, l3], axis=0), preferred_element_type=F32))
        vb = i_ref[rows, :].astype(BF16)
        gate = g_ref[rows, :]
        gate = gate * _sigmoid(gate)
        for g in range(heads):
            sl = slice(g * HG_DK, (g + 1) * HG_DK)
            qg, kg, vg = qs[:, sl], k[:, sl], vb[:, sl]
            eb = ex[0:c, sl]
            a = mask_ref[n_levels] * lax.dot_general(qg.astype(BF16), kg.astype(BF16), _NT,
                                                     preferred_element_type=F32)
            for li in range(n_levels):
                w = (jnp.where(low_ref[li] != 0.0, qg, kg) * ex[(2 + li) * c:(3 + li) * c, sl]).astype(BF16)
                a = a + mask_ref[li] * lax.dot_general(w, w, _NT, preferred_element_type=F32)
            st = st_ref[g]
            o = (jnp.dot(a.astype(BF16), vg, preferred_element_type=F32)
                 + lax.dot_general((qg * eb).astype(BF16), st.astype(BF16), _NT, preferred_element_type=F32))
            st_ref[g] = st * eb[c - 1:c, :] + lax.dot_general(vg, (kg * ex[c:2 * c, sl]).astype(BF16), _TN,
                                                              preferred_element_type=F32)
            o_ref[rows, sl] = (_rms(o, gn) * gate[:, sl]).astype(o_ref.dtype)
        return carry

    lax.fori_loop(0, n_chunks, body, 0, unroll=2)


def _hgrn(zhg, lb, gn, batch, seq):
    rall, low, mask, n_levels = _hgrn_tables()
    hb = HG_HEADS // HG_GROUP
    gw = HG_GROUP * HG_DK
    tc = min(HG_ROWS, seq)
    nt = seq // tc
    col = lambda off: pl.BlockSpec((tc, gw), lambda b, hh, tt: (b * nt + tt, off + hh))
    full = lambda a: pl.BlockSpec(a.shape, lambda b, hh, tt: (0,) * a.ndim)
    return pl.pallas_call(
        functools.partial(_hgrn_kernel, n_chunks=tc // HG_BLOCK, n_levels=n_levels, heads=HG_GROUP),
        grid=(batch, hb, nt),
        in_specs=[col(0), col(hb), col(2 * hb), col(3 * hb),
                  pl.BlockSpec((1, gw), lambda b, hh, tt: (0, hh)),
                  pl.BlockSpec((1, HG_DV), lambda b, hh, tt: (0, 0)),
                  full(rall), full(low), full(mask)],
        out_specs=pl.BlockSpec((tc, gw), lambda b, hh, tt: (b * nt + tt, hh)),
        out_shape=jax.ShapeDtypeStruct((batch * seq, HG_HEADS * HG_DV), BF16),
        scratch_shapes=[pltpu.VMEM((HG_GROUP, HG_DV, HG_DK), F32)],
        compiler_params=_params(("parallel", "parallel", "arbitrary"), 48),
        name="hgrn2",
    )(zhg, zhg, zhg, zhg, lb.reshape(1, HG_WIDTH), gn.reshape(1, HG_DV),
      jnp.asarray(rall, BF16), jnp.asarray(low), jnp.asarray(mask))


DSA_TQ = 256
DSA_ROW_GROUPS = 4
DSA_WIDTH_VARIANTS = 4


def _alibi_tables(seq):
    assert 8 % AT_HEADS == 0 and seq <= 64 * 256
    pos = jnp.arange(seq, dtype=jnp.int32)
    hi, lo = (pos // 64).astype(F32), (pos % 64).astype(F32)
    one = jnp.ones((seq,), F32)
    kext = jnp.stack([one, one, hi, lo], axis=1)
    kext = jnp.pad(kext, ((0, 0), (0, AT_HEAD_DIM - 4))).astype(BF16)
    qext = []
    for h in range(AT_HEADS):
        slope = 2.0 ** (-8.0 * (h + 1) / AT_HEADS)
        cols = jnp.stack([-slope * 64.0 * hi, -slope * lo, slope * 64.0 * one, slope * one], axis=1)
        qext.append(jnp.pad(cols, ((0, 0), (0, AT_HEAD_DIM - 4))))
    return jnp.concatenate(qext, axis=1).astype(BF16), kext


def _dsa_kernel(iq_ref, ikt_ref, iw_ref, q_ref, qext_ref, k_ref, kext_ref, v_ref, o_ref, acc_ref, key_ref, bias_ref,
                *, tq, seq, ksel, col_step):
    first_row = pl.program_id(1) * tq
    for g in range(seq // col_step):
        @pl.when(first_row // col_step == g)
        def _(width=(g + 1) * col_step):
            _dsa_body(iq_ref, ikt_ref, iw_ref, q_ref, qext_ref, k_ref, kext_ref, v_ref, o_ref, acc_ref, key_ref,
                      bias_ref, tq=tq, ksel=ksel, width=width, first_row=first_row)


def _dsa_body(iq_ref, ikt_ref, iw_ref, q_ref, qext_ref, k_ref, kext_ref, v_ref, o_ref, acc_ref, key_ref, bias_ref,
              *, tq, ksel, width, first_row):
    seq = width
    t0 = first_row
    acc_ref, key_ref, bias_ref = acc_ref.at[:, :width], key_ref.at[:, :width], bias_ref.at[:, :width]
    ikt = ikt_ref[0, :, :width]
    ik_hi = ikt.astype(BF16)
    ik_lo = (ikt - ik_hi.astype(F32)).astype(BF16)
    rhs = jnp.concatenate([ik_hi, ik_hi, ik_lo, ik_lo], axis=0)
    lane = lax.broadcasted_iota(jnp.int32, (tq, LANES), 1)
    w_all = iw_ref[...] * (IDX_HEADS ** -0.5 * IDX_DIM ** -0.5)
    for h in range(IDX_HEADS):
        if h % 2 == 0:
            x = iq_ref[:, (h // 2) * LANES:(h // 2 + 1) * LANES]
            x_hi = x.astype(BF16).astype(F32)
            x_lo = x - x_hi
            u = jnp.where(lane < IDX_DIM, x_hi, pltpu.roll(x_lo, IDX_DIM, 1)).astype(BF16)
        else:
            u = jnp.where(lane < IDX_DIM, pltpu.roll(x_hi, IDX_DIM, 1), x_lo).astype(BF16)
        s = jnp.dot(jnp.concatenate([u, u], axis=1), rhs, preferred_element_type=F32)
        term = jnp.maximum(s, 0.0) * w_all[:, IDX_DIM + h:IDX_DIM + h + 1]
        if h == 0:
            acc_ref[...] = term
        else:
            acc_ref[...] += term

    col = lax.broadcasted_iota(jnp.int32, (tq, seq), 1)
    row = t0 + lax.broadcasted_iota(jnp.int32, (tq, seq), 0)
    vis = col <= row
    bits = pltpu.bitcast(acc_ref[...], jnp.int32)
    key = bits ^ ((bits >> 31) & 0x7FFFFFFF)
    key_ref[...] = jnp.where(vis, key, INT_MIN)

    rg = tq // DSA_ROW_GROUPS

    def bisect(it, cs):
        bit = jnp.left_shift(jnp.int32(1), 31 - it)
        out = []
        for gi, c in enumerate(cs):
            cand = c | bit
            keys = key_ref[gi * rg:(gi + 1) * rg, :]
            cnt = jnp.sum((keys >= (cand ^ INT_MIN)).astype(jnp.int32), axis=1, keepdims=True)
            out.append(jnp.where(cnt >= ksel, cand, c))
        return tuple(out)

    cs = lax.fori_loop(0, 32, bisect, tuple(jnp.zeros((rg, 1), jnp.int32) for _ in range(DSA_ROW_GROUPS)))
    thr = jnp.concatenate(cs, axis=0) ^ INT_MIN
    bias_ref[...] = jnp.where(vis, jnp.where(key_ref[...] >= thr, 0.0, NEG_BIG), NEG_BIG)

    kext = kext_ref[:width, :]
    for h in range(AT_HEADS):
        hs = slice(h * AT_HEAD_DIM, (h + 1) * AT_HEAD_DIM)
        qa = jnp.concatenate([q_ref[:, hs], qext_ref[:, hs]], axis=1)
        ka = jnp.concatenate([k_ref[:width, hs], kext], axis=1)
        logit = lax.dot_general(qa, ka, _NT, preferred_element_type=F32) + bias_ref[...]
        m = jnp.max(logit, axis=1, keepdims=True)
        p = jnp.exp(logit - m)
        den = jnp.sum(p, axis=1, keepdims=True)
        oh = jnp.dot(p.astype(BF16), v_ref[:width, hs], preferred_element_type=F32) / den
        o_ref[:, hs] = oh.astype(o_ref.dtype)


def _dsa(ziq, ikt, ikw, zat, batch, seq, tq=DSA_TQ):
    nq = seq // tq
    ksel = min(TOPK_MAX, seq // 4)
    col_step = max(seq // DSA_WIDTH_VARIANTS, tq)
    qext, kext = _alibi_tables(seq)
    return pl.pallas_call(
        functools.partial(_dsa_kernel, tq=tq, seq=seq, ksel=ksel, col_step=col_step),
        grid=(batch, nq),
        in_specs=[pl.BlockSpec((tq, IDX_HEADS * IDX_DIM), lambda b, i: (b * nq + i, 0)),
                  pl.BlockSpec((1, IDX_DIM, seq), lambda b, i: (b, 0, 0)),
                  pl.BlockSpec((tq, LANES), lambda b, i: (b * nq + i, 0)),
                  pl.BlockSpec((tq, AT_WIDTH), lambda b, i: (b * nq + i, 0)),
                  pl.BlockSpec((tq, AT_WIDTH), lambda b, i: (i, 0)),
                  pl.BlockSpec((seq, AT_WIDTH), lambda b, i: (b, 1)),
                  pl.BlockSpec((seq, AT_HEAD_DIM), lambda b, i: (0, 0)),
                  pl.BlockSpec((seq, AT_WIDTH), lambda b, i: (b, 2))],
        out_specs=pl.BlockSpec((tq, AT_WIDTH), lambda b, i: (b * nq + i, 0)),
        out_shape=jax.ShapeDtypeStruct((batch * seq, AT_WIDTH), BF16),
        scratch_shapes=[pltpu.VMEM((tq, seq), F32), pltpu.VMEM((tq, seq), jnp.int32),
                        pltpu.VMEM((tq, seq), F32)],
        compiler_params=_params(("parallel", "arbitrary"), 48),
        name="dsa",
    )(ziq, ikt, ikw, zat, qext, zat, kext, zat)


def _merge_kernel(a1_ref, w1_ref, a2_ref, w2_ref, g1_ref, g2_ref, o_ref):
    y1 = jnp.dot(a1_ref[...], w1_ref[...], preferred_element_type=F32)
    y2 = jnp.dot(a2_ref[...], w2_ref[...], preferred_element_type=F32)
    o_ref[...] = (g1_ref[...].astype(F32) * y1 + g2_ref[...].astype(F32) * y2).astype(o_ref.dtype)


def _merge(o_hg, w_hg, o_at, w_at, gates, tm=1024, tn=1024):
    m, k1 = o_hg.shape
    k2 = o_at.shape[1]
    n = w_hg.shape[1]
    tm = min(tm, m)
    nb = n // tn
    return pl.pallas_call(
        _merge_kernel,
        grid=(m // tm, nb),
        in_specs=[pl.BlockSpec((tm, k1), lambda i, j: (i, 0)), pl.BlockSpec((k1, tn), lambda i, j: (0, j)),
                  pl.BlockSpec((tm, k2), lambda i, j: (i, 0)), pl.BlockSpec((k2, tn), lambda i, j: (0, j)),
                  pl.BlockSpec((tm, tn), lambda i, j: (i, j)), pl.BlockSpec((tm, tn), lambda i, j: (i, nb + j))],
        out_specs=pl.BlockSpec((tm, tn), lambda i, j: (i, j)),
        out_shape=jax.ShapeDtypeStruct((m, n), BF16),
        compiler_params=_params(("parallel", "arbitrary"), 48),
        name="merge",
    )(o_hg, w_hg, o_at, w_at, gates, gates)


def _outproj_kernel(m_ref, w_ref, x_ref, gn_ref, wr_ref, br_ref, x1_ref, hp_ref, ri_ref, rw_ref):
    x1 = x_ref[...] + jnp.dot(m_ref[...], w_ref[...], preferred_element_type=F32)
    x1_ref[...] = x1
    hn = _rms(x1, gn_ref[...])
    hp_ref[...] = _pack_bf16_halves(hn)
    h_hi = hn.astype(BF16)
    h_lo = (hn - h_hi.astype(F32)).astype(BF16)
    logit = jnp.dot(jnp.concatenate([h_hi, h_hi, h_lo], axis=1), wr_ref[...],
                    preferred_element_type=F32) + br_ref[...]
    lane = lax.broadcasted_iota(jnp.int32, logit.shape, 1)
    lane_f = lane.astype(F32)
    is_g = lane < N_GROUPS
    lg = jnp.where(is_g, logit, -jnp.inf)
    mg = jnp.max(lg, axis=1, keepdims=True)
    pg_top = 1.0 / jnp.sum(jnp.exp(lg - mg), axis=1, keepdims=True)
    gid = jnp.min(jnp.where(lg == mg, lane_f, float(LANES)), axis=1, keepdims=True).astype(jnp.int32)
    in_grp = ((lane - N_GROUPS) >> 3) == gid
    le = jnp.where(in_grp, logit, -jnp.inf)
    m1 = jnp.max(le, axis=1, keepdims=True)
    i1 = jnp.min(jnp.where(le == m1, lane_f, float(LANES)), axis=1, keepdims=True)
    le2 = jnp.where(lane_f == i1, -jnp.inf, le)
    m2 = jnp.max(le2, axis=1, keepdims=True)
    i2 = jnp.min(jnp.where(le2 == m2, lane_f, float(LANES)), axis=1, keepdims=True)
    r = jnp.exp(m2 - m1)
    w1 = pg_top / (1.0 + r)
    w2 = pg_top * r / (1.0 + r)
    e1 = i1.astype(jnp.int32) - N_GROUPS
    e2 = i2.astype(jnp.int32) - N_GROUPS
    ri_ref[...] = jnp.where(lane == 0, e1, jnp.where(lane == 1, e2, 0))
    rw_ref[...] = jnp.where(lane == 0, w1, jnp.where(lane == 1, w2, 0.0))


def _outproj(mixed, w_out, x, gn, w_r, b_r, tm=256):
    m, d = x.shape
    row = lambda w: pl.BlockSpec((tm, w), lambda i: (i, 0))
    const = lambda a: pl.BlockSpec(a.shape, lambda i: (0, 0), pipeline_mode=pl.Buffered(1))
    gn = gn.reshape(1, d)
    b_r = b_r.reshape(1, LANES)
    return pl.pallas_call(
        _outproj_kernel,
        grid=(m // tm,),
        in_specs=[row(mixed.shape[1]), const(w_out), row(d), const(gn), const(w_r), const(b_r)],
        out_specs=[row(d), row(d // 2), row(LANES), row(LANES)],
        out_shape=[jax.ShapeDtypeStruct((m, d), F32), jax.ShapeDtypeStruct((m, d // 2), jnp.int32),
                   jax.ShapeDtypeStruct((m, LANES), jnp.int32), jax.ShapeDtypeStruct((m, LANES), F32)],
        compiler_params=_params(("parallel",), 56),
        name="outproj_router",
    )(mixed, w_out, x, gn, w_r, b_r)


def _expert_kernel(te_ref, tv_ref, src_ref, dst_ref, hn_hbm, wg_ref, wu_ref, wd_ref, y_hbm,
                   xbuf, ybuf, xb_ref, wgb, wub, wdb, gsem, ssem, *, tm):
    t = pl.program_id(0)
    slot = lax.rem(t, 2)
    other = 1 - slot
    valid = tv_ref[t] > 0
    prev_valid = jnp.logical_and(t >= 1, tv_ref[jnp.maximum(t - 1, 0)] > 0)

    def gather_copy(tile, r, s):
        tok = src_ref[tile * tm + r]
        return pltpu.make_async_copy(hn_hbm.at[pl.ds(tok, 1), :], xbuf.at[s, pl.ds(r, 1), :], gsem.at[s])

    def scatter_copy(tile, r, s):
        dst = dst_ref[(tile + 1) * tm + r]
        return pltpu.make_async_copy(ybuf.at[s, pl.ds(r, 1), :], y_hbm.at[pl.ds(dst, 1), :], ssem.at[s])

    def start_rows(copy_fn, tile, s):
        def body(r, carry):
            copy_fn(tile, r, s).start()
            return carry

        lax.fori_loop(0, tm, body, 0, unroll=8)

    def gather_wait(s):
        pltpu.make_async_copy(hn_hbm.at[pl.ds(0, tm), :], xbuf.at[s], gsem.at[s]).wait()

    def scatter_wait(s):
        pltpu.make_async_copy(ybuf.at[s], y_hbm.at[pl.ds(0, tm), :], ssem.at[s]).wait()

    @pl.when(t == 0)
    def _():
        start_rows(gather_copy, 0, 0)
        ybuf[1] = jnp.zeros(ybuf.shape[1:], jnp.int32)

    @pl.when(prev_valid)
    def _():
        scatter_wait(slot)

    @pl.when(valid)
    def _():
        gather_wait(slot)

        @pl.when(jnp.logical_or(t == 0, te_ref[t] != te_ref[jnp.maximum(t - 1, 0)]))
        def _():
            wgb[...] = wg_ref[0].astype(BF16)
            wub[...] = wu_ref[0].astype(BF16)
            wdb[...] = wd_ref[0].astype(BF16)

        xb_ref[...] = _unpack_bf16_halves(xbuf[slot]).astype(BF16)
        for r in range(tm):
            gather_copy(t + 1, r, other).start()
        for r in range(tm):
            scatter_copy(t - 1, r, other).start()
        xb = xb_ref[...]
        g = jnp.dot(xb, wgb[...], preferred_element_type=F32)
        u = jnp.dot(xb, wub[...], preferred_element_type=F32)
        hid = (g * _sigmoid(g) * u).astype(BF16)
        ybuf[slot] = _pack_bf16_halves(jnp.dot(hid, wdb[...], preferred_element_type=F32))

    @pl.when(jnp.logical_and(jnp.logical_not(valid), prev_valid))
    def _():
        gather_wait(slot)
        start_rows(scatter_copy, t - 1, other)
        scatter_wait(other)


def _experts(hn, tile_expert, tile_valid, src, dst, w_g, w_u, w_d, tm):
    n_tok, dp = hn.shape
    d = 2 * dp
    n_steps = tile_expert.shape[0]
    wspec = lambda a: pl.BlockSpec((1,) + a.shape[1:], lambda t, te, tv, sr, ds: (te[t], 0, 0))
    return pl.pallas_call(
        functools.partial(_expert_kernel, tm=tm),
        grid_spec=pltpu.PrefetchScalarGridSpec(
            num_scalar_prefetch=4,
            grid=(n_steps,),
            in_specs=[pl.BlockSpec(memory_space=pl.ANY), wspec(w_g), wspec(w_u), wspec(w_d)],
            out_specs=pl.BlockSpec(memory_space=pl.ANY),
            scratch_shapes=[pltpu.VMEM((2, tm, dp), jnp.int32), pltpu.VMEM((2, tm, dp), jnp.int32),
                            pltpu.VMEM((tm, d), BF16),
                            pltpu.VMEM(w_g.shape[1:], BF16), pltpu.VMEM(w_u.shape[1:], BF16),
                            pltpu.VMEM(w_d.shape[1:], BF16),
                            pltpu.SemaphoreType.DMA((2,)), pltpu.SemaphoreType.DMA((2,))]),
        out_shape=jax.ShapeDtypeStruct((2 * n_tok + tm, dp), jnp.int32),
        compiler_params=_params(("arbitrary",), 56),
        name="experts",
    )(tile_expert, tile_valid, src, dst, hn, w_g, w_u, w_d)


def _dispatch_plan(route_e, tm):
    n_tok = route_e.shape[0]
    n_pairs = 2 * n_tok
    n_tiles = n_pairs // tm + N_EXPERTS + 1
    eflat = route_e.T.reshape(-1)
    order = jnp.argsort(eflat, stable=True).astype(jnp.int32)
    counts = jnp.sum(eflat[:, None] == jnp.arange(N_EXPERTS, dtype=jnp.int32)[None, :], axis=0, dtype=jnp.int32)
    tiles_e = (counts + tm - 1) // tm
    tile_end = jnp.cumsum(tiles_e)
    tile_off = tile_end - tiles_e
    start = jnp.cumsum(counts) - counts
    tid = jnp.arange(n_tiles, dtype=jnp.int32)
    te = jnp.minimum(jnp.sum(tile_end[None, :] <= tid[:, None], axis=1), N_EXPERTS - 1).astype(jnp.int32)
    tv = jnp.clip(counts[te] - (tid - tile_off[te]) * tm, 0, tm)
    used = tid < tile_end[-1]
    last_e = te[jnp.maximum(tile_end[-1] - 1, 0)]
    te = jnp.where(used, te, last_e).astype(jnp.int32)
    tv = jnp.where(used, tv, 0).astype(jnp.int32)
    r = jnp.arange(tm, dtype=jnp.int32)[None, :]
    src = (start[te] + (tid - tile_off[te]) * tm)[:, None] + r
    dst = jnp.where(r < tv[:, None], order[jnp.clip(src, 0, n_pairs - 1)], n_pairs + r).reshape(-1).astype(jnp.int32)
    spare = n_pairs + jnp.arange(tm, dtype=jnp.int32)
    return te, tv, dst % n_tok, jnp.concatenate([spare, dst])


def _final_kernel(x1_ref, y0_ref, y1_ref, rw_ref, p_ref, wgate_ref, wproj_ref, gple_ref, gfin_ref, o_ref,
                  *, last_layer):
    rw = rw_ref[...]
    x2 = (x1_ref[...] + rw[:, 0:1] * _unpack_bf16_halves(y0_ref[...])
          + rw[:, 1:2] * _unpack_bf16_halves(y1_ref[...]))
    hn = _rms(x2, gple_ref[...]).astype(BF16)
    gate = _sigmoid(jnp.dot(hn, wgate_ref[...], preferred_element_type=F32))
    emb = jnp.dot(p_ref[...].astype(BF16), wproj_ref[...], preferred_element_type=F32)
    x3 = x2 + gate * emb
    o_ref[...] = _rms(x3, gfin_ref[...]) if last_layer else x3


def _final(x1, y, rw, p, w_gate, w_proj, g_ple, g_fin, last_layer, tm=512):
    m, d = x1.shape
    nb = m // tm
    row = lambda w: pl.BlockSpec((tm, w), lambda i: (i, 0))
    const = lambda a: pl.BlockSpec(a.shape, lambda i: (0, 0), pipeline_mode=pl.Buffered(1))
    g_ple, g_fin = g_ple.reshape(1, d), g_fin.reshape(1, d)
    return pl.pallas_call(
        functools.partial(_final_kernel, last_layer=last_layer),
        grid=(nb,),
        in_specs=[row(d), row(d // 2), pl.BlockSpec((tm, d // 2), lambda i: (nb + i, 0)), row(LANES), row(p.shape[1]),
                  const(w_gate), const(w_proj), const(g_ple), const(g_fin)],
        out_specs=row(d),
        out_shape=jax.ShapeDtypeStruct((m, d), F32),
        compiler_params=_params(("parallel",), 56),
        name="final",
    )(x1, y, y, rw, p, w_gate, w_proj, g_ple, g_fin)


def kernel(x, p, attn_norm, w_in, b_in, hg_lb_logits, hg_out_norm, w_hg_up, w_attn_up, w_out, ffn_norm,
           w_router_group, b_router_group, w_router_expert, b_router_expert, w_exp_gate, w_exp_up,
           w_exp_down, ple_norm, w_ple_gate, w_ple_proj, final_norm):
    batch, seq, d = x.shape
    n = batch * seq
    depth = w_in.shape[0]
    o_hf, o_at = 0, 4 * HG_WIDTH
    o_iq = o_at + 3 * AT_WIDTH
    o_ik = o_iq + IDX_HEADS * IDX_DIM
    o_iw = o_ik + IDX_DIM
    o_g = o_iw + IDX_HEADS
    lbs = jnp.cumsum(jax.nn.softmax(hg_lb_logits.astype(F32), axis=0), axis=0)
    xf = x.reshape(n, d)
    moe_tm = min(256, n // 2)
    for li in range(depth):
        wt, b = w_in[li].T.astype(BF16), b_in[li]
        h = _rmsnorm(xf, attn_norm[li], BF16)
        zhg = _matmul(h, wt, o_hf, o_at - o_hf, b[o_hf:o_at], F32, name="proj_hgrn")
        qkv_scale = jnp.concatenate([jnp.full((AT_WIDTH,), AT_HEAD_DIM ** -0.5, F32), jnp.ones((2 * AT_WIDTH,), F32)])
        zat = _matmul(h, wt, o_at, o_iq - o_at, b[o_at:o_iq] * qkv_scale, BF16, scale=qkv_scale, name="proj_attn")
        ziq = _matmul(h, wt, o_iq, o_ik - o_iq, b[o_iq:o_ik], F32, name="proj_iq")
        zikw = _matmul(h, wt, o_ik, LANES, b[o_ik:o_ik + LANES], F32, name="proj_ikw")
        gates = _matmul(h, wt, o_g, 2 * d, b[o_g:], BF16, act="sigmoid", name="proj_gates")

        o_hg = _hgrn(zhg, lbs[li], hg_out_norm[li], batch, seq)
        ikt = zikw[:, :IDX_DIM].reshape(batch, seq, IDX_DIM).transpose(0, 2, 1)
        o_att = _dsa(ziq, ikt, zikw, zat, batch, seq)
        mixed = _merge(o_hg, w_hg_up[li].astype(BF16), o_att, w_attn_up[li].astype(BF16), gates)

        w_r = jnp.pad(jnp.concatenate([w_router_group[li], w_router_expert[li]], axis=1),
                      ((0, 0), (0, LANES - N_GROUPS - N_EXPERTS)))
        b_r = jnp.pad(jnp.concatenate([b_router_group[li], b_router_expert[li]]), (0, LANES - N_GROUPS - N_EXPERTS))
        w_r_hi = w_r.astype(BF16)
        w_r_lo = (w_r - w_r_hi.astype(F32)).astype(BF16)
        w_r = jnp.concatenate([w_r_hi, w_r_lo, w_r_hi], axis=0)
        x1, hn, route_e, route_w = _outproj(mixed, w_out[li].astype(BF16), xf, ffn_norm[li], w_r, b_r)
        tile_e, tile_v, src, dst = _dispatch_plan(route_e[:, :2], moe_tm)
        y = _experts(hn, tile_e, tile_v, src, dst, w_exp_gate[li], w_exp_up[li], w_exp_down[li], moe_tm)
        xf = _final(x1, y, route_w, p[li].reshape(n, -1), w_ple_gate[li].astype(BF16),
                    w_ple_proj[li].astype(BF16), ple_norm[li], final_norm, li == depth - 1)
    return xf.reshape(batch, seq, d)
```

```python
import functools

import numpy as np
import jax
import jax.numpy as jnp
from jax import lax
from jax.experimental import pallas as pl
from jax.experimental.pallas import tpu as pltpu

F32 = jnp.float32
BF16 = jnp.bfloat16

HG_HEADS = 8
HG_DK = 128
HG_DV = 128
HG_WIDTH = HG_HEADS * HG_DK
HG_CHUNK = 64
AT_HEADS = 8
AT_HEAD_DIM = 128
AT_WIDTH = AT_HEADS * AT_HEAD_DIM
IDX_HEADS = 16
IDX_DIM = 64
TOPK_MAX = 256
N_GROUPS = 4
EXPERTS_PER_GROUP = 8
N_EXPERTS = N_GROUPS * EXPERTS_PER_GROUP
D_EXPERT = 512
RMS_EPS = 1e-6
LANES = 128
VMEM_MIB = 48
VMEM_MIB_WHOLE_WEIGHTS = 56
INT_MIN = -(2 ** 31)
NEG_BIG = -1e30

_NT = (((1,), (1,)), ((), ()))
_TN = (((0,), (0,)), ((), ()))


def _params(sem, vmem_mb=None):
    kw = dict(dimension_semantics=sem)
    if vmem_mb is not None:
        kw["vmem_limit_bytes"] = vmem_mb * 1024 * 1024
    return pltpu.CompilerParams(**kw)


def _sigmoid(x):
    return 1.0 / (1.0 + jnp.exp(-x))


def _pack_bf16_halves(x):
    c = x.shape[1] // 2
    xr = x.astype(BF16).astype(F32)
    hi = pltpu.bitcast(xr[:, :c], jnp.int32)
    lo = lax.shift_right_logical(pltpu.bitcast(xr[:, c:], jnp.int32), 16)
    return hi | lo


def _unpack_bf16_halves(w):
    hi = pltpu.bitcast(w & -65536, F32)
    lo = pltpu.bitcast(lax.shift_left(w, 16), F32)
    return jnp.concatenate([hi, lo], axis=1)


def _rms(x, g):
    ms = jnp.mean(x * x, axis=-1, keepdims=True)
    return x * lax.rsqrt(ms + RMS_EPS) * g


def _rmsnorm_kernel(x_ref, g_ref, o_ref):
    o_ref[...] = _rms(x_ref[...], g_ref[...]).astype(o_ref.dtype)


def _rmsnorm(x, g, out_dtype, tm=512):
    m, d = x.shape
    return pl.pallas_call(
        _rmsnorm_kernel,
        grid=(m // tm,),
        in_specs=[pl.BlockSpec((tm, d), lambda i: (i, 0)), pl.BlockSpec((1, d), lambda i: (0, 0))],
        out_specs=pl.BlockSpec((tm, d), lambda i: (i, 0)),
        out_shape=jax.ShapeDtypeStruct((m, d), out_dtype),
        compiler_params=_params(("parallel",)),
        name="rmsnorm",
    )(x, g.reshape(1, d))


def _mm_kernel(a_ref, wt_ref, b_ref, o_ref, *, act):
    acc = lax.dot_general(a_ref[...], wt_ref[...], _NT, preferred_element_type=F32) + b_ref[...]
    if act == "sigmoid":
        acc = _sigmoid(acc)
    o_ref[...] = acc.astype(o_ref.dtype)


def _mm_scaled_kernel(a_ref, wt_ref, s_ref, b_ref, o_ref):
    acc = lax.dot_general(a_ref[...], wt_ref[...], _NT, preferred_element_type=F32)
    o_ref[...] = (acc * s_ref[...] + b_ref[...]).astype(o_ref.dtype)


def _matmul(a, wt, row0, n, b, out_dtype, act=None, scale=None, tm=1024, tn=1024, name="proj"):
    m, k = a.shape
    tm, tn = min(tm, m), min(tn, n)
    row = lambda: pl.BlockSpec((1, tn), lambda i, j: (0, j))
    scaled = scale is not None
    return pl.pallas_call(
        _mm_scaled_kernel if scaled else functools.partial(_mm_kernel, act=act),
        grid=(m // tm, n // tn),
        in_specs=[pl.BlockSpec((tm, k), lambda i, j: (i, 0)),
                  pl.BlockSpec((pl.Element(tn), pl.Element(k)),
                               lambda i, j: (pl.multiple_of(row0 + j * tn, 16), 0))]
                 + ([row()] if scaled else []) + [row()],
        out_specs=pl.BlockSpec((tm, tn), lambda i, j: (i, j)),
        out_shape=jax.ShapeDtypeStruct((m, n), out_dtype),
        compiler_params=_params(("parallel", "arbitrary"), VMEM_MIB),
        name=name,
    )(a, wt, *([scale.reshape(1, n)] if scaled else []), b.reshape(1, n))


HG_BLOCK = HG_CHUNK
HG_GROUP = 8
HG_ROWS = 512


def _hgrn_tables():
    c = HG_BLOCK
    levels = [c >> i for i in range(int(np.log2(c)))]
    t = np.arange(c)
    u = np.arange(c)
    rall = [(u[None, :] <= t[:, None]), (u[None, :] > t[:, None])]
    low, mask = [], []
    for p in levels:
        half = p // 2
        m = (t // p) * p + half
        lower = (t % p) >= half
        r = np.where(lower[:, None], (u[None, :] > m[:, None]) & (u[None, :] <= t[:, None]),
                     (u[None, :] > t[:, None]) & (u[None, :] <= m[:, None]))
        rall.append(r)
        low.append(np.broadcast_to(lower[:, None], (c, LANES)))
        mask.append(((t[:, None] // p) == (t[None, :] // p)) & lower[:, None] & ~lower[None, :])
    mask.append(np.eye(c, dtype=bool))
    rall = np.concatenate(rall, 0).astype(np.float32)
    return (np.tile(rall, (1, 3)), np.stack(low).astype(np.float32), np.stack(mask).astype(np.float32),
            len(levels))


def _hgrn_kernel(q_ref, f_ref, i_ref, g_ref, lb_ref, gn_ref, rall_ref, low_ref, mask_ref, o_ref, st_ref,
                 *, n_chunks, n_levels, heads):
    c = HG_BLOCK

    @pl.when(pl.program_id(2) == 0)
    def _():
        st_ref[...] = jnp.zeros_like(st_ref)

    lb = lb_ref[...]
    one_m_lb = 1.0 - lb
    gn = gn_ref[...]

    def body(ci, carry):
        rows = pl.ds(pl.multiple_of(ci * c, c), c)
        z = f_ref[rows, :]
        q = q_ref[rows, :]
        e = jnp.exp(-jnp.abs(z))
        r = 1.0 / (1.0 + e)
        pos = z >= 0
        sig = jnp.where(pos, r, e * r)
        nsig = jnp.where(pos, e * r, r)
        logf = jnp.log(lb + one_m_lb * sig)
        k = one_m_lb * nsig
        qs = q * _sigmoid(q)
        l1 = logf.astype(BF16)
        r1 = logf - l1.astype(F32)
        l2 = r1.astype(BF16)
        l3 = (r1 - l2.astype(F32)).astype(BF16)
        ex = jnp.exp(jnp.dot(rall_ref[...], jnp.concatenate([l1, l2, l3], axis=0), preferred_element_type=F32))
        vb = i_ref[rows, :].astype(BF16)
        gate = g_ref[rows, :]
        gate = gate * _sigmoid(gate)
        for g in range(heads):
            sl = slice(g * HG_DK, (g + 1) * HG_DK)
            qg, kg, vg = qs[:, sl], k[:, sl], vb[:, sl]
            eb = ex[0:c, sl]
            a = mask_ref[n_levels] * lax.dot_general(qg.astype(BF16), kg.astype(BF16), _NT,
                                                     preferred_element_type=F32)
            for li in range(n_levels):
                w = (jnp.where(low_ref[li] != 0.0, qg, kg) * ex[(2 + li) * c:(3 + li) * c, sl]).astype(BF16)
                a = a + mask_ref[li] * lax.dot_general(w, w, _NT, preferred_element_type=F32)
            st = st_ref[g]
            o = (jnp.dot(a.astype(BF16), vg, preferred_element_type=F32)
                 + lax.dot_general((qg * eb).astype(BF16), st.astype(BF16), _NT, preferred_element_type=F32))
            st_ref[g] = st * eb[c - 1:c, :] + lax.dot_general(vg, (kg * ex[c:2 * c, sl]).astype(BF16), _TN,
                                                              preferred_element_type=F32)
            o_ref[rows, sl] = (_rms(o, gn) * gate[:, sl]).astype(o_ref.dtype)
        return carry

    lax.fori_loop(0, n_chunks, body, 0, unroll=2)


def _hgrn(zhg, lb, gn, batch, seq):
    rall, low, mask, n_levels = _hgrn_tables()
    hb = HG_HEADS // HG_GROUP
    gw = HG_GROUP * HG_DK
    tc = min(HG_ROWS, seq)
    nt = seq // tc
    col = lambda off: pl.BlockSpec((tc, gw), lambda b, hh, tt: (b * nt + tt, off + hh))
    full = lambda a: pl.BlockSpec(a.shape, lambda b, hh, tt: (0,) * a.ndim)
    return pl.pallas_call(
        functools.partial(_hgrn_kernel, n_chunks=tc // HG_BLOCK, n_levels=n_levels, heads=HG_GROUP),
        grid=(batch, hb, nt),
        in_specs=[col(0), col(hb), col(2 * hb), col(3 * hb),
                  pl.BlockSpec((1, gw), lambda b, hh, tt: (0, hh)),
                  pl.BlockSpec((1, HG_DV), lambda b, hh, tt: (0, 0)),
                  full(rall), full(low), full(mask)],
        out_specs=pl.BlockSpec((tc, gw), lambda b, hh, tt: (b * nt + tt, hh)),
        out_shape=jax.ShapeDtypeStruct((batch * seq, HG_HEADS * HG_DV), BF16),
        scratch_shapes=[pltpu.VMEM((HG_GROUP, HG_DV, HG_DK), F32)],
        compiler_params=_params(("parallel", "parallel", "arbitrary"), VMEM_MIB),
        name="hgrn2",
    )(zhg, zhg, zhg, zhg, lb.reshape(1, HG_WIDTH), gn.reshape(1, HG_DV),
      jnp.asarray(rall, BF16), jnp.asarray(low), jnp.asarray(mask))


DSA_TQ = 256
DSA_ROW_GROUPS = 4
DSA_WIDTH_VARIANTS = 4


def _alibi_tables(seq):
    assert 8 % AT_HEADS == 0 and seq <= 64 * 256
    pos = jnp.arange(seq, dtype=jnp.int32)
    hi, lo = (pos // 64).astype(F32), (pos % 64).astype(F32)
    one = jnp.ones((seq,), F32)
    kext = jnp.stack([one, one, hi, lo], axis=1)
    kext = jnp.pad(kext, ((0, 0), (0, AT_HEAD_DIM - 4))).astype(BF16)
    qext = []
    for h in range(AT_HEADS):
        slope = 2.0 ** (-8.0 * (h + 1) / AT_HEADS)
        cols = jnp.stack([-slope * 64.0 * hi, -slope * lo, slope * 64.0 * one, slope * one], axis=1)
        qext.append(jnp.pad(cols, ((0, 0), (0, AT_HEAD_DIM - 4))))
    return jnp.concatenate(qext, axis=1).astype(BF16), kext


def _dsa_kernel(iq_ref, ikt_ref, iw_ref, q_ref, qext_ref, k_ref, kext_ref, v_ref, o_ref, acc_ref, key_ref, bias_ref,
                *, tq, seq, ksel, col_step):
    first_row = pl.program_id(1) * tq
    for g in range(seq // col_step):
        @pl.when(first_row // col_step == g)
        def _(width=(g + 1) * col_step):
            _dsa_body(iq_ref, ikt_ref, iw_ref, q_ref, qext_ref, k_ref, kext_ref, v_ref, o_ref, acc_ref, key_ref,
                      bias_ref, tq=tq, ksel=ksel, width=width, first_row=first_row)


def _dsa_body(iq_ref, ikt_ref, iw_ref, q_ref, qext_ref, k_ref, kext_ref, v_ref, o_ref, acc_ref, key_ref, bias_ref,
              *, tq, ksel, width, first_row):
    seq = width
    t0 = first_row
    acc_ref, key_ref, bias_ref = acc_ref.at[:, :width], key_ref.at[:, :width], bias_ref.at[:, :width]
    ikt = ikt_ref[0, :, :width]
    ik_hi = ikt.astype(BF16)
    ik_lo = (ikt - ik_hi.astype(F32)).astype(BF16)
    rhs = jnp.concatenate([ik_hi, ik_hi, ik_lo, ik_lo], axis=0)
    lane = lax.broadcasted_iota(jnp.int32, (tq, LANES), 1)
    w_all = iw_ref[...] * (IDX_HEADS ** -0.5 * IDX_DIM ** -0.5)
    for h in range(IDX_HEADS):
        if h % 2 == 0:
            x = iq_ref[:, (h // 2) * LANES:(h // 2 + 1) * LANES]
            x_hi = x.astype(BF16).astype(F32)
            x_lo = x - x_hi
            u = jnp.where(lane < IDX_DIM, x_hi, pltpu.roll(x_lo, IDX_DIM, 1)).astype(BF16)
        else:
            u = jnp.where(lane < IDX_DIM, pltpu.roll(x_hi, IDX_DIM, 1), x_lo).astype(BF16)
        s = jnp.dot(jnp.concatenate([u, u], axis=1), rhs, preferred_element_type=F32)
        term = jnp.maximum(s, 0.0) * w_all[:, IDX_DIM + h:IDX_DIM + h + 1]
        if h == 0:
            acc_ref[...] = term
        else:
            acc_ref[...] += term

    col = lax.broadcasted_iota(jnp.int32, (tq, seq), 1)
    row = t0 + lax.broadcasted_iota(jnp.int32, (tq, seq), 0)
    vis = col <= row
    bits = pltpu.bitcast(acc_ref[...], jnp.int32)
    key = bits ^ ((bits >> 31) & 0x7FFFFFFF)
    key_ref[...] = jnp.where(vis, key, INT_MIN)

    rg = tq // DSA_ROW_GROUPS

    def bisect(it, cs):
        bit = jnp.left_shift(jnp.int32(1), 31 - it)
        out = []
        for gi, c in enumerate(cs):
            cand = c | bit
            keys = key_ref[gi * rg:(gi + 1) * rg, :]
            cnt = jnp.sum((keys >= (cand ^ INT_MIN)).astype(jnp.int32), axis=1, keepdims=True)
            out.append(jnp.where(cnt >= ksel, cand, c))
        return tuple(out)

    cs = lax.fori_loop(0, 32, bisect, tuple(jnp.zeros((rg, 1), jnp.int32) for _ in range(DSA_ROW_GROUPS)))
    thr = jnp.concatenate(cs, axis=0) ^ INT_MIN
    bias_ref[...] = jnp.where(vis, jnp.where(key_ref[...] >= thr, 0.0, NEG_BIG), NEG_BIG)

    kext = kext_ref[:width, :]
    ones_v = jnp.ones((width, AT_HEAD_DIM), BF16)
    for h in range(AT_HEADS):
        hs = slice(h * AT_HEAD_DIM, (h + 1) * AT_HEAD_DIM)
        qa = jnp.concatenate([q_ref[:, hs], qext_ref[:, hs]], axis=1)
        ka = jnp.concatenate([k_ref[:width, hs], kext], axis=1)
        logit = lax.dot_general(qa, ka, _NT, preferred_element_type=F32) + bias_ref[...]
        m = jnp.max(logit, axis=1, keepdims=True)
        p = jnp.exp(logit - m).astype(BF16)
        pv = jnp.dot(p, jnp.concatenate([v_ref[:width, hs], ones_v], axis=1), preferred_element_type=F32)
        o_ref[:, hs] = (pv[:, :AT_HEAD_DIM] / pv[:, AT_HEAD_DIM:AT_HEAD_DIM + 1]).astype(o_ref.dtype)


def _dsa(ziq, ikt, ikw, zat, batch, seq, tq=DSA_TQ):
    nq = seq // tq
    ksel = min(TOPK_MAX, seq // 4)
    col_step = max(seq // DSA_WIDTH_VARIANTS, tq)
    qext, kext = _alibi_tables(seq)
    return pl.pallas_call(
        functools.partial(_dsa_kernel, tq=tq, seq=seq, ksel=ksel, col_step=col_step),
        grid=(batch, nq),
        in_specs=[pl.BlockSpec((tq, IDX_HEADS * IDX_DIM), lambda b, i: (b * nq + i, 0)),
                  pl.BlockSpec((1, IDX_DIM, seq), lambda b, i: (b, 0, 0)),
                  pl.BlockSpec((tq, LANES), lambda b, i: (b * nq + i, 0)),
                  pl.BlockSpec((tq, AT_WIDTH), lambda b, i: (b * nq + i, 0)),
                  pl.BlockSpec((tq, AT_WIDTH), lambda b, i: (i, 0)),
                  pl.BlockSpec((seq, AT_WIDTH), lambda b, i: (b, 1)),
                  pl.BlockSpec((seq, AT_HEAD_DIM), lambda b, i: (0, 0)),
                  pl.BlockSpec((seq, AT_WIDTH), lambda b, i: (b, 2))],
        out_specs=pl.BlockSpec((tq, AT_WIDTH), lambda b, i: (b * nq + i, 0)),
        out_shape=jax.ShapeDtypeStruct((batch * seq, AT_WIDTH), BF16),
        scratch_shapes=[pltpu.VMEM((tq, seq), F32), pltpu.VMEM((tq, seq), jnp.int32),
                        pltpu.VMEM((tq, seq), F32)],
        compiler_params=_params(("parallel", "arbitrary"), VMEM_MIB),
        name="dsa",
    )(ziq, ikt, ikw, zat, qext, zat, kext, zat)


def _merge_kernel(a1_ref, w1_ref, a2_ref, w2_ref, g1_ref, g2_ref, o_ref):
    y1 = jnp.dot(a1_ref[...], w1_ref[...], preferred_element_type=F32)
    y2 = jnp.dot(a2_ref[...], w2_ref[...], preferred_element_type=F32)
    o_ref[...] = (g1_ref[...].astype(F32) * y1 + g2_ref[...].astype(F32) * y2).astype(o_ref.dtype)


def _merge(o_hg, w_hg, o_at, w_at, gates, tm=1024, tn=1024):
    m, k1 = o_hg.shape
    k2 = o_at.shape[1]
    n = w_hg.shape[1]
    tm = min(tm, m)
    nb = n // tn
    return pl.pallas_call(
        _merge_kernel,
        grid=(m // tm, nb),
        in_specs=[pl.BlockSpec((tm, k1), lambda i, j: (i, 0)), pl.BlockSpec((k1, tn), lambda i, j: (0, j)),
                  pl.BlockSpec((tm, k2), lambda i, j: (i, 0)), pl.BlockSpec((k2, tn), lambda i, j: (0, j)),
                  pl.BlockSpec((tm, tn), lambda i, j: (i, j)), pl.BlockSpec((tm, tn), lambda i, j: (i, nb + j))],
        out_specs=pl.BlockSpec((tm, tn), lambda i, j: (i, j)),
        out_shape=jax.ShapeDtypeStruct((m, n), BF16),
        compiler_params=_params(("parallel", "arbitrary"), VMEM_MIB),
        name="merge",
    )(o_hg, w_hg, o_at, w_at, gates, gates)


def _outproj_kernel(m_ref, w_ref, x_ref, gn_ref, wr_ref, br_ref, x1_ref, hp_ref, ri_ref, rw_ref):
    x1 = x_ref[...] + jnp.dot(m_ref[...], w_ref[...], preferred_element_type=F32)
    x1_ref[...] = x1
    hn = _rms(x1, gn_ref[...])
    hp_ref[...] = _pack_bf16_halves(hn)
    h_hi = hn.astype(BF16)
    h_lo = (hn - h_hi.astype(F32)).astype(BF16)
    logit = jnp.dot(jnp.concatenate([h_hi, h_hi, h_lo], axis=1), wr_ref[...],
                    preferred_element_type=F32) + br_ref[...]
    lane = lax.broadcasted_iota(jnp.int32, logit.shape, 1)
    lane_f = lane.astype(F32)
    is_g = lane < N_GROUPS
    lg = jnp.where(is_g, logit, -jnp.inf)
    mg = jnp.max(lg, axis=1, keepdims=True)
    pg_top = 1.0 / jnp.sum(jnp.exp(lg - mg), axis=1, keepdims=True)
    gid = jnp.min(jnp.where(lg == mg, lane_f, float(LANES)), axis=1, keepdims=True).astype(jnp.int32)
    in_grp = ((lane - N_GROUPS) >> 3) == gid
    le = jnp.where(in_grp, logit, -jnp.inf)
    m1 = jnp.max(le, axis=1, keepdims=True)
    i1 = jnp.min(jnp.where(le == m1, lane_f, float(LANES)), axis=1, keepdims=True)
    le2 = jnp.where(lane_f == i1, -jnp.inf, le)
    m2 = jnp.max(le2, axis=1, keepdims=True)
    i2 = jnp.min(jnp.where(le2 == m2, lane_f, float(LANES)), axis=1, keepdims=True)
    r = jnp.exp(m2 - m1)
    w1 = pg_top / (1.0 + r)
    w2 = pg_top * r / (1.0 + r)
    e1 = i1.astype(jnp.int32) - N_GROUPS
    e2 = i2.astype(jnp.int32) - N_GROUPS
    ri_ref[...] = jnp.where(lane == 0, e1, jnp.where(lane == 1, e2, 0))
    rw_ref[...] = jnp.where(lane == 0, w1, jnp.where(lane == 1, w2, 0.0))


def _outproj(mixed, w_out, x, gn, w_r, b_r, tm=256):
    m, d = x.shape
    row = lambda w: pl.BlockSpec((tm, w), lambda i: (i, 0))
    const = lambda a: pl.BlockSpec(a.shape, lambda i: (0, 0), pipeline_mode=pl.Buffered(1))
    gn = gn.reshape(1, d)
    b_r = b_r.reshape(1, LANES)
    return pl.pallas_call(
        _outproj_kernel,
        grid=(m // tm,),
        in_specs=[row(mixed.shape[1]), const(w_out), row(d), const(gn), const(w_r), const(b_r)],
        out_specs=[row(d), row(d // 2), row(LANES), row(LANES)],
        out_shape=[jax.ShapeDtypeStruct((m, d), F32), jax.ShapeDtypeStruct((m, d // 2), jnp.int32),
                   jax.ShapeDtypeStruct((m, LANES), jnp.int32), jax.ShapeDtypeStruct((m, LANES), F32)],
        compiler_params=_params(("parallel",), VMEM_MIB_WHOLE_WEIGHTS),
        name="outproj_router",
    )(mixed, w_out, x, gn, w_r, b_r)


def _expert_kernel(te_ref, tv_ref, src_ref, dst_ref, hn_hbm, wg_ref, wu_ref, wd_ref, y_hbm,
                   xbuf, ybuf, xb_ref, wgb, wub, wdb, gsem, ssem, *, tm):
    t = pl.program_id(0)
    slot = lax.rem(t, 2)
    other = 1 - slot
    valid = tv_ref[t] > 0
    prev_valid = jnp.logical_and(t >= 1, tv_ref[jnp.maximum(t - 1, 0)] > 0)

    def gather_copy(tile, r, s):
        tok = src_ref[tile * tm + r]
        return pltpu.make_async_copy(hn_hbm.at[pl.ds(tok, 1), :], xbuf.at[s, pl.ds(r, 1), :], gsem.at[s])

    def scatter_copy(tile, r, s):
        dst = dst_ref[(tile + 1) * tm + r]
        return pltpu.make_async_copy(ybuf.at[s, pl.ds(r, 1), :], y_hbm.at[pl.ds(dst, 1), :], ssem.at[s])

    def start_rows(copy_fn, tile, s):
        def body(r, carry):
            copy_fn(tile, r, s).start()
            return carry

        lax.fori_loop(0, tm, body, 0, unroll=8)

    def gather_wait(s):
        pltpu.make_async_copy(hn_hbm.at[pl.ds(0, tm), :], xbuf.at[s], gsem.at[s]).wait()

    def scatter_wait(s):
        pltpu.make_async_copy(ybuf.at[s], y_hbm.at[pl.ds(0, tm), :], ssem.at[s]).wait()

    @pl.when(t == 0)
    def _():
        start_rows(gather_copy, 0, 0)
        ybuf[1] = jnp.zeros(ybuf.shape[1:], jnp.int32)

    @pl.when(prev_valid)
    def _():
        scatter_wait(slot)

    @pl.when(valid)
    def _():
        gather_wait(slot)

        @pl.when(jnp.logical_or(t == 0, te_ref[t] != te_ref[jnp.maximum(t - 1, 0)]))
        def _():
            wgb[...] = wg_ref[0].astype(BF16)
            wub[...] = wu_ref[0].astype(BF16)
            wdb[...] = wd_ref[0].astype(BF16)

        xb_ref[...] = _unpack_bf16_halves(xbuf[slot]).astype(BF16)
        for r in range(tm):
            gather_copy(t + 1, r, other).start()
        for r in range(tm):
            scatter_copy(t - 1, r, other).start()
        xb = xb_ref[...]
        g = jnp.dot(xb, wgb[...], preferred_element_type=F32)
        u = jnp.dot(xb, wub[...], preferred_element_type=F32)
        hid = (g * _sigmoid(g) * u).astype(BF16)
        ybuf[slot] = _pack_bf16_halves(jnp.dot(hid, wdb[...], preferred_element_type=F32))

    @pl.when(jnp.logical_and(jnp.logical_not(valid), prev_valid))
    def _():
        gather_wait(slot)
        start_rows(scatter_copy, t - 1, other)
        scatter_wait(other)


def _experts(hn, tile_expert, tile_valid, src, dst, w_g, w_u, w_d, tm):
    n_tok, dp = hn.shape
    d = 2 * dp
    n_steps = tile_expert.shape[0]
    wspec = lambda a: pl.BlockSpec((1,) + a.shape[1:], lambda t, te, tv, sr, ds: (te[t], 0, 0))
    return pl.pallas_call(
        functools.partial(_expert_kernel, tm=tm),
        grid_spec=pltpu.PrefetchScalarGridSpec(
            num_scalar_prefetch=4,
            grid=(n_steps,),
            in_specs=[pl.BlockSpec(memory_space=pl.ANY), wspec(w_g), wspec(w_u), wspec(w_d)],
            out_specs=pl.BlockSpec(memory_space=pl.ANY),
            scratch_shapes=[pltpu.VMEM((2, tm, dp), jnp.int32), pltpu.VMEM((2, tm, dp), jnp.int32),
                            pltpu.VMEM((tm, d), BF16),
                            pltpu.VMEM(w_g.shape[1:], BF16), pltpu.VMEM(w_u.shape[1:], BF16),
                            pltpu.VMEM(w_d.shape[1:], BF16),
                            pltpu.SemaphoreType.DMA((2,)), pltpu.SemaphoreType.DMA((2,))]),
        out_shape=jax.ShapeDtypeStruct((2 * n_tok + tm, dp), jnp.int32),
        compiler_params=_params(("arbitrary",), VMEM_MIB_WHOLE_WEIGHTS),
        name="experts",
    )(tile_expert, tile_valid, src, dst, hn, w_g, w_u, w_d)


def _dispatch_plan(route_e, tm):
    n_tok = route_e.shape[0]
    n_pairs = 2 * n_tok
    n_tiles = n_pairs // tm + N_EXPERTS + 1
    eflat = route_e.T.reshape(-1)
    order = jnp.argsort(eflat, stable=True).astype(jnp.int32)
    counts = jnp.sum(eflat[:, None] == jnp.arange(N_EXPERTS, dtype=jnp.int32)[None, :], axis=0, dtype=jnp.int32)
    tiles_e = (counts + tm - 1) // tm
    tile_end = jnp.cumsum(tiles_e)
    tile_off = tile_end - tiles_e
    start = jnp.cumsum(counts) - counts
    tid = jnp.arange(n_tiles, dtype=jnp.int32)
    te = jnp.minimum(jnp.sum(tile_end[None, :] <= tid[:, None], axis=1), N_EXPERTS - 1).astype(jnp.int32)
    tv = jnp.clip(counts[te] - (tid - tile_off[te]) * tm, 0, tm)
    used = tid < tile_end[-1]
    last_e = te[jnp.maximum(tile_end[-1] - 1, 0)]
    te = jnp.where(used, te, last_e).astype(jnp.int32)
    tv = jnp.where(used, tv, 0).astype(jnp.int32)
    r = jnp.arange(tm, dtype=jnp.int32)[None, :]
    src = (start[te] + (tid - tile_off[te]) * tm)[:, None] + r
    dst = jnp.where(r < tv[:, None], order[jnp.clip(src, 0, n_pairs - 1)], n_pairs + r).reshape(-1).astype(jnp.int32)
    spare = n_pairs + jnp.arange(tm, dtype=jnp.int32)
    return te, tv, dst % n_tok, jnp.concatenate([spare, dst])


def _final_kernel(x1_ref, y0_ref, y1_ref, rw_ref, p_ref, wgate_ref, wproj_ref, gple_ref, gfin_ref, o_ref,
                  *, last_layer):
    rw = rw_ref[...]
    x2 = (x1_ref[...] + rw[:, 0:1] * _unpack_bf16_halves(y0_ref[...])
          + rw[:, 1:2] * _unpack_bf16_halves(y1_ref[...]))
    hn = _rms(x2, gple_ref[...]).astype(BF16)
    gate = _sigmoid(jnp.dot(hn, wgate_ref[...], preferred_element_type=F32))
    emb = jnp.dot(p_ref[...].astype(BF16), wproj_ref[...], preferred_element_type=F32)
    x3 = x2 + gate * emb
    o_ref[...] = _rms(x3, gfin_ref[...]) if last_layer else x3


def _final(x1, y, rw, p, w_gate, w_proj, g_ple, g_fin, last_layer, tm=512):
    m, d = x1.shape
    nb = m // tm
    row = lambda w: pl.BlockSpec((tm, w), lambda i: (i, 0))
    const = lambda a: pl.BlockSpec(a.shape, lambda i: (0, 0), pipeline_mode=pl.Buffered(1))
    g_ple, g_fin = g_ple.reshape(1, d), g_fin.reshape(1, d)
    return pl.pallas_call(
        functools.partial(_final_kernel, last_layer=last_layer),
        grid=(nb,),
        in_specs=[row(d), row(d // 2), pl.BlockSpec((tm, d // 2), lambda i: (nb + i, 0)), row(LANES), row(p.shape[1]),
                  const(w_gate), const(w_proj), const(g_ple), const(g_fin)],
        out_specs=row(d),
        out_shape=jax.ShapeDtypeStruct((m, d), F32),
        compiler_params=_params(("parallel",), VMEM_MIB_WHOLE_WEIGHTS),
        name="final",
    )(x1, y, y, rw, p, w_gate, w_proj, g_ple, g_fin)


def kernel(x, p, attn_norm, w_in, b_in, hg_lb_logits, hg_out_norm, w_hg_up, w_attn_up, w_out, ffn_norm,
           w_router_group, b_router_group, w_router_expert, b_router_expert, w_exp_gate, w_exp_up,
           w_exp_down, ple_norm, w_ple_gate, w_ple_proj, final_norm):
    batch, seq, d = x.shape
    n = batch * seq
    depth = w_in.shape[0]
    o_hf, o_at = 0, 4 * HG_WIDTH
    o_iq = o_at + 3 * AT_WIDTH
    o_ik = o_iq + IDX_HEADS * IDX_DIM
    o_iw = o_ik + IDX_DIM
    o_g = o_iw + IDX_HEADS
    lbs = jnp.cumsum(jax.nn.softmax(hg_lb_logits.astype(F32), axis=0), axis=0)
    xf = x.reshape(n, d)
    moe_tm = min(256, n // 2)
    for li in range(depth):
        wt, b = w_in[li].T.astype(BF16), b_in[li]
        h = _rmsnorm(xf, attn_norm[li], BF16)
        zhg = _matmul(h, wt, o_hf, o_at - o_hf, b[o_hf:o_at], F32, name="proj_hgrn")
        qkv_scale = jnp.concatenate([jnp.full((AT_WIDTH,), AT_HEAD_DIM ** -0.5, F32), jnp.ones((2 * AT_WIDTH,), F32)])
        zat = _matmul(h, wt, o_at, o_iq - o_at, b[o_at:o_iq] * qkv_scale, BF16, scale=qkv_scale, name="proj_attn")
        ziq = _matmul(h, wt, o_iq, o_ik - o_iq, b[o_iq:o_ik], F32, name="proj_iq")
        zikw = _matmul(h, wt, o_ik, LANES, b[o_ik:o_ik + LANES], F32, name="proj_ikw")
        gates = _matmul(h, wt, o_g, 2 * d, b[o_g:], BF16, act="sigmoid", name="proj_gates")

        o_hg = _hgrn(zhg, lbs[li], hg_out_norm[li], batch, seq)
        ikt = zikw[:, :IDX_DIM].reshape(batch, seq, IDX_DIM).transpose(0, 2, 1)
        o_att = _dsa(ziq, ikt, zikw, zat, batch, seq)
        mixed = _merge(o_hg, w_hg_up[li].astype(BF16), o_att, w_attn_up[li].astype(BF16), gates)

        w_r = jnp.pad(jnp.concatenate([w_router_group[li], w_router_expert[li]], axis=1),
                      ((0, 0), (0, LANES - N_GROUPS - N_EXPERTS)))
        b_r = jnp.pad(jnp.concatenate([b_router_group[li], b_router_expert[li]]), (0, LANES - N_GROUPS - N_EXPERTS))
        w_r_hi = w_r.astype(BF16)
        w_r_lo = (w_r - w_r_hi.astype(F32)).astype(BF16)
        w_r = jnp.concatenate([w_r_hi, w_r_lo, w_r_hi], axis=0)
        x1, hn, route_e, route_w = _outproj(mixed, w_out[li].astype(BF16), xf, ffn_norm[li], w_r, b_r)
        tile_e, tile_v, src, dst = _dispatch_plan(route_e[:, :2], moe_tm)
        y = _experts(hn, tile_e, tile_v, src, dst, w_exp_gate[li], w_exp_up[li], w_exp_down[li], moe_tm)
        xf = _final(x1, y, route_w, p[li].reshape(n, -1), w_ple_gate[li].astype(BF16),
                    w_ple_proj[li].astype(BF16), ple_norm[li], final_norm, li == depth - 1)
    return xf.reshape(batch, seq, d)
```

```python
import functools

import numpy as np
import jax
import jax.numpy as jnp
from jax import lax
from jax.experimental import pallas as pl
from jax.experimental.pallas import tpu as pltpu

F32 = jnp.float32
BF16 = jnp.bfloat16

HG_HEADS = 8
HG_DK = 128
HG_DV = 128
HG_WIDTH = HG_HEADS * HG_DK
HG_CHUNK = 64
AT_HEADS = 8
AT_HEAD_DIM = 128
AT_WIDTH = AT_HEADS * AT_HEAD_DIM
IDX_HEADS = 16
IDX_DIM = 64
TOPK_MAX = 256
N_GROUPS = 4
EXPERTS_PER_GROUP = 8
N_EXPERTS = N_GROUPS * EXPERTS_PER_GROUP
D_EXPERT = 512
RMS_EPS = 1e-6
LANES = 128
VMEM_MIB = 48
VMEM_MIB_WHOLE_WEIGHTS = 56
INT_MIN = -(2 ** 31)
NEG_BIG = -1e30

_NT = (((1,), (1,)), ((), ()))
_TN = (((0,), (0,)), ((), ()))


def _params(sem, vmem_mb=None):
    kw = dict(dimension_semantics=sem)
    if vmem_mb is not None:
        kw["vmem_limit_bytes"] = vmem_mb * 1024 * 1024
    return pltpu.CompilerParams(**kw)


def _sigmoid(x):
    return 1.0 / (1.0 + jnp.exp(-x))


def _pack_bf16_halves(x):
    c = x.shape[1] // 2
    xr = x.astype(BF16).astype(F32)
    hi = pltpu.bitcast(xr[:, :c], jnp.int32)
    lo = lax.shift_right_logical(pltpu.bitcast(xr[:, c:], jnp.int32), 16)
    return hi | lo


def _unpack_bf16_halves(w):
    hi = pltpu.bitcast(w & -65536, F32)
    lo = pltpu.bitcast(lax.shift_left(w, 16), F32)
    return jnp.concatenate([hi, lo], axis=1)


def _rms(x, g):
    ms = jnp.mean(x * x, axis=-1, keepdims=True)
    return x * lax.rsqrt(ms + RMS_EPS) * g


def _rmsnorm_kernel(x_ref, g_ref, o_ref):
    o_ref[...] = _rms(x_ref[...], g_ref[...]).astype(o_ref.dtype)


def _rmsnorm(x, g, out_dtype, tm=512):
    m, d = x.shape
    return pl.pallas_call(
        _rmsnorm_kernel,
        grid=(m // tm,),
        in_specs=[pl.BlockSpec((tm, d), lambda i: (i, 0)), pl.BlockSpec((1, d), lambda i: (0, 0))],
        out_specs=pl.BlockSpec((tm, d), lambda i: (i, 0)),
        out_shape=jax.ShapeDtypeStruct((m, d), out_dtype),
        compiler_params=_params(("parallel",)),
        name="rmsnorm",
    )(x, g.reshape(1, d))


def _mm_kernel(a_ref, wt_ref, b_ref, o_ref, *, act):
    acc = lax.dot_general(a_ref[...], wt_ref[...], _NT, preferred_element_type=F32) + b_ref[...]
    if act == "sigmoid":
        acc = _sigmoid(acc)
    o_ref[...] = acc.astype(o_ref.dtype)


def _mm_scaled_kernel(a_ref, wt_ref, s_ref, b_ref, o_ref):
    acc = lax.dot_general(a_ref[...], wt_ref[...], _NT, preferred_element_type=F32)
    o_ref[...] = (acc * s_ref[...] + b_ref[...]).astype(o_ref.dtype)


def _matmul(a, wt, row0, n, b, out_dtype, act=None, scale=None, tm=1024, tn=1024, name="proj"):
    m, k = a.shape
    tm, tn = min(tm, m), min(tn, n)
    row = lambda: pl.BlockSpec((1, tn), lambda i, j: (0, j))
    scaled = scale is not None
    return pl.pallas_call(
        _mm_scaled_kernel if scaled else functools.partial(_mm_kernel, act=act),
        grid=(m // tm, n // tn),
        in_specs=[pl.BlockSpec((tm, k), lambda i, j: (i, 0)),
                  pl.BlockSpec((pl.Element(tn), pl.Element(k)),
                               lambda i, j: (pl.multiple_of(row0 + j * tn, 16), 0))]
                 + ([row()] if scaled else []) + [row()],
        out_specs=pl.BlockSpec((tm, tn), lambda i, j: (i, j)),
        out_shape=jax.ShapeDtypeStruct((m, n), out_dtype),
        compiler_params=_params(("parallel", "arbitrary"), VMEM_MIB),
        name=name,
    )(a, wt, *([scale.reshape(1, n)] if scaled else []), b.reshape(1, n))


HG_BLOCK = HG_CHUNK
HG_GROUP = 8
HG_ROWS = 512


def _hgrn_tables():
    c = HG_BLOCK
    levels = [c >> i for i in range(int(np.log2(c)))]
    t = np.arange(c)
    u = np.arange(c)
    rall = [(u[None, :] <= t[:, None]), (u[None, :] > t[:, None])]
    low, mask = [], []
    for p in levels:
        half = p // 2
        m = (t // p) * p + half
        lower = (t % p) >= half
        r = np.where(lower[:, None], (u[None, :] > m[:, None]) & (u[None, :] <= t[:, None]),
                     (u[None, :] > t[:, None]) & (u[None, :] <= m[:, None]))
        rall.append(r)
        low.append(np.broadcast_to(lower[:, None], (c, LANES)))
        mask.append(((t[:, None] // p) == (t[None, :] // p)) & lower[:, None] & ~lower[None, :])
    mask.append(np.eye(c, dtype=bool))
    rall = np.concatenate(rall, 0).astype(np.float32)
    return (np.tile(rall, (1, 3)), np.stack(low).astype(np.float32), np.stack(mask).astype(np.float32),
            len(levels))


def _hgrn_kernel(q_ref, f_ref, i_ref, g_ref, lb_ref, gn_ref, rall_ref, low_ref, mask_ref, o_ref, st_ref,
                 *, n_chunks, n_levels, heads):
    c = HG_BLOCK

    @pl.when(pl.program_id(2) == 0)
    def _():
        st_ref[...] = jnp.zeros_like(st_ref)

    lb = lb_ref[...]
    one_m_lb = 1.0 - lb
    gn = gn_ref[...]

    def body(ci, carry):
        rows = pl.ds(pl.multiple_of(ci * c, c), c)
        z = f_ref[rows, :]
        q = q_ref[rows, :]
        e = jnp.exp(-jnp.abs(z))
        r = 1.0 / (1.0 + e)
        pos = z >= 0
        sig = jnp.where(pos, r, e * r)
        nsig = jnp.where(pos, e * r, r)
        logf = jnp.log(lb + one_m_lb * sig)
        k = one_m_lb * nsig
        qs = q * _sigmoid(q)
        l1 = logf.astype(BF16)
        r1 = logf - l1.astype(F32)
        l2 = r1.astype(BF16)
        l3 = (r1 - l2.astype(F32)).astype(BF16)
        ex = jnp.exp(jnp.dot(rall_ref[...], jnp.concatenate([l1, l2, l3], axis=0), preferred_element_type=F32))
        vb = i_ref[rows, :].astype(BF16)
        gate = g_ref[rows, :]
        gate = gate * _sigmoid(gate)
        for g in range(heads):
            sl = slice(g * HG_DK, (g + 1) * HG_DK)
            qg, kg, vg = qs[:, sl], k[:, sl], vb[:, sl]
            eb = ex[0:c, sl]
            a = mask_ref[n_levels] * lax.dot_general(qg.astype(BF16), kg.astype(BF16), _NT,
                                                     preferred_element_type=F32)
            for li in range(n_levels):
                w = (jnp.where(low_ref[li] != 0.0, qg, kg) * ex[(2 + li) * c:(3 + li) * c, sl]).astype(BF16)
                a = a + mask_ref[li] * lax.dot_general(w, w, _NT, preferred_element_type=F32)
            st = st_ref[g]
            o = (jnp.dot(a.astype(BF16), vg, preferred_element_type=F32)
                 + lax.dot_general((qg * eb).astype(BF16), st.astype(BF16), _NT, preferred_element_type=F32))
            st_ref[g] = st * eb[c - 1:c, :] + lax.dot_general(vg, (kg * ex[c:2 * c, sl]).astype(BF16), _TN,
                                                              preferred_element_type=F32)
            o_ref[rows, sl] = (_rms(o, gn) * gate[:, sl]).astype(o_ref.dtype)
        return carry

    lax.fori_loop(0, n_chunks, body, 0, unroll=2)


def _hgrn(zhg, lb, gn, batch, seq):
    rall, low, mask, n_levels = _hgrn_tables()
    hb = HG_HEADS // HG_GROUP
    gw = HG_GROUP * HG_DK
    tc = min(HG_ROWS, seq)
    nt = seq // tc
    col = lambda off: pl.BlockSpec((tc, gw), lambda b, hh, tt: (b * nt + tt, off + hh))
    full = lambda a: pl.BlockSpec(a.shape, lambda b, hh, tt: (0,) * a.ndim)
    return pl.pallas_call(
        functools.partial(_hgrn_kernel, n_chunks=tc // HG_BLOCK, n_levels=n_levels, heads=HG_GROUP),
        grid=(batch, hb, nt),
        in_specs=[col(0), col(hb), col(2 * hb), col(3 * hb),
                  pl.BlockSpec((1, gw), lambda b, hh, tt: (0, hh)),
                  pl.BlockSpec((1, HG_DV), lambda b, hh, tt: (0, 0)),
                  full(rall), full(low), full(mask)],
        out_specs=pl.BlockSpec((tc, gw), lambda b, hh, tt: (b * nt + tt, hh)),
        out_shape=jax.ShapeDtypeStruct((batch * seq, HG_HEADS * HG_DV), BF16),
        scratch_shapes=[pltpu.VMEM((HG_GROUP, HG_DV, HG_DK), F32)],
        compiler_params=_params(("parallel", "parallel", "arbitrary"), VMEM_MIB),
        name="hgrn2",
    )(zhg, zhg, zhg, zhg, lb.reshape(1, HG_WIDTH), gn.reshape(1, HG_DV),
      jnp.asarray(rall, BF16), jnp.asarray(low), jnp.asarray(mask))


DSA_TQ = 256
DSA_ROW_GROUPS = 4
DSA_WIDTH_VARIANTS = 4


def _alibi_tables(seq):
    assert 8 % AT_HEADS == 0 and seq <= 64 * 256
    pos = jnp.arange(seq, dtype=jnp.int32)
    hi, lo = (pos // 64).astype(F32), (pos % 64).astype(F32)
    one = jnp.ones((seq,), F32)
    kext = jnp.stack([one, one, hi, lo], axis=1)
    kext = jnp.pad(kext, ((0, 0), (0, AT_HEAD_DIM - 4))).astype(BF16)
    qext = []
    for h in range(AT_HEADS):
        slope = 2.0 ** (-8.0 * (h + 1) / AT_HEADS)
        cols = jnp.stack([-slope * 64.0 * hi, -slope * lo, slope * 64.0 * one, slope * one], axis=1)
        qext.append(jnp.pad(cols, ((0, 0), (0, AT_HEAD_DIM - 4))))
    return jnp.concatenate(qext, axis=1).astype(BF16), kext


def _dsa_kernel(iq_ref, ikt_ref, iw_ref, q_ref, qext_ref, k_ref, kext_ref, v_ref, o_ref, acc_ref, key_ref, bias_ref,
                *, tq, seq, ksel, col_step):
    first_row = pl.program_id(1) * tq
    for g in range(seq // col_step):
        @pl.when(first_row // col_step == g)
        def _(width=(g + 1) * col_step):
            _dsa_body(iq_ref, ikt_ref, iw_ref, q_ref, qext_ref, k_ref, kext_ref, v_ref, o_ref, acc_ref, key_ref,
                      bias_ref, tq=tq, ksel=ksel, width=width, first_row=first_row)


def _dsa_body(iq_ref, ikt_ref, iw_ref, q_ref, qext_ref, k_ref, kext_ref, v_ref, o_ref, acc_ref, key_ref, bias_ref,
              *, tq, ksel, width, first_row):
    seq = width
    t0 = first_row
    acc_ref, key_ref, bias_ref = acc_ref.at[:, :width], key_ref.at[:, :width], bias_ref.at[:, :width]
    ikt = ikt_ref[0, :, :width]
    ik_hi = ikt.astype(BF16)
    ik_lo = (ikt - ik_hi.astype(F32)).astype(BF16)
    rhs = jnp.concatenate([ik_hi, ik_hi, ik_lo, ik_lo], axis=0)
    lane = lax.broadcasted_iota(jnp.int32, (tq, LANES), 1)
    w_all = iw_ref[...] * (IDX_HEADS ** -0.5 * IDX_DIM ** -0.5)
    for h in range(IDX_HEADS):
        if h % 2 == 0:
            x = iq_ref[:, (h // 2) * LANES:(h // 2 + 1) * LANES]
            x_hi = x.astype(BF16).astype(F32)
            x_lo = x - x_hi
            u = jnp.where(lane < IDX_DIM, x_hi, pltpu.roll(x_lo, IDX_DIM, 1)).astype(BF16)
        else:
            u = jnp.where(lane < IDX_DIM, pltpu.roll(x_hi, IDX_DIM, 1), x_lo).astype(BF16)
        s = jnp.dot(jnp.concatenate([u, u], axis=1), rhs, preferred_element_type=F32)
        term = jnp.maximum(s, 0.0) * w_all[:, IDX_DIM + h:IDX_DIM + h + 1]
        if h == 0:
            acc_ref[...] = term
        else:
            acc_ref[...] += term

    col = lax.broadcasted_iota(jnp.int32, (tq, seq), 1)
    row = t0 + lax.broadcasted_iota(jnp.int32, (tq, seq), 0)
    vis = col <= row
    bits = pltpu.bitcast(acc_ref[...], jnp.int32)
    key = bits ^ ((bits >> 31) & 0x7FFFFFFF)
    key_ref[...] = jnp.where(vis, key, INT_MIN)

    rg = tq // DSA_ROW_GROUPS

    def bisect(it, cs):
        bit = jnp.left_shift(jnp.int32(1), 31 - it)
        out = []
        for gi, c in enumerate(cs):
            cand = c | bit
            keys = key_ref[gi * rg:(gi + 1) * rg, :]
            cnt = jnp.sum((keys >= (cand ^ INT_MIN)).astype(jnp.int32), axis=1, keepdims=True)
            out.append(jnp.where(cnt >= ksel, cand, c))
        return tuple(out)

    cs = lax.fori_loop(0, 32, bisect, tuple(jnp.zeros((rg, 1), jnp.int32) for _ in range(DSA_ROW_GROUPS)))
    thr = jnp.concatenate(cs, axis=0) ^ INT_MIN
    bias_ref[...] = jnp.where(vis, jnp.where(key_ref[...] >= thr, 0.0, NEG_BIG), NEG_BIG)

    kext = kext_ref[:width, :]
    ones_v = jnp.ones((width, AT_HEAD_DIM), BF16)
    for h in range(AT_HEADS):
        hs = slice(h * AT_HEAD_DIM, (h + 1) * AT_HEAD_DIM)
        qa = jnp.concatenate([q_ref[:, hs], qext_ref[:, hs]], axis=1)
        ka = jnp.concatenate([k_ref[:width, hs], kext], axis=1)
        logit = lax.dot_general(qa, ka, _NT, preferred_element_type=F32) + bias_ref[...]
        m = jnp.max(logit, axis=1, keepdims=True)
        p = jnp.exp(logit - m).astype(BF16)
        pv = jnp.dot(p, jnp.concatenate([v_ref[:width, hs], ones_v], axis=1), preferred_element_type=F32)
        o_ref[:, hs] = (pv[:, :AT_HEAD_DIM] / pv[:, AT_HEAD_DIM:AT_HEAD_DIM + 1]).astype(o_ref.dtype)


def _dsa(ziq, ikt, ikw, zat, batch, seq, tq=DSA_TQ):
    nq = seq // tq
    ksel = min(TOPK_MAX, seq // 4)
    col_step = max(seq // DSA_WIDTH_VARIANTS, tq)
    qext, kext = _alibi_tables(seq)
    return pl.pallas_call(
        functools.partial(_dsa_kernel, tq=tq, seq=seq, ksel=ksel, col_step=col_step),
        grid=(batch, nq),
        in_specs=[pl.BlockSpec((tq, IDX_HEADS * IDX_DIM), lambda b, i: (b * nq + i, 0)),
                  pl.BlockSpec((1, IDX_DIM, seq), lambda b, i: (b, 0, 0)),
                  pl.BlockSpec((tq, LANES), lambda b, i: (b * nq + i, 0)),
                  pl.BlockSpec((tq, AT_WIDTH), lambda b, i: (b * nq + i, 0)),
                  pl.BlockSpec((tq, AT_WIDTH), lambda b, i: (i, 0)),
                  pl.BlockSpec((seq, AT_WIDTH), lambda b, i: (b, 1)),
                  pl.BlockSpec((seq, AT_HEAD_DIM), lambda b, i: (0, 0)),
                  pl.BlockSpec((seq, AT_WIDTH), lambda b, i: (b, 2))],
        out_specs=pl.BlockSpec((tq, AT_WIDTH), lambda b, i: (b * nq + i, 0)),
        out_shape=jax.ShapeDtypeStruct((batch * seq, AT_WIDTH), BF16),
        scratch_shapes=[pltpu.VMEM((tq, seq), F32), pltpu.VMEM((tq, seq), jnp.int32),
                        pltpu.VMEM((tq, seq), F32)],
        compiler_params=_params(("parallel", "arbitrary"), VMEM_MIB),
        name="dsa",
    )(ziq, ikt, ikw, zat, qext, zat, kext, zat)


def _merge_kernel(a1_ref, w1_ref, a2_ref, w2_ref, g1_ref, g2_ref, o_ref):
    y1 = jnp.dot(a1_ref[...], w1_ref[...], preferred_element_type=F32)
    y2 = jnp.dot(a2_ref[...], w2_ref[...], preferred_element_type=F32)
    o_ref[...] = (g1_ref[...].astype(F32) * y1 + g2_ref[...].astype(F32) * y2).astype(o_ref.dtype)


def _merge(o_hg, w_hg, o_at, w_at, gates, tm=1024, tn=1024):
    m, k1 = o_hg.shape
    k2 = o_at.shape[1]
    n = w_hg.shape[1]
    tm = min(tm, m)
    nb = n // tn
    return pl.pallas_call(
        _merge_kernel,
        grid=(m // tm, nb),
        in_specs=[pl.BlockSpec((tm, k1), lambda i, j: (i, 0)), pl.BlockSpec((k1, tn), lambda i, j: (0, j)),
                  pl.BlockSpec((tm, k2), lambda i, j: (i, 0)), pl.BlockSpec((k2, tn), lambda i, j: (0, j)),
                  pl.BlockSpec((tm, tn), lambda i, j: (i, j)), pl.BlockSpec((tm, tn), lambda i, j: (i, nb + j))],
        out_specs=pl.BlockSpec((tm, tn), lambda i, j: (i, j)),
        out_shape=jax.ShapeDtypeStruct((m, n), BF16),
        compiler_params=_params(("parallel", "arbitrary"), VMEM_MIB),
        name="merge",
    )(o_hg, w_hg, o_at, w_at, gates, gates)


def _outproj_kernel(m_ref, w_ref, x_ref, gn_ref, wr_ref, br_ref, x1_ref, hp_ref, ri_ref, rw_ref):
    x1 = x_ref[...] + jnp.dot(m_ref[...], w_ref[...], preferred_element_type=F32)
    x1_ref[...] = x1
    hn = _rms(x1, gn_ref[...])
    hp_ref[...] = _pack_bf16_halves(hn)
    h_hi = hn.astype(BF16)
    h_lo = (hn - h_hi.astype(F32)).astype(BF16)
    logit = jnp.dot(jnp.concatenate([h_hi, h_hi, h_lo], axis=1), wr_ref[...],
                    preferred_element_type=F32) + br_ref[...]
    lane = lax.broadcasted_iota(jnp.int32, logit.shape, 1)
    lane_f = lane.astype(F32)
    is_g = lane < N_GROUPS
    lg = jnp.where(is_g, logit, -jnp.inf)
    mg = jnp.max(lg, axis=1, keepdims=True)
    pg_top = 1.0 / jnp.sum(jnp.exp(lg - mg), axis=1, keepdims=True)
    gid = jnp.min(jnp.where(lg == mg, lane_f, float(LANES)), axis=1, keepdims=True).astype(jnp.int32)
    in_grp = ((lane - N_GROUPS) >> 3) == gid
    le = jnp.where(in_grp, logit, -jnp.inf)
    m1 = jnp.max(le, axis=1, keepdims=True)
    i1 = jnp.min(jnp.where(le == m1, lane_f, float(LANES)), axis=1, keepdims=True)
    le2 = jnp.where(lane_f == i1, -jnp.inf, le)
    m2 = jnp.max(le2, axis=1, keepdims=True)
    i2 = jnp.min(jnp.where(le2 == m2, lane_f, float(LANES)), axis=1, keepdims=True)
    r = jnp.exp(m2 - m1)
    w1 = pg_top / (1.0 + r)
    w2 = pg_top * r / (1.0 + r)
    e1 = i1.astype(jnp.int32) - N_GROUPS
    e2 = i2.astype(jnp.int32) - N_GROUPS
    ri_ref[...] = jnp.where(lane == 0, e1, jnp.where(lane == 1, e2, 0))
    rw_ref[...] = jnp.where(lane == 0, w1, jnp.where(lane == 1, w2, 0.0))


def _outproj(mixed, w_out, x, gn, w_r, b_r, tm=256):
    m, d = x.shape
    row = lambda w: pl.BlockSpec((tm, w), lambda i: (i, 0))
    const = lambda a: pl.BlockSpec(a.shape, lambda i: (0, 0), pipeline_mode=pl.Buffered(1))
    gn = gn.reshape(1, d)
    b_r = b_r.reshape(1, LANES)
    return pl.pallas_call(
        _outproj_kernel,
        grid=(m // tm,),
        in_specs=[row(mixed.shape[1]), const(w_out), row(d), const(gn), const(w_r), const(b_r)],
        out_specs=[row(d), row(d // 2), row(LANES), row(LANES)],
        out_shape=[jax.ShapeDtypeStruct((m, d), F32), jax.ShapeDtypeStruct((m, d // 2), jnp.int32),
                   jax.ShapeDtypeStruct((m, LANES), jnp.int32), jax.ShapeDtypeStruct((m, LANES), F32)],
        compiler_params=_params(("parallel",), VMEM_MIB_WHOLE_WEIGHTS),
        name="outproj_router",
    )(mixed, w_out, x, gn, w_r, b_r)


def _expert_kernel(te_ref, tv_ref, src_ref, dst_ref, hn_hbm, wg_ref, wu_ref, wd_ref, y_hbm,
                   xbuf, ybuf, xb_ref, wgb, wub, wdb, gsem, ssem, *, tm):
    t = pl.program_id(0)
    slot = lax.rem(t, 2)
    other = 1 - slot
    valid = tv_ref[t] > 0
    prev_valid = jnp.logical_and(t >= 1, tv_ref[jnp.maximum(t - 1, 0)] > 0)

    def gather_copy(tile, r, s):
        tok = src_ref[tile * tm + r]
        return pltpu.make_async_copy(hn_hbm.at[pl.ds(tok, 1), :], xbuf.at[s, pl.ds(r, 1), :], gsem.at[s])

    def scatter_copy(tile, r, s):
        dst = dst_ref[(tile + 1) * tm + r]
        return pltpu.make_async_copy(ybuf.at[s, pl.ds(r, 1), :], y_hbm.at[pl.ds(dst, 1), :], ssem.at[s])

    def start_rows(copy_fn, tile, s):
        def body(r, carry):
            copy_fn(tile, r, s).start()
            return carry

        lax.fori_loop(0, tm, body, 0, unroll=8)

    def gather_wait(s):
        pltpu.make_async_copy(hn_hbm.at[pl.ds(0, tm), :], xbuf.at[s], gsem.at[s]).wait()

    def scatter_wait(s):
        pltpu.make_async_copy(ybuf.at[s], y_hbm.at[pl.ds(0, tm), :], ssem.at[s]).wait()

    @pl.when(t == 0)
    def _():
        start_rows(gather_copy, 0, 0)
        ybuf[1] = jnp.zeros(ybuf.shape[1:], jnp.int32)

    @pl.when(prev_valid)
    def _():
        scatter_wait(slot)

    @pl.when(valid)
    def _():
        gather_wait(slot)

        @pl.when(jnp.logical_or(t == 0, te_ref[t] != te_ref[jnp.maximum(t - 1, 0)]))
        def _():
            wgb[...] = wg_ref[0].astype(BF16)
            wub[...] = wu_ref[0].astype(BF16)
            wdb[...] = wd_ref[0].astype(BF16)

        xb_ref[...] = _unpack_bf16_halves(xbuf[slot]).astype(BF16)
        for r in range(tm):
            gather_copy(t + 1, r, other).start()
        for r in range(tm):
            scatter_copy(t - 1, r, other).start()
        xb = xb_ref[...]
        g = jnp.dot(xb, wgb[...], preferred_element_type=F32)
        u = jnp.dot(xb, wub[...], preferred_element_type=F32)
        hid = (g * _sigmoid(g) * u).astype(BF16)
        ybuf[slot] = _pack_bf16_halves(jnp.dot(hid, wdb[...], preferred_element_type=F32))

    @pl.when(jnp.logical_and(jnp.logical_not(valid), prev_valid))
    def _():
        gather_wait(slot)
        start_rows(scatter_copy, t - 1, other)
        scatter_wait(other)


def _experts(hn, tile_expert, tile_valid, src, dst, w_g, w_u, w_d, tm):
    n_tok, dp = hn.shape
    d = 2 * dp
    n_steps = tile_expert.shape[0]
    wspec = lambda a: pl.BlockSpec((1,) + a.shape[1:], lambda t, te, tv, sr, ds: (te[t], 0, 0))
    return pl.pallas_call(
        functools.partial(_expert_kernel, tm=tm),
        grid_spec=pltpu.PrefetchScalarGridSpec(
            num_scalar_prefetch=4,
            grid=(n_steps,),
            in_specs=[pl.BlockSpec(memory_space=pl.ANY), wspec(w_g), wspec(w_u), wspec(w_d)],
            out_specs=pl.BlockSpec(memory_space=pl.ANY),
            scratch_shapes=[pltpu.VMEM((2, tm, dp), jnp.int32), pltpu.VMEM((2, tm, dp), jnp.int32),
                            pltpu.VMEM((tm, d), BF16),
                            pltpu.VMEM(w_g.shape[1:], BF16), pltpu.VMEM(w_u.shape[1:], BF16),
                            pltpu.VMEM(w_d.shape[1:], BF16),
                            pltpu.SemaphoreType.DMA((2,)), pltpu.SemaphoreType.DMA((2,))]),
        out_shape=jax.ShapeDtypeStruct((2 * n_tok + tm, dp), jnp.int32),
        compiler_params=_params(("arbitrary",), VMEM_MIB_WHOLE_WEIGHTS),
        name="experts",
    )(tile_expert, tile_valid, src, dst, hn, w_g, w_u, w_d)


def _dispatch_plan(route_e, tm):
    n_tok = route_e.shape[0]
    n_pairs = 2 * n_tok
    n_tiles = n_pairs // tm + N_EXPERTS + 1
    eflat = route_e.T.reshape(-1)
    order = jnp.argsort(eflat, stable=True).astype(jnp.int32)
    counts = jnp.sum(eflat[:, None] == jnp.arange(N_EXPERTS, dtype=jnp.int32)[None, :], axis=0, dtype=jnp.int32)
    tiles_e = (counts + tm - 1) // tm
    tile_end = jnp.cumsum(tiles_e)
    tile_off = tile_end - tiles_e
    start = jnp.cumsum(counts) - counts
    tid = jnp.arange(n_tiles, dtype=jnp.int32)
    te = jnp.minimum(jnp.sum(tile_end[None, :] <= tid[:, None], axis=1), N_EXPERTS - 1).astype(jnp.int32)
    tv = jnp.clip(counts[te] - (tid - tile_off[te]) * tm, 0, tm)
    used = tid < tile_end[-1]
    last_e = te[jnp.maximum(tile_end[-1] - 1, 0)]
    te = jnp.where(used, te, last_e).astype(jnp.int32)
    tv = jnp.where(used, tv, 0).astype(jnp.int32)
    r = jnp.arange(tm, dtype=jnp.int32)[None, :]
    src = (start[te] + (tid - tile_off[te]) * tm)[:, None] + r
    dst = jnp.where(r < tv[:, None], order[jnp.clip(src, 0, n_pairs - 1)], n_pairs + r).reshape(-1).astype(jnp.int32)
    spare = n_pairs + jnp.arange(tm, dtype=jnp.int32)
    return te, tv, dst % n_tok, jnp.concatenate([spare, dst])


def _final_kernel(x1_ref, y0_ref, y1_ref, rw_ref, p_ref, wgate_ref, wproj_ref, gple_ref, gfin_ref, o_ref,
                  *, last_layer):
    rw = rw_ref[...]
    x2 = (x1_ref[...] + rw[:, 0:1] * _unpack_bf16_halves(y0_ref[...])
          + rw[:, 1:2] * _unpack_bf16_halves(y1_ref[...]))
    hn = _rms(x2, gple_ref[...]).astype(BF16)
    gate = _sigmoid(jnp.dot(hn, wgate_ref[...], preferred_element_type=F32))
    emb = jnp.dot(p_ref[...].astype(BF16), wproj_ref[...], preferred_element_type=F32)
    x3 = x2 + gate * emb
    o_ref[...] = _rms(x3, gfin_ref[...]) if last_layer else x3


def _final(x1, y, rw, p, w_gate, w_proj, g_ple, g_fin, last_layer, tm=512):
    m, d = x1.shape
    nb = m // tm
    row = lambda w: pl.BlockSpec((tm, w), lambda i: (i, 0))
    const = lambda a: pl.BlockSpec(a.shape, lambda i: (0, 0), pipeline_mode=pl.Buffered(1))
    g_ple, g_fin = g_ple.reshape(1, d), g_fin.reshape(1, d)
    return pl.pallas_call(
        functools.partial(_final_kernel, last_layer=last_layer),
        grid=(nb,),
        in_specs=[row(d), row(d // 2), pl.BlockSpec((tm, d // 2), lambda i: (nb + i, 0)), row(LANES), row(p.shape[1]),
                  const(w_gate), const(w_proj), const(g_ple), const(g_fin)],
        out_specs=row(d),
        out_shape=jax.ShapeDtypeStruct((m, d), F32),
        compiler_params=_params(("parallel",), VMEM_MIB_WHOLE_WEIGHTS),
        name="final",
    )(x1, y, y, rw, p, w_gate, w_proj, g_ple, g_fin)


def kernel(x, p, attn_norm, w_in, b_in, hg_lb_logits, hg_out_norm, w_hg_up, w_attn_up, w_out, ffn_norm,
           w_router_group, b_router_group, w_router_expert, b_router_expert, w_exp_gate, w_exp_up,
           w_exp_down, ple_norm, w_ple_gate, w_ple_proj, final_norm):
    batch, seq, d = x.shape
    n = batch * seq
    depth = w_in.shape[0]
    o_hf, o_at = 0, 4 * HG_WIDTH
    o_iq = o_at + 3 * AT_WIDTH
    o_ik = o_iq + IDX_HEADS * IDX_DIM
    o_iw = o_ik + IDX_DIM
    o_g = o_iw + IDX_HEADS
    lbs = jnp.cumsum(jax.nn.softmax(hg_lb_logits.astype(F32), axis=0), axis=0)
    xf = x.reshape(n, d)
    moe_tm = min(128, n // 2)
    for li in range(depth):
        wt, b = w_in[li].T.astype(BF16), b_in[li]
        h = _rmsnorm(xf, attn_norm[li], BF16)
        zhg = _matmul(h, wt, o_hf, o_at - o_hf, b[o_hf:o_at], F32, name="proj_hgrn")
        qkv_scale = jnp.concatenate([jnp.full((AT_WIDTH,), AT_HEAD_DIM ** -0.5, F32), jnp.ones((2 * AT_WIDTH,), F32)])
        zat = _matmul(h, wt, o_at, o_iq - o_at, b[o_at:o_iq] * qkv_scale, BF16, scale=qkv_scale, name="proj_attn")
        ziq = _matmul(h, wt, o_iq, o_ik - o_iq, b[o_iq:o_ik], F32, name="proj_iq")
        zikw = _matmul(h, wt, o_ik, LANES, b[o_ik:o_ik + LANES], F32, name="proj_ikw")
        gates = _matmul(h, wt, o_g, 2 * d, b[o_g:], BF16, act="sigmoid", name="proj_gates")

        o_hg = _hgrn(zhg, lbs[li], hg_out_norm[li], batch, seq)
        ikt = zikw[:, :IDX_DIM].reshape(batch, seq, IDX_DIM).transpose(0, 2, 1)
        o_att = _dsa(ziq, ikt, zikw, zat, batch, seq)
        mixed = _merge(o_hg, w_hg_up[li].astype(BF16), o_att, w_attn_up[li].astype(BF16), gates)

        w_r = jnp.pad(jnp.concatenate([w_router_group[li], w_router_expert[li]], axis=1),
                      ((0, 0), (0, LANES - N_GROUPS - N_EXPERTS)))
        b_r = jnp.pad(jnp.concatenate([b_router_group[li], b_router_expert[li]]), (0, LANES - N_GROUPS - N_EXPERTS))
        w_r_hi = w_r.astype(BF16)
        w_r_lo = (w_r - w_r_hi.astype(F32)).astype(BF16)
        w_r = jnp.concatenate([w_r_hi, w_r_lo, w_r_hi], axis=0)
        x1, hn, route_e, route_w = _outproj(mixed, w_out[li].astype(BF16), xf, ffn_norm[li], w_r, b_r)
        tile_e, tile_v, src, dst = _dispatch_plan(route_e[:, :2], moe_tm)
        y = _experts(hn, tile_e, tile_v, src, dst, w_exp_gate[li], w_exp_up[li], w_exp_down[li], moe_tm)
        xf = _final(x1, y, route_w, p[li].reshape(n, -1), w_ple_gate[li].astype(BF16),
                    w_ple_proj[li].astype(BF16), ple_norm[li], final_norm, li == depth - 1)
    return xf.reshape(batch, seq, d)
```
